```python
import jax, jax.numpy as jnp
from jax import lax
import numpy as np

D_MODEL = 1024
BATCH = 4
SEQ = 4096
DEPTH = 1

SSD_HEADS = 8
SSD_HEAD_DIM = 64
SSD_D = SSD_HEADS * SSD_HEAD_DIM
SSD_GROUPS = 2
SSD_STATE = 128
SSD_CONV = 4
SSD_CHUNK = 128
SSD_CONV_DIM = SSD_D + 2 * SSD_GROUPS * SSD_STATE
SSD_COLS = SSD_D + SSD_CONV_DIM + SSD_HEADS

RWKV_HEADS = 8
RWKV_HEAD_DIM = 64
RWKV_D = RWKV_HEADS * RWKV_HEAD_DIM
DECAY_LORA = 64
AAA_LORA = 64
GATE_LORA = 128
RWKV_COLS = 3 * RWKV_D + DECAY_LORA + AAA_LORA + GATE_LORA
RWKV_GN_EPS = 64e-5

D_MIX = SSD_D + RWKV_D
D_IN = SSD_COLS + RWKV_COLS

D_FF = 2816
FFN_CONV = 3
NORM_EPS = 1e-6

kernel_name = "hymba_ssd_rwkv7_sandwich_convglu"


def _rms_norm(x, g):
    xf = x.astype(jnp.float32)
    y = xf * lax.rsqrt(jnp.mean(xf * xf, axis=-1, keepdims=True) + NORM_EPS)
    return (y * g.astype(jnp.float32)).astype(x.dtype)


def _causal_dwconv(x, w, b):
    K, C = w.shape
    y = lax.conv_general_dilated(
        x, w[:, None, :].astype(x.dtype), window_strides=(1,), padding=[(K - 1, 0)],
        dimension_numbers=("NWC", "WIO", "NWC"), feature_group_count=C)
    return y + b.astype(x.dtype)


def _token_shift(x):
    return jnp.pad(x, ((0, 0), (1, 0), (0, 0)))[:, :-1]


def _segsum(a):
    cs = jnp.cumsum(a, axis=-1)
    diff = cs[..., :, None] - cs[..., None, :]
    L = a.shape[-1]
    mask = jnp.tril(jnp.ones((L, L), dtype=bool))
    return jnp.where(mask, diff, -jnp.inf)


def _ssd_chunked(xh, dt, A, Bg, Cg):
    b, T, h, p = xh.shape
    g, n = Bg.shape[2], Bg.shape[3]
    c, L = T // SSD_CHUNK, SSD_CHUNK
    Bh = jnp.repeat(Bg, h // g, axis=2).reshape(b, c, L, h, n)
    Ch = jnp.repeat(Cg, h // g, axis=2).reshape(b, c, L, h, n)
    X = (xh * dt[..., None]).reshape(b, c, L, h, p)
    a = (dt * A).reshape(b, c, L, h).transpose(0, 3, 1, 2)
    a_cs = jnp.cumsum(a, axis=-1)
    decay_in = jnp.exp(_segsum(a))
    scores = jnp.einsum("bclhn,bcshn->bhcls", Ch, Bh)
    y_diag = jnp.einsum("bhcls,bcshp->bclhp", scores * decay_in, X)
    decay_states = jnp.exp(a_cs[..., -1:] - a_cs)
    states = jnp.einsum("bclhn,bhcl,bclhp->bchpn", Bh, decay_states, X)
    chunk_decay = jnp.pad(a_cs[..., -1], ((0, 0), (0, 0), (1, 0)))
    decay_chunk = jnp.exp(_segsum(chunk_decay))
    states_p = jnp.concatenate([jnp.zeros_like(states[:, :1]), states], axis=1)
    states_in = jnp.einsum("bhzc,bchpn->bzhpn", decay_chunk, states_p)[:, :-1]
    y_off = jnp.einsum("bclhn,bchpn,bhcl->bclhp", Ch, states_in, jnp.exp(a_cs))
    return (y_diag + y_off).reshape(b, T, h, p)


def _ssd_mixer(p, conv_w, conv_b, dt_bias, a_log, d_skip, norm_g):
    b, T, _ = p.shape
    z, xbc, dt = jnp.split(p, [SSD_D, SSD_D + SSD_CONV_DIM], axis=-1)
    xbc = jax.nn.silu(_causal_dwconv(xbc, conv_w.astype(jnp.float32), conv_b.astype(jnp.float32)))
    xs, Bm, Cm = jnp.split(xbc, [SSD_D, SSD_D + SSD_GROUPS * SSD_STATE], axis=-1)
    xh = xs.reshape(b, T, SSD_HEADS, SSD_HEAD_DIM)
    Bm = Bm.reshape(b, T, SSD_GROUPS, SSD_STATE)
    Cm = Cm.reshape(b, T, SSD_GROUPS, SSD_STATE)
    dt = jax.nn.softplus(dt + dt_bias.astype(jnp.float32))
    A = -jnp.exp(a_log.astype(jnp.float32))
    y = _ssd_chunked(xh, dt, A, Bm, Cm) + d_skip.astype(jnp.float32)[:, None] * xh
    y = y.reshape(b, T, SSD_D) * jax.nn.silu(z)
    y = y.reshape(b, T, SSD_GROUPS, SSD_D // SSD_GROUPS)
    y = y * lax.rsqrt(jnp.mean(y * y, axis=-1, keepdims=True) + NORM_EPS)
    return y.reshape(b, T, SSD_D) * norm_g.astype(jnp.float32)


def _rwkv7_scan(r, w, k, v, za, zb):
    b, T, h, n = r.shape

    def step(S, inp):
        r_t, w_t, k_t, v_t, a_t, b_t = inp
        sa = jnp.einsum("bhij,bhj->bhi", S, a_t)
        S = S * w_t[:, :, None, :] + sa[..., None] * b_t[:, :, None, :] + v_t[..., None] * k_t[:, :, None, :]
        return S, jnp.einsum("bhij,bhj->bhi", S, r_t)

    xs = tuple(jnp.moveaxis(t, 1, 0) for t in (r, w, k, v, za, zb))
    S0 = jnp.zeros((b, h, n, n), dtype=r.dtype)
    _, ys = lax.scan(step, S0, xs)
    return jnp.moveaxis(ys, 0, 1)


def _rwkv7_mixer(p, mu, w0, w2, a0, a2, g2, k_k, k_a, r_k, ln_w, ln_b):
    b, T, _ = p.shape
    f32 = lambda t: t.astype(jnp.float32)
    p = p + (_token_shift(p) - p) * f32(mu)
    i1 = RWKV_D
    i2 = i1 + DECAY_LORA
    i3 = i2 + RWKV_D
    i4 = i3 + RWKV_D
    i5 = i4 + AAA_LORA
    r, w_lo, k, v, a_lo, g_lo = jnp.split(p, [i1, i2, i3, i4, i5], axis=-1)
    w_log = -jax.nn.softplus(-(f32(w0) + jnp.tanh(w_lo) @ f32(w2))) - 0.5
    decay = jnp.exp(-jnp.exp(w_log))
    a = jax.nn.sigmoid(f32(a0) + a_lo @ f32(a2))
    g = jax.nn.sigmoid(g_lo) @ f32(g2)
    heads = lambda t: t.reshape(b, T, RWKV_HEADS, RWKV_HEAD_DIM)
    kk = heads(k * f32(k_k))
    kk = kk / jnp.maximum(jnp.sqrt(jnp.sum(kk * kk, axis=-1, keepdims=True)), 1e-12)
    k = k * (1.0 + (a - 1.0) * f32(k_a))
    rh, kh, vh, ah = heads(r), heads(k), heads(v), heads(a)
    y = _rwkv7_scan(rh, heads(decay), kh, vh, -kk, kk * ah)
    mean = jnp.mean(y, axis=-1, keepdims=True)
    var = jnp.mean(jnp.square(y - mean), axis=-1, keepdims=True)
    y = ((y - mean) * lax.rsqrt(var + RWKV_GN_EPS)).reshape(b, T, RWKV_D) * f32(ln_w) + f32(ln_b)
    bonus = jnp.sum(rh * kh * f32(r_k), axis=-1, keepdims=True) * vh
    return (y + bonus.reshape(b, T, RWKV_D)) * g


def setup_inputs(seed: int = 0) -> dict:
    key = jax.random.key(seed)
    ks = jax.random.split(key, 32)
    nrm = lambda k, shape, s: jax.random.normal(k, shape, jnp.float32) * s
    L = DEPTH
    dt = jnp.exp(jax.random.uniform(ks[5], (L, SSD_HEADS), jnp.float32) * (np.log(0.1) - np.log(0.001)) + np.log(0.001))
    dt = jnp.clip(dt, 1e-4)
    return {
        "x": nrm(ks[0], (BATCH, SEQ, D_MODEL), 1.0),
        "pre_mix_norm": 1.0 + nrm(ks[1], (L, D_MODEL), 0.05),
        "w_in": nrm(ks[2], (L, D_MODEL, D_IN), D_MODEL ** -0.5),
        "ssd_conv_w": nrm(ks[3], (L, SSD_CONV, SSD_CONV_DIM), SSD_CONV ** -0.5),
        "ssd_conv_b": nrm(ks[4], (L, SSD_CONV_DIM), 0.02),
        "ssd_dt_bias": dt + jnp.log(-jnp.expm1(-dt)),
        "ssd_a_log": jnp.log(jax.random.uniform(ks[6], (L, SSD_HEADS), jnp.float32, 1.0, 16.0)),
        "ssd_d": 1.0 + nrm(ks[7], (L, SSD_HEADS), 0.1),
        "ssd_norm": 1.0 + nrm(ks[8], (L, SSD_D), 0.05),
        "rwkv_mu": jax.random.uniform(ks[9], (L, RWKV_COLS), jnp.float32),
        "rwkv_w0": jax.random.uniform(ks[10], (L, RWKV_D), jnp.float32, -6.0, 1.0),
        "rwkv_w2": nrm(ks[11], (L, DECAY_LORA, RWKV_D), 0.1 * DECAY_LORA ** -0.5),
        "rwkv_a0": nrm(ks[12], (L, RWKV_D), 0.1),
        "rwkv_a2": nrm(ks[13], (L, AAA_LORA, RWKV_D), 0.5 * AAA_LORA ** -0.5),
        "rwkv_g2": nrm(ks[14], (L, GATE_LORA, RWKV_D), GATE_LORA ** -0.5),
        "rwkv_k_k": 0.85 + nrm(ks[15], (L, RWKV_D), 0.05),
        "rwkv_k_a": 1.0 + nrm(ks[16], (L, RWKV_D), 0.05),
        "rwkv_r_k": nrm(ks[17], (L, RWKV_HEADS, RWKV_HEAD_DIM), 0.1),
        "rwkv_ln_w": 1.0 + nrm(ks[18], (L, RWKV_D), 0.05),
        "rwkv_ln_b": nrm(ks[19], (L, RWKV_D), 0.02),
        "w_out": nrm(ks[20], (L, D_MIX, D_MODEL), D_MIX ** -0.5),
        "post_mix_norm": 1.0 + nrm(ks[21], (L, D_MODEL), 0.05),
        "pre_ffn_norm": 1.0 + nrm(ks[22], (L, D_MODEL), 0.05),
        "ffn_w_up": nrm(ks[23], (L, D_MODEL, 2 * D_FF), D_MODEL ** -0.5),
        "ffn_conv_w": nrm(ks[24], (L, FFN_CONV, 2 * D_FF), FFN_CONV ** -0.5),
        "ffn_conv_b": nrm(ks[25], (L, 2 * D_FF), 0.02),
        "ffn_w_down": nrm(ks[26], (L, D_FF, D_MODEL), D_FF ** -0.5),
        "post_ffn_norm": 1.0 + nrm(ks[27], (L, D_MODEL), 0.05),
    }


def reference(x, pre_mix_norm, w_in, ssd_conv_w, ssd_conv_b, ssd_dt_bias, ssd_a_log, ssd_d, ssd_norm,
              rwkv_mu, rwkv_w0, rwkv_w2, rwkv_a0, rwkv_a2, rwkv_g2, rwkv_k_k, rwkv_k_a, rwkv_r_k,
              rwkv_ln_w, rwkv_ln_b, w_out, post_mix_norm, pre_ffn_norm, ffn_w_up, ffn_conv_w,
              ffn_conv_b, ffn_w_down, post_ffn_norm):
    h = x
    for l in range(DEPTH):
        xn = _rms_norm(h, pre_mix_norm[l])
        proj = (xn @ w_in[l]).astype(jnp.float32)
        p_ssd, p_rwkv = jnp.split(proj, [SSD_COLS], axis=-1)
        y_ssd = _ssd_mixer(p_ssd, ssd_conv_w[l], ssd_conv_b[l], ssd_dt_bias[l], ssd_a_log[l],
                           ssd_d[l], ssd_norm[l])
        y_rwkv = _rwkv7_mixer(p_rwkv, rwkv_mu[l], rwkv_w0[l], rwkv_w2[l], rwkv_a0[l], rwkv_a2[l],
                              rwkv_g2[l], rwkv_k_k[l], rwkv_k_a[l], rwkv_r_k[l], rwkv_ln_w[l], rwkv_ln_b[l])
        mix = jnp.concatenate([y_ssd, y_rwkv], axis=-1).astype(h.dtype) @ w_out[l]
        h = h + _rms_norm(mix, post_mix_norm[l])
        hn = _rms_norm(h, pre_ffn_norm[l])
        u = _causal_dwconv(hn @ ffn_w_up[l], ffn_conv_w[l], ffn_conv_b[l])
        gate, val = jnp.split(u, 2, axis=-1)
        f = (jax.nn.silu(gate) * val) @ ffn_w_down[l]
        h = h + _rms_norm(f, post_ffn_norm[l])
    return h
```

```python
import functools

import jax
import jax.numpy as jnp
from jax import lax
from jax.experimental import pallas as pl
from jax.experimental.pallas import tpu as pltpu

F32 = jnp.float32
BF16 = jnp.bfloat16

D_MODEL = 1024
SSD_HEADS = 8
SSD_HEAD_DIM = 64
SSD_D = SSD_HEADS * SSD_HEAD_DIM
SSD_GROUPS = 2
SSD_STATE = 128
SSD_CONV = 4
SSD_CHUNK = 128
SSD_CONV_DIM = SSD_D + 2 * SSD_GROUPS * SSD_STATE
SSD_COLS = SSD_D + SSD_CONV_DIM + SSD_HEADS
RWKV_HEADS = 8
RWKV_HEAD_DIM = 64
RWKV_D = RWKV_HEADS * RWKV_HEAD_DIM
DECAY_LORA = 64
AAA_LORA = 64
GATE_LORA = 128
RWKV_COLS = 3 * RWKV_D + DECAY_LORA + AAA_LORA + GATE_LORA
RWKV_GN_EPS = 64e-5
D_FF = 2816
FFN_CONV = 3
NORM_EPS = 1e-6

LANES = 128
SUBLANES = 8
SSD_PCOLS = SSD_D + SSD_CONV_DIM + LANES
RWKV_CHUNK = 64
PAIR = 2 * RWKV_HEAD_DIM
FFN_COLS = 256
VMEM_LIMIT = 56 * 1024 * 1024

NN = (((1,), (0,)), ((), ()))
NT = (((1,), (1,)), ((), ()))
TN = (((0,), (0,)), ((), ()))


def _mm(a, b, dims=NN):
    return lax.dot_general(a, b, dims, preferred_element_type=F32)


def _dot1(a, b, dims=NN):
    return _mm(a.astype(BF16), b.astype(BF16), dims)


def _split2(a):
    hi = a.astype(BF16)
    lo = (a - hi.astype(F32)).astype(BF16)
    return hi, lo


def _dot3(a, b, dims=NN):
    ah, al = _split2(a)
    bh, bl = _split2(b)
    return _mm(ah, bh, dims) + (_mm(ah, bl, dims) + _mm(al, bh, dims))


def _dot_sel(sel_bf16, a):
    a1 = a.astype(BF16)
    r1 = a - a1.astype(F32)
    a2 = r1.astype(BF16)
    a3 = (r1 - a2.astype(F32)).astype(BF16)
    return _mm(sel_bf16, a1) + (_mm(sel_bf16, a2) + _mm(sel_bf16, a3))


def _a_dot_sel(a, sel_bf16):
    a1 = a.astype(BF16)
    r1 = a - a1.astype(F32)
    a2 = r1.astype(BF16)
    a3 = (r1 - a2.astype(F32)).astype(BF16)
    return _mm(a1, sel_bf16) + (_mm(a2, sel_bf16) + _mm(a3, sel_bf16))


def _rms(x, g):
    return x * lax.rsqrt(jnp.mean(x * x, axis=-1, keepdims=True) + NORM_EPS) * g


def _sigmoid(x):
    return 1.0 / (1.0 + jnp.exp(-x))


def _softplus(x):
    return jnp.maximum(x, 0.0) + jnp.log(1.0 + jnp.exp(-jnp.abs(x)))


def _iota2(shape, axis):
    return lax.broadcasted_iota(jnp.int32, shape, axis)


def _inproj_kernel(x_ref, g_ref, ws_ref, wr_ref, ps_ref, pr_ref):
    xb = _rms(x_ref[...], g_ref[...]).astype(BF16)
    ps_ref[...] = _mm(xb, ws_ref[...])
    pr_ref[...] = _mm(xb, wr_ref[...])


def _inproj(x2, g, w_ssd, w_rwkv, tm):
    n = x2.shape[0]
    const = lambda i: (0, 0)
    return pl.pallas_call(
        _inproj_kernel,
        name="inproj",
        grid=(n // tm,),
        in_specs=[
            pl.BlockSpec((tm, D_MODEL), lambda i: (i, 0)),
            pl.BlockSpec((1, D_MODEL), const),
            pl.BlockSpec((D_MODEL, SSD_PCOLS), const, pipeline_mode=pl.Buffered(1)),
            pl.BlockSpec((D_MODEL, RWKV_COLS), const, pipeline_mode=pl.Buffered(1)),
        ],
        out_specs=[
            pl.BlockSpec((tm, SSD_PCOLS), lambda i: (i, 0)),
            pl.BlockSpec((tm, RWKV_COLS), lambda i: (i, 0)),
        ],
        out_shape=[
            jax.ShapeDtypeStruct((n, SSD_PCOLS), F32),
            jax.ShapeDtypeStruct((n, RWKV_COLS), F32),
        ],
        compiler_params=pltpu.CompilerParams(
            dimension_semantics=("arbitrary",), vmem_limit_bytes=VMEM_LIMIT),
    )(x2, g, w_ssd, w_rwkv)


def _ssd_kernel(p_ref, cw_ref, cb_ref, dtb_ref, alog_ref, dsk_ref, ng_ref, o_ref, buf_ref, st_ref):
    L = SSD_CHUNK
    gw = SSD_D // SSD_GROUPS

    @pl.when(pl.program_id(1) == 0)
    def _():
        buf_ref[0:SUBLANES, :] = jnp.zeros((SUBLANES, SSD_CONV_DIM), F32)
        st_ref[...] = jnp.zeros(st_ref.shape, F32)

    z = p_ref[:, 0:SSD_D]
    xbc_raw = p_ref[:, SSD_D:SSD_D + SSD_CONV_DIM]
    dt_raw = p_ref[:, SSD_D + SSD_CONV_DIM:SSD_PCOLS]

    buf_ref[SUBLANES:SUBLANES + L, :] = xbc_raw
    acc = cb_ref[...] + cw_ref[0:1, :] * buf_ref[SUBLANES - 3:SUBLANES - 3 + L, :]
    for k in range(1, SSD_CONV):
        off = SUBLANES - (SSD_CONV - 1) + k
        acc = acc + cw_ref[k:k + 1, :] * buf_ref[off:off + L, :]
    buf_ref[0:SUBLANES, :] = xbc_raw[L - SUBLANES:L, :]
    xbc = acc * _sigmoid(acc)
    xs = xbc[:, 0:SSD_D]

    dt = _softplus(dt_raw + dtb_ref[...])
    a = dt * (-jnp.exp(alog_ref[...]))
    row = _iota2((L, L), 0)
    col = _iota2((L, L), 1)
    causal = row >= col
    tril = jnp.where(causal, 1.0, 0.0).astype(BF16)
    a_cs = _dot_sel(tril, a)
    a_cs_t = a_cs.T
    a_last = a_cs[L - 1:L, :]

    hsel = (_iota2((LANES, SSD_D), 1) // SSD_HEAD_DIM == _iota2((LANES, SSD_D), 0))
    hsel = jnp.where(hsel, 1.0, 0.0).astype(BF16)
    dt_e = _a_dot_sel(dt, hsel)
    ea_e = _a_dot_sel(jnp.exp(a_cs), hsel)
    ds_e = _a_dot_sel(jnp.exp(a_last - a_cs), hsel)

    x_dt = xs * dt_e
    x_b = x_dt.astype(BF16)
    x_dec = (x_dt * ds_e).astype(BF16)
    lane_lo = _iota2((L, LANES), 1) < SSD_HEAD_DIM

    for g in range(SSD_GROUPS):
        b_g = xbc[:, SSD_D + g * SSD_STATE:SSD_D + (g + 1) * SSD_STATE].astype(BF16)
        c_off = SSD_D + SSD_GROUPS * SSD_STATE
        c_g = xbc[:, c_off + g * SSD_STATE:c_off + (g + 1) * SSD_STATE].astype(BF16)
        scores = _mm(c_g, b_g, NT)
        state = st_ref[g]
        y_off = _mm(c_g, state.astype(BF16)) * ea_e[:, g * gw:(g + 1) * gw]
        st_ref[g] = state * ea_e[L - 1:L, g * gw:(g + 1) * gw] + _mm(b_g, x_dec[:, g * gw:(g + 1) * gw], TN)
        y_parts = []
        for j in range(gw // LANES):
            h0 = g * (SSD_HEADS // SSD_GROUPS) + 2 * j
            ms = []
            for h in (h0, h0 + 1):
                seg = a_cs[:, h:h + 1] - a_cs_t[h:h + 1, :]
                dec = jnp.exp(jnp.where(causal, seg, -jnp.inf))
                ms.append((scores * dec).astype(BF16))
            xp = x_b[:, h0 * SSD_HEAD_DIM:h0 * SSD_HEAD_DIM + LANES]
            zero = jnp.zeros_like(xp)
            x_bd = jnp.concatenate([jnp.where(lane_lo, xp, zero), jnp.where(lane_lo, zero, xp)], axis=0)
            y_parts.append(_mm(jnp.concatenate(ms, axis=1), x_bd))
        y = jnp.concatenate(y_parts, axis=1) + y_off
        y = y + dsk_ref[:, g * gw:(g + 1) * gw] * xs[:, g * gw:(g + 1) * gw]
        zg = z[:, g * gw:(g + 1) * gw]
        y = y * (zg * _sigmoid(zg))
        y = y * lax.rsqrt(jnp.mean(y * y, axis=-1, keepdims=True) + NORM_EPS)
        o_ref[:, g * gw:(g + 1) * gw] = y * ng_ref[:, g * gw:(g + 1) * gw]


def _ssd(p_ssd, cw, cb, dtb, alog, dsk, ng, batch, seq):
    L = SSD_CHUNK
    nc = seq // L
    const = lambda b, c: (0, 0)
    return pl.pallas_call(
        _ssd_kernel,
        name="ssd",
        grid=(batch, nc),
        in_specs=[
            pl.BlockSpec((L, SSD_PCOLS), lambda b, c: (b * nc + c, 0)),
            pl.BlockSpec((SSD_CONV, SSD_CONV_DIM), const),
            pl.BlockSpec((1, SSD_CONV_DIM), const),
            pl.BlockSpec((1, LANES), const),
            pl.BlockSpec((1, LANES), const),
            pl.BlockSpec((1, SSD_D), const),
            pl.BlockSpec((1, SSD_D), const),
        ],
        out_specs=pl.BlockSpec((L, SSD_D), lambda b, c: (b * nc + c, 0)),
        out_shape=jax.ShapeDtypeStruct((batch * seq, SSD_D), F32),
        scratch_shapes=[
            pltpu.VMEM((SUBLANES + L, SSD_CONV_DIM), F32),
            pltpu.VMEM((SSD_GROUPS, SSD_STATE, SSD_D // SSD_GROUPS), F32),
        ],
        compiler_params=pltpu.CompilerParams(
            dimension_semantics=("arbitrary", "arbitrary"), vmem_limit_bytes=VMEM_LIMIT),
    )(p_ssd, cw, cb, dtb, alog, dsk, ng)


def _seg_sum(x, lane_lo):
    s_lo = jnp.sum(jnp.where(lane_lo, x, 0.0), axis=-1, keepdims=True)
    s_hi = jnp.sum(jnp.where(lane_lo, 0.0, x), axis=-1, keepdims=True)
    return jnp.where(lane_lo, s_lo, s_hi)


def _stack_heads(x, lane_lo):
    zero = jnp.zeros_like(x)
    return jnp.concatenate([jnp.where(lane_lo, x, zero), jnp.where(lane_lo, zero, x)], axis=0)


def _unit_lower_inverse(a_strict, row, col, eye):
    a8 = jnp.where(row // 8 == col // 8, a_strict, 0.0)
    t = eye + a8
    a2 = _dot3(a8, a8)
    t = t + _dot3(t, a2)
    a4 = _dot3(a2, a2)
    t = t + _dot3(t, a4)
    for s in (8, 16, 32):
        lower_left = (row // (2 * s) == col // (2 * s)) & ((row // s) % 2 == 1) & ((col // s) % 2 == 0)
        a21 = jnp.where(lower_left, a_strict, 0.0)
        t = t + _dot3(_dot3(t, a21), t)
    return t


def _rwkv_kernel(p_ref, mu_ref, w0_ref, w2_ref, a0_ref, a2_ref, g2_ref, kk_ref, ka_ref, rk_ref,
                 lnw_ref, lnb_ref, o_ref, buf_ref, st_ref):
    L = RWKV_CHUNK
    H2 = 2 * L

    @pl.when(pl.program_id(1) == 0)
    def _():
        buf_ref[0:SUBLANES, :] = jnp.zeros((SUBLANES, RWKV_COLS), F32)
        st_ref[...] = jnp.zeros(st_ref.shape, F32)

    p = p_ref[...]
    buf_ref[SUBLANES:SUBLANES + L, :] = p
    prev = buf_ref[SUBLANES - 1:SUBLANES - 1 + L, :]
    buf_ref[0:SUBLANES, :] = p[L - SUBLANES:L, :]
    pm = p + (prev - p) * mu_ref[...]

    D = RWKV_D
    r = pm[:, 0:D]
    k = pm[:, D:2 * D]
    v = pm[:, 2 * D:3 * D]
    wa = pm[:, 3 * D:3 * D + LANES]
    g_lo = pm[:, 3 * D + LANES:3 * D + 2 * LANES]

    w_log = -_softplus(-(w0_ref[...] + _dot1(jnp.tanh(wa), w2_ref[...]))) - 0.5
    lw = -jnp.exp(w_log)
    alr = _sigmoid(a0_ref[...] + _dot1(wa, a2_ref[...]))
    gate = _dot1(_sigmoid(g_lo), g2_ref[...])
    kk = k * kk_ref[...]
    k2 = k * (1.0 + (alr - 1.0) * ka_ref[...])

    tril = jnp.where(_iota2((L, L), 0) >= _iota2((L, L), 1), 1.0, 0.0).astype(BF16)
    cs = _dot_sel(tril, lw)

    lane_lo = _iota2((L, PAIR), 1) < RWKV_HEAD_DIM
    row = _iota2((H2, H2), 0)
    col = _iota2((H2, H2), 1)
    eye = jnp.where(row == col, 1.0, 0.0)
    same_head = row // L == col // L
    strict = same_head & (row % L > col % L)
    incl = same_head & (row % L >= col % L)
    strict2 = jnp.concatenate([strict, strict], axis=1)
    incl2 = jnp.concatenate([incl, incl], axis=1)

    for q in range(RWKV_D // PAIR):
        sl = slice(q * PAIR, (q + 1) * PAIR)
        cs_p = cs[:, sl]
        lw_p = lw[:, sl]
        cs_last = cs_p[L - 1:L, :]
        p_in = jnp.exp(cs_p)
        p_ex = jnp.exp(cs_p - lw_p)
        p_inv = jnp.exp(-cs_p)
        p_end = jnp.exp(cs_last - cs_p)
        p_all = jnp.exp(cs_last)

        kk_p = kk[:, sl]
        kk_n = kk_p / jnp.maximum(jnp.sqrt(_seg_sum(kk_p * kk_p, lane_lo)), 1e-12)
        b_p = kk_n * alr[:, sl]
        r_p = r[:, sl]
        k_p = k2[:, sl]
        v_p = v[:, sl]

        lhs_a = _stack_heads(-kk_n * p_ex, lane_lo)
        lhs_r = _stack_heads(r_p * p_in, lane_lo)
        rhs_bk = jnp.concatenate([_stack_heads(b_p * p_inv, lane_lo),
                                  _stack_heads(k_p * p_inv, lane_lo)], axis=0)
        end_bk = jnp.concatenate([_stack_heads(b_p * p_end, lane_lo),
                                  _stack_heads(k_p * p_end, lane_lo)], axis=0)
        v_st = _stack_heads(v_p, lane_lo)

        g_a = jnp.where(strict2, _dot3(lhs_a, rhs_bk, NT), 0.0)
        g_r = jnp.where(incl2, _dot1(lhs_r, rhs_bk, NT), 0.0)
        t_inv = _unit_lower_inverse(g_a[:, 0:H2], row, col, eye)

        s_t = st_ref[q]
        u_st = _dot3(t_inv, _dot3(lhs_a, s_t, NT) + _dot3(g_a[:, H2:2 * H2], v_st))
        uv = jnp.concatenate([u_st, v_st], axis=0)
        y_st = _dot1(lhs_r, s_t, NT) + _dot1(g_r, uv)
        st_ref[q] = s_t * p_all + _dot3(uv, end_bk, TN)
        y = y_st[0:L, :] + y_st[L:H2, :]

        mean = _seg_sum(y, lane_lo) * (1.0 / RWKV_HEAD_DIM)
        d = y - mean
        var = _seg_sum(d * d, lane_lo) * (1.0 / RWKV_HEAD_DIM)
        yn = d * lax.rsqrt(var + RWKV_GN_EPS) * lnw_ref[:, sl] + lnb_ref[:, sl]
        bonus = _seg_sum(r_p * k_p * rk_ref[:, sl], lane_lo) * v_p
        o_ref[:, sl] = (yn + bonus) * gate[:, sl]


def _rwkv(p_rwkv, mu, w0, w2p, a0, a2p, g2, k_k, k_a, r_k, ln_w, ln_b, batch, seq):
    L = RWKV_CHUNK
    nc = seq // L
    const = lambda b, c: (0, 0)
    vec = pl.BlockSpec((1, RWKV_D), const)
    return pl.pallas_call(
        _rwkv_kernel,
        name="rwkv7",
        grid=(batch, nc),
        in_specs=[
            pl.BlockSpec((L, RWKV_COLS), lambda b, c: (b * nc + c, 0)),
            pl.BlockSpec((1, RWKV_COLS), const),
            vec,
            pl.BlockSpec((LANES, RWKV_D), const),
            vec,
            pl.BlockSpec((LANES, RWKV_D), const),
            pl.BlockSpec((GATE_LORA, RWKV_D), const),
            vec, vec, vec, vec, vec,
        ],
        out_specs=pl.BlockSpec((L, RWKV_D), lambda b, c: (b * nc + c, 0)),
        out_shape=jax.ShapeDtypeStruct((batch * seq, RWKV_D), F32),
        scratch_shapes=[
            pltpu.VMEM((SUBLANES + L, RWKV_COLS), F32),
            pltpu.VMEM((RWKV_D // PAIR, PAIR, PAIR), F32),
        ],
        compiler_params=pltpu.CompilerParams(
            dimension_semantics=("arbitrary", "arbitrary"), vmem_limit_bytes=VMEM_LIMIT),
    )(p_rwkv, mu, w0, w2p, a0, a2p, g2, k_k, k_a, r_k, ln_w, ln_b)


def _ffn_kernel(ys_ref, yr_ref, x_ref, wo_ref, g1_ref, g2_ref, wup_ref, cw_ref, cb_ref, wdn_ref, g3_ref,
                o_ref, ubuf_ref, acc_ref, *, tm):
    @pl.when(pl.program_id(1) == 0)
    def _():
        ubuf_ref[0:SUBLANES, :] = jnp.zeros((SUBLANES, 2 * D_FF), F32)

    mix = _mm(ys_ref[...].astype(BF16), wo_ref[0:SSD_D, :]) + _mm(yr_ref[...].astype(BF16), wo_ref[SSD_D:, :])
    h = x_ref[...] + _rms(mix, g1_ref[...])
    hn = _rms(h, g2_ref[...]).astype(BF16)

    def conv(c0):
        cs = slice(c0, c0 + FFN_COLS)
        ubuf_ref[SUBLANES:SUBLANES + tm, cs] = _mm(hn, wup_ref[:, cs])
        out = cb_ref[:, cs]
        for k in range(FFN_CONV):
            off = SUBLANES - (FFN_CONV - 1) + k
            out = out + cw_ref[k:k + 1, cs] * ubuf_ref[off:off + tm, cs]
        ubuf_ref[0:SUBLANES, cs] = ubuf_ref[tm:tm + SUBLANES, cs]
        return out

    for j in range(D_FF // FFN_COLS):
        gate = conv(j * FFN_COLS)
        val = conv(D_FF + j * FFN_COLS)
        act = (gate * _sigmoid(gate) * val).astype(BF16)
        part = _mm(act, wdn_ref[j * FFN_COLS:(j + 1) * FFN_COLS, :])
        if j == 0:
            acc_ref[...] = part
        else:
            acc_ref[...] += part
    o_ref[...] = h + _rms(acc_ref[...], g3_ref[...])


def _ffn(ys, yr, x2, wo, g1, g2, wup, cw, cb, wdn, g3, batch, seq, tm):
    nb = seq // tm
    const = lambda b, i: (0, 0)
    rows = lambda b, i: (b * nb + i, 0)
    res = functools.partial(pl.BlockSpec, index_map=const, pipeline_mode=pl.Buffered(1))
    return pl.pallas_call(
        functools.partial(_ffn_kernel, tm=tm),
        name="outproj_ffn",
        grid=(batch, nb),
        in_specs=[
            pl.BlockSpec((tm, SSD_D), rows),
            pl.BlockSpec((tm, RWKV_D), rows),
            pl.BlockSpec((tm, D_MODEL), rows),
            res((SSD_D + RWKV_D, D_MODEL)),
            pl.BlockSpec((1, D_MODEL), const),
            pl.BlockSpec((1, D_MODEL), const),
            res((D_MODEL, 2 * D_FF)),
            pl.BlockSpec((FFN_CONV, 2 * D_FF), const),
            pl.BlockSpec((1, 2 * D_FF), const),
            res((D_FF, D_MODEL)),
            pl.BlockSpec((1, D_MODEL), const),
        ],
        out_specs=pl.BlockSpec((tm, D_MODEL), rows),
        out_shape=jax.ShapeDtypeStruct((batch * seq, D_MODEL), F32),
        scratch_shapes=[
            pltpu.VMEM((SUBLANES + tm, 2 * D_FF), F32),
            pltpu.VMEM((tm, D_MODEL), F32),
        ],
        compiler_params=pltpu.CompilerParams(
            dimension_semantics=("arbitrary", "arbitrary"), vmem_limit_bytes=VMEM_LIMIT),
    )(ys, yr, x2, wo, g1, g2, wup, cw, cb, wdn, g3)


def _pad_lanes(v):
    return jnp.pad(v.astype(F32), (0, LANES - v.shape[0]))[None, :]


def _layer(h2, batch, seq, pre_mix_norm, w_in, ssd_conv_w, ssd_conv_b, ssd_dt_bias, ssd_a_log, ssd_d, ssd_norm,
           rwkv_mu, rwkv_w0, rwkv_w2, rwkv_a0, rwkv_a2, rwkv_g2, rwkv_k_k, rwkv_k_a, rwkv_r_k,
           rwkv_ln_w, rwkv_ln_b, w_out, post_mix_norm, pre_ffn_norm, ffn_w_up, ffn_conv_w,
           ffn_conv_b, ffn_w_down, post_ffn_norm):
    row = lambda v: v.astype(F32).reshape(1, -1)

    zx = SSD_D + SSD_CONV_DIM
    w_ssd = jnp.pad(w_in[:, :SSD_COLS], ((0, 0), (0, SSD_PCOLS - SSD_COLS))).astype(BF16)
    i1 = RWKV_D
    i2 = i1 + DECAY_LORA
    i3 = i2 + RWKV_D
    i4 = i3 + RWKV_D
    i5 = i4 + AAA_LORA
    perm = lambda t: jnp.concatenate(
        [t[..., 0:i1], t[..., i2:i3], t[..., i3:i4], t[..., i1:i2], t[..., i4:i5], t[..., i5:]], axis=-1)
    w_rwkv = perm(w_in[:, SSD_COLS:]).astype(BF16)
    mu = perm(rwkv_mu).astype(F32).reshape(1, -1)
    w2p = jnp.concatenate([rwkv_w2, jnp.zeros((AAA_LORA, RWKV_D), rwkv_w2.dtype)], axis=0).astype(BF16)
    a2p = jnp.concatenate([jnp.zeros((DECAY_LORA, RWKV_D), rwkv_a2.dtype), rwkv_a2], axis=0).astype(BF16)
    assert zx + SSD_HEADS == SSD_COLS

    p_ssd, p_rwkv = _inproj(h2, row(pre_mix_norm), w_ssd, w_rwkv, tm=512)
    y_ssd = _ssd(p_ssd, ssd_conv_w.astype(F32), row(ssd_conv_b), _pad_lanes(ssd_dt_bias), _pad_lanes(ssd_a_log),
                 row(jnp.repeat(ssd_d, SSD_HEAD_DIM)), row(ssd_norm), batch, seq)
    y_rwkv = _rwkv(p_rwkv, mu, row(rwkv_w0), w2p, row(rwkv_a0), a2p, rwkv_g2.astype(BF16), row(rwkv_k_k),
                   row(rwkv_k_a), row(rwkv_r_k), row(rwkv_ln_w), row(rwkv_ln_b), batch, seq)
    return _ffn(y_ssd, y_rwkv, h2, w_out.astype(BF16), row(post_mix_norm), row(pre_ffn_norm),
                ffn_w_up.astype(BF16), ffn_conv_w.astype(F32), row(ffn_conv_b), ffn_w_down.astype(BF16),
                row(post_ffn_norm), batch, seq, tm=256)


def kernel(x, pre_mix_norm, w_in, ssd_conv_w, ssd_conv_b, ssd_dt_bias, ssd_a_log, ssd_d, ssd_norm, rwkv_mu, rwkv_w0, rwkv_w2, rwkv_a0, rwkv_a2, rwkv_g2, rwkv_k_k, rwkv_k_a, rwkv_r_k, rwkv_ln_w, rwkv_ln_b, w_out, post_mix_norm, pre_ffn_norm, ffn_w_up, ffn_conv_w, ffn_conv_b, ffn_w_down, post_ffn_norm):
    batch, seq, d = x.shape
    h2 = x.reshape(batch * seq, d)
    params = (pre_mix_norm, w_in, ssd_conv_w, ssd_conv_b, ssd_dt_bias, ssd_a_log, ssd_d, ssd_norm, rwkv_mu,
              rwkv_w0, rwkv_w2, rwkv_a0, rwkv_a2, rwkv_g2, rwkv_k_k, rwkv_k_a, rwkv_r_k, rwkv_ln_w, rwkv_ln_b,
              w_out, post_mix_norm, pre_ffn_norm, ffn_w_up, ffn_conv_w, ffn_conv_b, ffn_w_down, post_ffn_norm)
    for l in range(pre_mix_norm.shape[0]):
        h2 = _layer(h2, batch, seq, *(t[l] for t in params))
    return h2.reshape(batch, seq, d)
```

```python
import functools

import jax
import jax.numpy as jnp
from jax import lax
from jax.experimental import pallas as pl
from jax.experimental.pallas import tpu as pltpu

F32 = jnp.float32
BF16 = jnp.bfloat16

D_MODEL = 1024
SSD_HEADS = 8
SSD_HEAD_DIM = 64
SSD_D = SSD_HEADS * SSD_HEAD_DIM
SSD_GROUPS = 2
SSD_STATE = 128
SSD_CONV = 4
SSD_CHUNK = 128
SSD_CONV_DIM = SSD_D + 2 * SSD_GROUPS * SSD_STATE
SSD_COLS = SSD_D + SSD_CONV_DIM + SSD_HEADS
RWKV_HEADS = 8
RWKV_HEAD_DIM = 64
RWKV_D = RWKV_HEADS * RWKV_HEAD_DIM
DECAY_LORA = 64
AAA_LORA = 64
GATE_LORA = 128
RWKV_COLS = 3 * RWKV_D + DECAY_LORA + AAA_LORA + GATE_LORA
RWKV_GN_EPS = 64e-5
D_FF = 2816
FFN_CONV = 3
NORM_EPS = 1e-6

LANES = 128
SUBLANES = 8
SSD_PCOLS = SSD_D + SSD_CONV_DIM + LANES
RWKV_CHUNK = 64
RWKV_BLOCK = 256
PAIR = 2 * RWKV_HEAD_DIM
FFN_COLS = 256
VMEM_LIMIT = 56 * 1024 * 1024

NN = (((1,), (0,)), ((), ()))
NT = (((1,), (1,)), ((), ()))
TN = (((0,), (0,)), ((), ()))


def _mm(a, b, dims=NN):
    return lax.dot_general(a, b, dims, preferred_element_type=F32)


def _dot1(a, b, dims=NN):
    return _mm(a.astype(BF16), b.astype(BF16), dims)


def _split3(a):
    a1 = a.astype(BF16)
    r1 = a - a1.astype(F32)
    a2 = r1.astype(BF16)
    a3 = (r1 - a2.astype(F32)).astype(BF16)
    return a1, a2, a3


def _dot_sel(sel_bf16, a):
    a1, a2, a3 = _split3(a)
    return _mm(sel_bf16, a1) + (_mm(sel_bf16, a2) + _mm(sel_bf16, a3))


def _a_dot_sel(a, sel_bf16):
    a1, a2, a3 = _split3(a)
    return _mm(a1, sel_bf16) + (_mm(a2, sel_bf16) + _mm(a3, sel_bf16))


def _rms(x, g):
    return x * lax.rsqrt(jnp.mean(x * x, axis=-1, keepdims=True) + NORM_EPS) * g


def _sigmoid(x):
    return 1.0 / (1.0 + jnp.exp(-x))


def _softplus(x):
    return jnp.maximum(x, 0.0) + jnp.log(1.0 + jnp.exp(-jnp.abs(x)))


def _iota2(shape, axis):
    return lax.broadcasted_iota(jnp.int32, shape, axis)


def _inproj_kernel(x_ref, g_ref, ws_ref, wr_ref, ps_ref, pr_ref):
    xb = _rms(x_ref[...], g_ref[...]).astype(BF16)
    ps_ref[...] = _mm(xb, ws_ref[...])
    pr_ref[...] = _mm(xb, wr_ref[...])


def _inproj(x2, g, w_ssd, w_rwkv, tm):
    n = x2.shape[0]
    const = lambda i: (0, 0)
    return pl.pallas_call(
        _inproj_kernel,
        name="inproj",
        grid=(n // tm,),
        in_specs=[
            pl.BlockSpec((tm, D_MODEL), lambda i: (i, 0)),
            pl.BlockSpec((1, D_MODEL), const),
            pl.BlockSpec((D_MODEL, SSD_PCOLS), const, pipeline_mode=pl.Buffered(1)),
            pl.BlockSpec((D_MODEL, RWKV_COLS), const, pipeline_mode=pl.Buffered(1)),
        ],
        out_specs=[
            pl.BlockSpec((tm, SSD_PCOLS), lambda i: (i, 0)),
            pl.BlockSpec((tm, RWKV_COLS), lambda i: (i, 0)),
        ],
        out_shape=[
            jax.ShapeDtypeStruct((n, SSD_PCOLS), F32),
            jax.ShapeDtypeStruct((n, RWKV_COLS), F32),
        ],
        compiler_params=pltpu.CompilerParams(
            dimension_semantics=("arbitrary",), vmem_limit_bytes=VMEM_LIMIT),
    )(x2, g, w_ssd, w_rwkv)


def _ssd_kernel(p_ref, cw_ref, cb_ref, dtb_ref, alog_ref, dsk_ref, ng_ref, o_ref, buf_ref, st_ref):
    L = SSD_CHUNK
    gw = SSD_D // SSD_GROUPS

    @pl.when(pl.program_id(1) == 0)
    def _():
        buf_ref[0:SUBLANES, :] = jnp.zeros((SUBLANES, SSD_CONV_DIM), F32)
        st_ref[...] = jnp.zeros(st_ref.shape, F32)

    z = p_ref[:, 0:SSD_D]
    xbc_raw = p_ref[:, SSD_D:SSD_D + SSD_CONV_DIM]
    dt_raw = p_ref[:, SSD_D + SSD_CONV_DIM:SSD_PCOLS]

    buf_ref[SUBLANES:SUBLANES + L, :] = xbc_raw
    acc = cb_ref[...] + cw_ref[0:1, :] * buf_ref[SUBLANES - 3:SUBLANES - 3 + L, :]
    for k in range(1, SSD_CONV):
        off = SUBLANES - (SSD_CONV - 1) + k
        acc = acc + cw_ref[k:k + 1, :] * buf_ref[off:off + L, :]
    buf_ref[0:SUBLANES, :] = xbc_raw[L - SUBLANES:L, :]
    xbc = acc * _sigmoid(acc)
    xs = xbc[:, 0:SSD_D]

    dt = _softplus(dt_raw + dtb_ref[...])
    a = dt * (-jnp.exp(alog_ref[...]))
    row = _iota2((L, L), 0)
    col = _iota2((L, L), 1)
    causal = row >= col
    tril = jnp.where(causal, 1.0, 0.0).astype(BF16)
    a_cs = _dot_sel(tril, a)
    a_cs_t = a_cs.T
    a_last = a_cs[L - 1:L, :]

    hsel = (_iota2((LANES, SSD_D), 1) // SSD_HEAD_DIM == _iota2((LANES, SSD_D), 0))
    hsel = jnp.where(hsel, 1.0, 0.0).astype(BF16)
    dt_e = _a_dot_sel(dt, hsel)
    ea_e = _a_dot_sel(jnp.exp(a_cs), hsel)
    ds_e = _a_dot_sel(jnp.exp(a_last - a_cs), hsel)

    x_dt = xs * dt_e
    x_b = x_dt.astype(BF16)
    x_dec = (x_dt * ds_e).astype(BF16)
    lane_lo = _iota2((L, LANES), 1) < SSD_HEAD_DIM

    for g in range(SSD_GROUPS):
        b_g = xbc[:, SSD_D + g * SSD_STATE:SSD_D + (g + 1) * SSD_STATE].astype(BF16)
        c_off = SSD_D + SSD_GROUPS * SSD_STATE
        c_g = xbc[:, c_off + g * SSD_STATE:c_off + (g + 1) * SSD_STATE].astype(BF16)
        scores = _mm(c_g, b_g, NT)
        state = st_ref[g]
        y_off = _mm(c_g, state.astype(BF16)) * ea_e[:, g * gw:(g + 1) * gw]
        st_ref[g] = state * ea_e[L - 1:L, g * gw:(g + 1) * gw] + _mm(b_g, x_dec[:, g * gw:(g + 1) * gw], TN)
        y_parts = []
        for j in range(gw // LANES):
            h0 = g * (SSD_HEADS // SSD_GROUPS) + 2 * j
            ms = []
            for h in (h0, h0 + 1):
                seg = a_cs[:, h:h + 1] - a_cs_t[h:h + 1, :]
                dec = jnp.exp(jnp.where(causal, seg, -jnp.inf))
                ms.append((scores * dec).astype(BF16))
            xp = x_b[:, h0 * SSD_HEAD_DIM:h0 * SSD_HEAD_DIM + LANES]
            zero = jnp.zeros_like(xp)
            x_bd = jnp.concatenate([jnp.where(lane_lo, xp, zero), jnp.where(lane_lo, zero, xp)], axis=0)
            y_parts.append(_mm(jnp.concatenate(ms, axis=1), x_bd))
        y = jnp.concatenate(y_parts, axis=1) + y_off
        y = y + dsk_ref[:, g * gw:(g + 1) * gw] * xs[:, g * gw:(g + 1) * gw]
        zg = z[:, g * gw:(g + 1) * gw]
        y = y * (zg * _sigmoid(zg))
        y = y * lax.rsqrt(jnp.mean(y * y, axis=-1, keepdims=True) + NORM_EPS)
        o_ref[:, g * gw:(g + 1) * gw] = y * ng_ref[:, g * gw:(g + 1) * gw]


def _ssd(p_ssd, cw, cb, dtb, alog, dsk, ng, batch, seq):
    L = SSD_CHUNK
    nc = seq // L
    const = lambda b, c: (0, 0)
    return pl.pallas_call(
        _ssd_kernel,
        name="ssd",
        grid=(batch, nc),
        in_specs=[
            pl.BlockSpec((L, SSD_PCOLS), lambda b, c: (b * nc + c, 0)),
            pl.BlockSpec((SSD_CONV, SSD_CONV_DIM), const),
            pl.BlockSpec((1, SSD_CONV_DIM), const),
            pl.BlockSpec((1, LANES), const),
            pl.BlockSpec((1, LANES), const),
            pl.BlockSpec((1, SSD_D), const),
            pl.BlockSpec((1, SSD_D), const),
        ],
        out_specs=pl.BlockSpec((L, SSD_D), lambda b, c: (b * nc + c, 0)),
        out_shape=jax.ShapeDtypeStruct((batch * seq, SSD_D), F32),
        scratch_shapes=[
            pltpu.VMEM((SUBLANES + L, SSD_CONV_DIM), F32),
            pltpu.VMEM((SSD_GROUPS, SSD_STATE, SSD_D // SSD_GROUPS), F32),
        ],
        compiler_params=pltpu.CompilerParams(
            dimension_semantics=("arbitrary", "arbitrary"), vmem_limit_bytes=VMEM_LIMIT),
    )(p_ssd, cw, cb, dtb, alog, dsk, ng)


def _seg_sum(x, lane_lo):
    s_lo = jnp.sum(jnp.where(lane_lo, x, 0.0), axis=-1, keepdims=True)
    s_hi = jnp.sum(jnp.where(lane_lo, 0.0, x), axis=-1, keepdims=True)
    return jnp.where(lane_lo, s_lo, s_hi)


def _stack_heads(x, lane_lo):
    zero = jnp.zeros_like(x)
    return jnp.concatenate([jnp.where(lane_lo, x, zero), jnp.where(lane_lo, zero, x)], axis=0)


def _unit_lower_inverse_many(a_list, row, col, eye):
    blk8 = row // 8 == col // 8
    a8 = [jnp.where(blk8, a, 0.0).astype(BF16) for a in a_list]
    t = [eye + x.astype(F32) for x in a8]
    a2 = [_mm(x, x).astype(BF16) for x in a8]
    t = [ti + _mm(ti.astype(BF16), x) for ti, x in zip(t, a2)]
    a4 = [_mm(x, x).astype(BF16) for x in a2]
    t = [ti + _mm(ti.astype(BF16), x) for ti, x in zip(t, a4)]
    for s in (8, 16, 32):
        lower_left = (row // (2 * s) == col // (2 * s)) & ((row // s) % 2 == 1) & ((col // s) % 2 == 0)
        tb = [ti.astype(BF16) for ti in t]
        x = [_mm(tbi, jnp.where(lower_left, a, 0.0).astype(BF16)).astype(BF16) for tbi, a in zip(tb, a_list)]
        t = [ti + _mm(xi, tbi) for ti, xi, tbi in zip(t, x, tb)]
    return t


def _rwkv_kernel(p_ref, mu_ref, w0_ref, w2_ref, a0_ref, a2_ref, g2_ref, kk_ref, ka_ref, rk_ref,
                 lnw_ref, lnb_ref, o_ref, buf_ref, st_ref, pre_ref):
    L = RWKV_CHUNK
    TB = RWKV_BLOCK
    H2 = 2 * L
    D = RWKV_D
    n_pairs = D // PAIR

    @pl.when(pl.program_id(1) == 0)
    def _():
        buf_ref[0:SUBLANES, :] = jnp.zeros((SUBLANES, RWKV_COLS), F32)
        st_ref[...] = jnp.zeros(st_ref.shape, F32)

    p = p_ref[...]
    buf_ref[SUBLANES:SUBLANES + TB, :] = p
    prev = buf_ref[SUBLANES - 1:SUBLANES - 1 + TB, :]
    buf_ref[0:SUBLANES, :] = p[TB - SUBLANES:TB, :]
    pm = p + (prev - p) * mu_ref[...]

    k = pm[:, D:2 * D]
    wa = pm[:, 3 * D:3 * D + LANES]
    g_lo = pm[:, 3 * D + LANES:3 * D + 2 * LANES]
    w_log = -_softplus(-(w0_ref[...] + _dot1(jnp.tanh(wa), w2_ref[...]))) - 0.5
    lw = -jnp.exp(w_log)
    alr = _sigmoid(a0_ref[...] + _dot1(wa, a2_ref[...]))
    blk_tril = (_iota2((TB, TB), 0) >= _iota2((TB, TB), 1)) & (_iota2((TB, TB), 0) // L == _iota2((TB, TB), 1) // L)
    R_, K_, V_, KK_, ALR_, LW_, CS_, G_ = range(8)
    pre_ref[R_] = pm[:, 0:D]
    pre_ref[K_] = k * (1.0 + (alr - 1.0) * ka_ref[...])
    pre_ref[V_] = pm[:, 2 * D:3 * D]
    pre_ref[KK_] = k * kk_ref[...]
    pre_ref[ALR_] = alr
    pre_ref[LW_] = lw
    pre_ref[CS_] = _dot_sel(jnp.where(blk_tril, 1.0, 0.0).astype(BF16), lw)
    pre_ref[G_] = _dot1(_sigmoid(g_lo), g2_ref[...])

    lane_lo = _iota2((L, PAIR), 1) < RWKV_HEAD_DIM
    row = _iota2((H2, H2), 0)
    col = _iota2((H2, H2), 1)
    eye = jnp.where(row == col, 1.0, 0.0)
    same_head = row // L == col // L
    strict = same_head & (row % L > col % L)
    incl = same_head & (row % L >= col % L)

    items = [(c, q) for c in range(TB // L) for q in range(n_pairs)]

    lhs_a, lhs_r, hat_b, p_all, v_t, a_ab, a_ak, a_rb, a_rk, vk = ([] for _ in range(10))
    for c, q in items:
        rs = slice(c * L, (c + 1) * L)
        sl = slice(q * PAIR, (q + 1) * PAIR)
        cs_p = pre_ref[CS_, rs, sl]
        cs_last = pre_ref[CS_, (c + 1) * L - 1:(c + 1) * L, sl]
        p_inv = jnp.exp(-cs_p)
        p_end = jnp.exp(cs_last - cs_p)
        kk_p = pre_ref[KK_, rs, sl]
        kk_n = kk_p * lax.rsqrt(jnp.maximum(_seg_sum(kk_p * kk_p, lane_lo), 1e-24))
        b_p = kk_n * pre_ref[ALR_, rs, sl]
        k_p = pre_ref[K_, rs, sl]
        la = _stack_heads(-kk_n * jnp.exp(cs_p - pre_ref[LW_, rs, sl]), lane_lo).astype(BF16)
        lr = _stack_heads(pre_ref[R_, rs, sl] * jnp.exp(cs_p), lane_lo).astype(BF16)
        rhs = jnp.concatenate([_stack_heads(b_p * p_inv, lane_lo), _stack_heads(k_p * p_inv, lane_lo)],
                              axis=0).astype(BF16)
        g = _mm(jnp.concatenate([la, lr], axis=0), rhs, NT)
        a_ab.append(jnp.where(strict, g[0:H2, 0:H2], 0.0))
        a_ak.append(jnp.where(strict, g[0:H2, H2:], 0.0).astype(BF16))
        a_rb.append(jnp.where(incl, g[H2:, 0:H2], 0.0).astype(BF16))
        a_rk.append(jnp.where(incl, g[H2:, H2:], 0.0).astype(BF16))
        lhs_a.append(la)
        lhs_r.append(lr)
        hat_b.append(_stack_heads(b_p * p_end, lane_lo).astype(BF16))
        p_all.append(jnp.exp(cs_last))
        vt = _stack_heads(pre_ref[V_, rs, sl], lane_lo).T.astype(BF16)
        v_t.append(vt)
        vk.append(_mm(vt, _stack_heads(k_p * p_end, lane_lo).astype(BF16)))

    t_inv = [t.astype(BF16) for t in _unit_lower_inverse_many(a_ab, row, col, eye)]
    t_a = [_mm(t, la).astype(BF16) for t, la in zip(t_inv, lhs_a)]
    av_t = [_mm(vt, a, NT).astype(BF16) for vt, a in zip(v_t, a_ak)]
    tav_t = [_mm(x, t, NT) for x, t in zip(av_t, t_inv)]
    arkv_t = [_mm(vt, a, NT) for vt, a in zip(v_t, a_rk)]

    for c in range(TB // L):
        rs = slice(c * L, (c + 1) * L)
        idx = [c * n_pairs + q for q in range(n_pairs)]
        s0 = [st_ref[q] for q in range(n_pairs)]
        s0b = [s.astype(BF16) for s in s0]
        u_t = [_mm(s0b[q], t_a[i], NT) + tav_t[i] for q, i in enumerate(idx)]
        u_tb = [u.astype(BF16) for u in u_t]
        for q, i in enumerate(idx):
            st_ref[q] = s0[q] * p_all[i] + _mm(u_tb[q], hat_b[i]) + vk[i]
        y_t = [_mm(s0b[q], lhs_r[i], NT) + _mm(u_tb[q], a_rb[i], NT) + arkv_t[i] for q, i in enumerate(idx)]
        for q in range(n_pairs):
            sl = slice(q * PAIR, (q + 1) * PAIR)
            y_st = y_t[q].T
            y = y_st[0:L, :] + y_st[L:H2, :]
            mean = _seg_sum(y, lane_lo) * (1.0 / RWKV_HEAD_DIM)
            d = y - mean
            var = _seg_sum(d * d, lane_lo) * (1.0 / RWKV_HEAD_DIM)
            yn = d * lax.rsqrt(var + RWKV_GN_EPS) * lnw_ref[:, sl] + lnb_ref[:, sl]
            v_p = pre_ref[V_, rs, sl]
            bonus = _seg_sum(pre_ref[R_, rs, sl] * pre_ref[K_, rs, sl] * rk_ref[:, sl], lane_lo) * v_p
            o_ref[rs, sl] = (yn + bonus) * pre_ref[G_, rs, sl]


def _rwkv(p_rwkv, mu, w0, w2p, a0, a2p, g2, k_k, k_a, r_k, ln_w, ln_b, batch, seq):
    TB = RWKV_BLOCK
    nb = seq // TB
    const = lambda b, c: (0, 0)
    vec = pl.BlockSpec((1, RWKV_D), const)
    return pl.pallas_call(
        _rwkv_kernel,
        name="rwkv7",
        grid=(batch, nb),
        in_specs=[
            pl.BlockSpec((TB, RWKV_COLS), lambda b, c: (b * nb + c, 0)),
            pl.BlockSpec((1, RWKV_COLS), const),
            vec,
            pl.BlockSpec((LANES, RWKV_D), const),
            vec,
            pl.BlockSpec((LANES, RWKV_D), const),
            pl.BlockSpec((GATE_LORA, RWKV_D), const),
            vec, vec, vec, vec, vec,
        ],
        out_specs=pl.BlockSpec((TB, RWKV_D), lambda b, c: (b * nb + c, 0)),
        out_shape=jax.ShapeDtypeStruct((batch * seq, RWKV_D), F32),
        scratch_shapes=[
            pltpu.VMEM((SUBLANES + TB, RWKV_COLS), F32),
            pltpu.VMEM((RWKV_D // PAIR, PAIR, PAIR), F32),
            pltpu.VMEM((8, TB, RWKV_D), F32),
        ],
        compiler_params=pltpu.CompilerParams(
            dimension_semantics=("arbitrary", "arbitrary"), vmem_limit_bytes=VMEM_LIMIT),
    )(p_rwkv, mu, w0, w2p, a0, a2p, g2, k_k, k_a, r_k, ln_w, ln_b)


def _ffn_kernel(ys_ref, yr_ref, x_ref, wo_ref, g1_ref, g2_ref, wup_ref, cw_ref, cb_ref, wdn_ref, g3_ref,
                o_ref, ubuf_ref, acc_ref, *, tm):
    @pl.when(pl.program_id(1) == 0)
    def _():
        ubuf_ref[0:SUBLANES, :] = jnp.zeros((SUBLANES, 2 * D_FF), F32)

    mix = _mm(ys_ref[...].astype(BF16), wo_ref[0:SSD_D, :]) + _mm(yr_ref[...].astype(BF16), wo_ref[SSD_D:, :])
    h = x_ref[...] + _rms(mix, g1_ref[...])
    hn = _rms(h, g2_ref[...]).astype(BF16)

    def conv(c0):
        cs = slice(c0, c0 + FFN_COLS)
        ubuf_ref[SUBLANES:SUBLANES + tm, cs] = _mm(hn, wup_ref[:, cs])
        out = cb_ref[:, cs]
        for k in range(FFN_CONV):
            off = SUBLANES - (FFN_CONV - 1) + k
            out = out + cw_ref[k:k + 1, cs] * ubuf_ref[off:off + tm, cs]
        ubuf_ref[0:SUBLANES, cs] = ubuf_ref[tm:tm + SUBLANES, cs]
        return out

    for j in range(D_FF // FFN_COLS):
        gate = conv(j * FFN_COLS)
        val = conv(D_FF + j * FFN_COLS)
        act = (gate * _sigmoid(gate) * val).astype(BF16)
        part = _mm(act, wdn_ref[j * FFN_COLS:(j + 1) * FFN_COLS, :])
        if j == 0:
            acc_ref[...] = part
        else:
            acc_ref[...] += part
    o_ref[...] = h + _rms(acc_ref[...], g3_ref[...])


def _ffn(ys, yr, x2, wo, g1, g2, wup, cw, cb, wdn, g3, batch, seq, tm):
    nb = seq // tm
    const = lambda b, i: (0, 0)
    rows = lambda b, i: (b * nb + i, 0)
    res = functools.partial(pl.BlockSpec, index_map=const, pipeline_mode=pl.Buffered(1))
    return pl.pallas_call(
        functools.partial(_ffn_kernel, tm=tm),
        name="outproj_ffn",
        grid=(batch, nb),
        in_specs=[
            pl.BlockSpec((tm, SSD_D), rows),
            pl.BlockSpec((tm, RWKV_D), rows),
            pl.BlockSpec((tm, D_MODEL), rows),
            res((SSD_D + RWKV_D, D_MODEL)),
            pl.BlockSpec((1, D_MODEL), const),
            pl.BlockSpec((1, D_MODEL), const),
            res((D_MODEL, 2 * D_FF)),
            pl.BlockSpec((FFN_CONV, 2 * D_FF), const),
            pl.BlockSpec((1, 2 * D_FF), const),
            res((D_FF, D_MODEL)),
            pl.BlockSpec((1, D_MODEL), const),
        ],
        out_specs=pl.BlockSpec((tm, D_MODEL), rows),
        out_shape=jax.ShapeDtypeStruct((batch * seq, D_MODEL), F32),
        scratch_shapes=[
            pltpu.VMEM((SUBLANES + tm, 2 * D_FF), F32),
            pltpu.VMEM((tm, D_MODEL), F32),
        ],
        compiler_params=pltpu.CompilerParams(
            dimension_semantics=("arbitrary", "arbitrary"), vmem_limit_bytes=VMEM_LIMIT),
    )(ys, yr, x2, wo, g1, g2, wup, cw, cb, wdn, g3)


def _pad_lanes(v):
    return jnp.pad(v.astype(F32), (0, LANES - v.shape[0]))[None, :]


def _layer(h2, batch, seq, pre_mix_norm, w_in, ssd_conv_w, ssd_conv_b, ssd_dt_bias, ssd_a_log, ssd_d, ssd_norm,
           rwkv_mu, rwkv_w0, rwkv_w2, rwkv_a0, rwkv_a2, rwkv_g2, rwkv_k_k, rwkv_k_a, rwkv_r_k,
           rwkv_ln_w, rwkv_ln_b, w_out, post_mix_norm, pre_ffn_norm, ffn_w_up, ffn_conv_w,
           ffn_conv_b, ffn_w_down, post_ffn_norm):
    row = lambda v: v.astype(F32).reshape(1, -1)

    w_ssd = jnp.pad(w_in[:, :SSD_COLS], ((0, 0), (0, SSD_PCOLS - SSD_COLS))).astype(BF16)
    i1 = RWKV_D
    i2 = i1 + DECAY_LORA
    i3 = i2 + RWKV_D
    i4 = i3 + RWKV_D
    i5 = i4 + AAA_LORA
    perm = lambda t: jnp.concatenate(
        [t[..., 0:i1], t[..., i2:i3], t[..., i3:i4], t[..., i1:i2], t[..., i4:i5], t[..., i5:]], axis=-1)
    w_rwkv = perm(w_in[:, SSD_COLS:]).astype(BF16)
    mu = perm(rwkv_mu).astype(F32).reshape(1, -1)
    w2p = jnp.concatenate([rwkv_w2, jnp.zeros((AAA_LORA, RWKV_D), rwkv_w2.dtype)], axis=0).astype(BF16)
    a2p = jnp.concatenate([jnp.zeros((DECAY_LORA, RWKV_D), rwkv_a2.dtype), rwkv_a2], axis=0).astype(BF16)

    p_ssd, p_rwkv = _inproj(h2, row(pre_mix_norm), w_ssd, w_rwkv, tm=512)
    y_ssd = _ssd(p_ssd, ssd_conv_w.astype(F32), row(ssd_conv_b), _pad_lanes(ssd_dt_bias), _pad_lanes(ssd_a_log),
                 row(jnp.repeat(ssd_d, SSD_HEAD_DIM)), row(ssd_norm), batch, seq)
    y_rwkv = _rwkv(p_rwkv, mu, row(rwkv_w0), w2p, row(rwkv_a0), a2p, rwkv_g2.astype(BF16), row(rwkv_k_k),
                   row(rwkv_k_a), row(rwkv_r_k), row(rwkv_ln_w), row(rwkv_ln_b), batch, seq)
    return _ffn(y_ssd, y_rwkv, h2, w_out.astype(BF16), row(post_mix_norm), row(pre_ffn_norm),
                ffn_w_up.astype(BF16), ffn_conv_w.astype(F32), row(ffn_conv_b), ffn_w_down.astype(BF16),
                row(post_ffn_norm), batch, seq, tm=256)


def kernel(x, pre_mix_norm, w_in, ssd_conv_w, ssd_conv_b, ssd_dt_bias, ssd_a_log, ssd_d, ssd_norm, rwkv_mu, rwkv_w0, rwkv_w2, rwkv_a0, rwkv_a2, rwkv_g2, rwkv_k_k, rwkv_k_a, rwkv_r_k, rwkv_ln_w, rwkv_ln_b, w_out, post_mix_norm, pre_ffn_norm, ffn_w_up, ffn_conv_w, ffn_conv_b, ffn_w_down, post_ffn_norm):
    batch, seq, d = x.shape
    h2 = x.reshape(batch * seq, d)
    params = (pre_mix_norm, w_in, ssd_conv_w, ssd_conv_b, ssd_dt_bias, ssd_a_log, ssd_d, ssd_norm, rwkv_mu,
              rwkv_w0, rwkv_w2, rwkv_a0, rwkv_a2, rwkv_g2, rwkv_k_k, rwkv_k_a, rwkv_r_k, rwkv_ln_w, rwkv_ln_b,
              w_out, post_mix_norm, pre_ffn_norm, ffn_w_up, ffn_conv_w, ffn_conv_b, ffn_w_down, post_ffn_norm)
    for l in range(pre_mix_norm.shape[0]):
        h2 = _layer(h2, batch, seq, *(t[l] for t in params))
    return h2.reshape(batch, seq, d)
```

```python
import functools

import jax
import jax.numpy as jnp
from jax import lax
from jax.experimental import pallas as pl
from jax.experimental.pallas import tpu as pltpu

F32 = jnp.float32
BF16 = jnp.bfloat16

D_MODEL = 1024
SSD_HEADS = 8
SSD_HEAD_DIM = 64
SSD_D = SSD_HEADS * SSD_HEAD_DIM
SSD_GROUPS = 2
SSD_STATE = 128
SSD_CONV = 4
SSD_CHUNK = 128
SSD_CONV_DIM = SSD_D + 2 * SSD_GROUPS * SSD_STATE
SSD_COLS = SSD_D + SSD_CONV_DIM + SSD_HEADS
RWKV_HEADS = 8
RWKV_HEAD_DIM = 64
RWKV_D = RWKV_HEADS * RWKV_HEAD_DIM
DECAY_LORA = 64
AAA_LORA = 64
GATE_LORA = 128
RWKV_COLS = 3 * RWKV_D + DECAY_LORA + AAA_LORA + GATE_LORA
RWKV_GN_EPS = 64e-5
D_FF = 2816
FFN_CONV = 3
NORM_EPS = 1e-6

LANES = 128
SUBLANES = 8
SSD_PCOLS = SSD_D + SSD_CONV_DIM + LANES
RWKV_CHUNK = 64
RWKV_BLOCK = 256
PAIR = 2 * RWKV_HEAD_DIM
FFN_COLS = 256
VMEM_LIMIT = 56 * 1024 * 1024

NN = (((1,), (0,)), ((), ()))
NT = (((1,), (1,)), ((), ()))
TN = (((0,), (0,)), ((), ()))


def _mm(a, b, dims=NN):
    return lax.dot_general(a, b, dims, preferred_element_type=F32)


def _dot1(a, b, dims=NN):
    return _mm(a.astype(BF16), b.astype(BF16), dims)


def _split3(a):
    a1 = a.astype(BF16)
    r1 = a - a1.astype(F32)
    a2 = r1.astype(BF16)
    a3 = (r1 - a2.astype(F32)).astype(BF16)
    return a1, a2, a3


def _dot_sel(sel_bf16, a):
    a1, a2, a3 = _split3(a)
    return _mm(sel_bf16, a1) + (_mm(sel_bf16, a2) + _mm(sel_bf16, a3))


def _a_dot_sel(a, sel_bf16):
    a1, a2, a3 = _split3(a)
    return _mm(a1, sel_bf16) + (_mm(a2, sel_bf16) + _mm(a3, sel_bf16))


def _rms(x, g):
    return x * lax.rsqrt(jnp.mean(x * x, axis=-1, keepdims=True) + NORM_EPS) * g


def _sigmoid(x):
    return 0.5 + 0.5 * jnp.tanh(0.5 * x)


def _silu(x):
    h = 0.5 * x
    return h + h * jnp.tanh(h)


def _softplus(x):
    return jnp.maximum(x, 0.0) + jnp.log(1.0 + jnp.exp(-jnp.abs(x)))


def _iota2(shape, axis):
    return lax.broadcasted_iota(jnp.int32, shape, axis)


def _inproj_kernel(x_ref, g_ref, ws_ref, wr_ref, ps_ref, pr_ref):
    xb = _rms(x_ref[...], g_ref[...]).astype(BF16)
    ps_ref[...] = _mm(xb, ws_ref[...])
    pr_ref[...] = _mm(xb, wr_ref[...])


def _inproj(x2, g, w_ssd, w_rwkv, tm):
    n = x2.shape[0]
    const = lambda i: (0, 0)
    return pl.pallas_call(
        _inproj_kernel,
        name="inproj",
        grid=(n // tm,),
        in_specs=[
            pl.BlockSpec((tm, D_MODEL), lambda i: (i, 0)),
            pl.BlockSpec((1, D_MODEL), const),
            pl.BlockSpec((D_MODEL, SSD_PCOLS), const, pipeline_mode=pl.Buffered(1)),
            pl.BlockSpec((D_MODEL, RWKV_COLS), const, pipeline_mode=pl.Buffered(1)),
        ],
        out_specs=[
            pl.BlockSpec((tm, SSD_PCOLS), lambda i: (i, 0)),
            pl.BlockSpec((tm, RWKV_COLS), lambda i: (i, 0)),
        ],
        out_shape=[
            jax.ShapeDtypeStruct((n, SSD_PCOLS), F32),
            jax.ShapeDtypeStruct((n, RWKV_COLS), F32),
        ],
        compiler_params=pltpu.CompilerParams(
            dimension_semantics=("arbitrary",), vmem_limit_bytes=VMEM_LIMIT),
    )(x2, g, w_ssd, w_rwkv)


def _ssd_kernel(p_ref, cw_ref, cb_ref, dtb_ref, alog_ref, dsk_ref, ng_ref, o_ref, buf_ref, st_ref):
    L = SSD_CHUNK
    gw = SSD_D // SSD_GROUPS

    @pl.when(pl.program_id(1) == 0)
    def _():
        buf_ref[0:SUBLANES, :] = jnp.zeros((SUBLANES, SSD_CONV_DIM), F32)
        st_ref[...] = jnp.zeros(st_ref.shape, F32)

    z = p_ref[:, 0:SSD_D]
    xbc_raw = p_ref[:, SSD_D:SSD_D + SSD_CONV_DIM]
    dt_raw = p_ref[:, SSD_D + SSD_CONV_DIM:SSD_PCOLS]

    buf_ref[SUBLANES:SUBLANES + L, :] = xbc_raw
    acc = cb_ref[...] + cw_ref[0:1, :] * buf_ref[SUBLANES - 3:SUBLANES - 3 + L, :]
    for k in range(1, SSD_CONV):
        off = SUBLANES - (SSD_CONV - 1) + k
        acc = acc + cw_ref[k:k + 1, :] * buf_ref[off:off + L, :]
    buf_ref[0:SUBLANES, :] = xbc_raw[L - SUBLANES:L, :]
    xbc = _silu(acc)
    xs = xbc[:, 0:SSD_D]

    dt = _softplus(dt_raw + dtb_ref[...])
    a = dt * (-jnp.exp(alog_ref[...]))
    row = _iota2((L, L), 0)
    col = _iota2((L, L), 1)
    causal = row >= col
    tril = jnp.where(causal, 1.0, 0.0).astype(BF16)
    a_cs = _dot_sel(tril, a)
    a_cs_t = a_cs.T
    a_last = a_cs[L - 1:L, :]

    hsel = (_iota2((LANES, SSD_D), 1) // SSD_HEAD_DIM == _iota2((LANES, SSD_D), 0))
    hsel = jnp.where(hsel, 1.0, 0.0).astype(BF16)
    dt_e = _a_dot_sel(dt, hsel)
    ea_e = _a_dot_sel(jnp.exp(a_cs), hsel)
    ds_e = _a_dot_sel(jnp.exp(a_last - a_cs), hsel)

    x_dt = xs * dt_e
    x_b = x_dt.astype(BF16)
    x_dec = (x_dt * ds_e).astype(BF16)
    lane_lo = _iota2((L, LANES), 1) < SSD_HEAD_DIM

    for g in range(SSD_GROUPS):
        b_g = xbc[:, SSD_D + g * SSD_STATE:SSD_D + (g + 1) * SSD_STATE].astype(BF16)
        c_off = SSD_D + SSD_GROUPS * SSD_STATE
        c_g = xbc[:, c_off + g * SSD_STATE:c_off + (g + 1) * SSD_STATE].astype(BF16)
        scores = _mm(c_g, b_g, NT)
        state = st_ref[g]
        y_off = _mm(c_g, state.astype(BF16)) * ea_e[:, g * gw:(g + 1) * gw]
        st_ref[g] = state * ea_e[L - 1:L, g * gw:(g + 1) * gw] + _mm(b_g, x_dec[:, g * gw:(g + 1) * gw], TN)
        y_parts = []
        for j in range(gw // LANES):
            h0 = g * (SSD_HEADS // SSD_GROUPS) + 2 * j
            ms = []
            for h in (h0, h0 + 1):
                seg = a_cs[:, h:h + 1] - a_cs_t[h:h + 1, :]
                dec = jnp.exp(jnp.where(causal, seg, -jnp.inf))
                ms.append((scores * dec).astype(BF16))
            xp = x_b[:, h0 * SSD_HEAD_DIM:h0 * SSD_HEAD_DIM + LANES]
            zero = jnp.zeros_like(xp)
            x_bd = jnp.concatenate([jnp.where(lane_lo, xp, zero), jnp.where(lane_lo, zero, xp)], axis=0)
            y_parts.append(_mm(jnp.concatenate(ms, axis=1), x_bd))
        y = jnp.concatenate(y_parts, axis=1) + y_off
        y = y + dsk_ref[:, g * gw:(g + 1) * gw] * xs[:, g * gw:(g + 1) * gw]
        zg = z[:, g * gw:(g + 1) * gw]
        y = y * _silu(zg)
        y = y * lax.rsqrt(jnp.mean(y * y, axis=-1, keepdims=True) + NORM_EPS)
        o_ref[:, g * gw:(g + 1) * gw] = y * ng_ref[:, g * gw:(g + 1) * gw]


def _ssd(p_ssd, cw, cb, dtb, alog, dsk, ng, batch, seq):
    L = SSD_CHUNK
    nc = seq // L
    const = lambda b, c: (0, 0)
    return pl.pallas_call(
        _ssd_kernel,
        name="ssd",
        grid=(batch, nc),
        in_specs=[
            pl.BlockSpec((L, SSD_PCOLS), lambda b, c: (b * nc + c, 0)),
            pl.BlockSpec((SSD_CONV, SSD_CONV_DIM), const),
            pl.BlockSpec((1, SSD_CONV_DIM), const),
            pl.BlockSpec((1, LANES), const),
            pl.BlockSpec((1, LANES), const),
            pl.BlockSpec((1, SSD_D), const),
            pl.BlockSpec((1, SSD_D), const),
        ],
        out_specs=pl.BlockSpec((L, SSD_D), lambda b, c: (b * nc + c, 0)),
        out_shape=jax.ShapeDtypeStruct((batch * seq, SSD_D), F32),
        scratch_shapes=[
            pltpu.VMEM((SUBLANES + L, SSD_CONV_DIM), F32),
            pltpu.VMEM((SSD_GROUPS, SSD_STATE, SSD_D // SSD_GROUPS), F32),
        ],
        compiler_params=pltpu.CompilerParams(
            dimension_semantics=("arbitrary", "arbitrary"), vmem_limit_bytes=VMEM_LIMIT),
    )(p_ssd, cw, cb, dtb, alog, dsk, ng)


def _seg_sum(x, lane_lo):
    s_lo = jnp.sum(jnp.where(lane_lo, x, 0.0), axis=-1, keepdims=True)
    s_hi = jnp.sum(jnp.where(lane_lo, 0.0, x), axis=-1, keepdims=True)
    return jnp.where(lane_lo, s_lo, s_hi)


def _stack_heads(x, lane_lo):
    zero = jnp.zeros_like(x)
    return jnp.concatenate([jnp.where(lane_lo, x, zero), jnp.where(lane_lo, zero, x)], axis=0)


def _unit_lower_inverse_many(a_list, row, col, eye):
    blk8 = row // 8 == col // 8
    a8 = [jnp.where(blk8, a, 0.0).astype(BF16) for a in a_list]
    t = [eye + x.astype(F32) for x in a8]
    a2 = [_mm(x, x).astype(BF16) for x in a8]
    t = [ti + _mm(ti.astype(BF16), x) for ti, x in zip(t, a2)]
    a4 = [_mm(x, x).astype(BF16) for x in a2]
    t = [ti + _mm(ti.astype(BF16), x) for ti, x in zip(t, a4)]
    for s in (8, 16, 32):
        lower_left = (row // (2 * s) == col // (2 * s)) & ((row // s) % 2 == 1) & ((col // s) % 2 == 0)
        tb = [ti.astype(BF16) for ti in t]
        x = [_mm(tbi, jnp.where(lower_left, a, 0.0).astype(BF16)).astype(BF16) for tbi, a in zip(tb, a_list)]
        t = [ti + _mm(xi, tbi) for ti, xi, tbi in zip(t, x, tb)]
    return t


def _rwkv_kernel(p_ref, mu_ref, w0_ref, w2_ref, a0_ref, a2_ref, g2_ref, kk_ref, ka_ref, rk_ref,
                 lnw_ref, lnb_ref, o_ref, buf_ref, st_ref, pre_ref):
    L = RWKV_CHUNK
    TB = RWKV_BLOCK
    H2 = 2 * L
    D = RWKV_D
    n_pairs = D // PAIR

    @pl.when(pl.program_id(1) == 0)
    def _():
        buf_ref[0:SUBLANES, :] = jnp.zeros((SUBLANES, RWKV_COLS), F32)
        st_ref[...] = jnp.zeros(st_ref.shape, F32)

    p = p_ref[...]
    buf_ref[SUBLANES:SUBLANES + TB, :] = p
    prev = buf_ref[SUBLANES - 1:SUBLANES - 1 + TB, :]
    buf_ref[0:SUBLANES, :] = p[TB - SUBLANES:TB, :]
    pm = p + (prev - p) * mu_ref[...]

    k = pm[:, D:2 * D]
    wa = pm[:, 3 * D:3 * D + LANES]
    g_lo = pm[:, 3 * D + LANES:3 * D + 2 * LANES]
    w_log = -_softplus(-(w0_ref[...] + _dot1(jnp.tanh(wa), w2_ref[...]))) - 0.5
    lw = -jnp.exp(w_log)
    alr = _sigmoid(a0_ref[...] + _dot1(wa, a2_ref[...]))
    blk_tril = (_iota2((TB, TB), 0) >= _iota2((TB, TB), 1)) & (_iota2((TB, TB), 0) // L == _iota2((TB, TB), 1) // L)
    R_, K_, V_, KK_, ALR_, LW_, CS_, G_ = range(8)
    pre_ref[R_] = pm[:, 0:D]
    pre_ref[K_] = k * (1.0 + (alr - 1.0) * ka_ref[...])
    pre_ref[V_] = pm[:, 2 * D:3 * D]
    pre_ref[KK_] = k * kk_ref[...]
    pre_ref[ALR_] = alr
    pre_ref[LW_] = lw
    pre_ref[CS_] = _dot_sel(jnp.where(blk_tril, 1.0, 0.0).astype(BF16), lw)
    pre_ref[G_] = _dot1(_sigmoid(g_lo), g2_ref[...])

    lane_lo = _iota2((L, PAIR), 1) < RWKV_HEAD_DIM
    row = _iota2((H2, H2), 0)
    col = _iota2((H2, H2), 1)
    eye = jnp.where(row == col, 1.0, 0.0)
    same_head = row // L == col // L
    strict = same_head & (row % L > col % L)
    incl = same_head & (row % L >= col % L)

    items = [(c, q) for c in range(TB // L) for q in range(n_pairs)]

    lhs_a, lhs_r, hat_b, p_all, v_t, a_ab, a_ak, a_rb, a_rk, vk = ([] for _ in range(10))
    for c, q in items:
        rs = slice(c * L, (c + 1) * L)
        sl = slice(q * PAIR, (q + 1) * PAIR)
        cs_p = pre_ref[CS_, rs, sl]
        cs_last = pre_ref[CS_, (c + 1) * L - 1:(c + 1) * L, sl]
        p_inv = jnp.exp(-cs_p)
        p_end = jnp.exp(cs_last - cs_p)
        kk_p = pre_ref[KK_, rs, sl]
        kk_n = kk_p * lax.rsqrt(jnp.maximum(_seg_sum(kk_p * kk_p, lane_lo), 1e-24))
        b_p = kk_n * pre_ref[ALR_, rs, sl]
        k_p = pre_ref[K_, rs, sl]
        la = _stack_heads(-kk_n * jnp.exp(cs_p - pre_ref[LW_, rs, sl]), lane_lo).astype(BF16)
        lr = _stack_heads(pre_ref[R_, rs, sl] * jnp.exp(cs_p), lane_lo).astype(BF16)
        rhs = jnp.concatenate([_stack_heads(b_p * p_inv, lane_lo), _stack_heads(k_p * p_inv, lane_lo)],
                              axis=0).astype(BF16)
        g = _mm(jnp.concatenate([la, lr], axis=0), rhs, NT)
        a_ab.append(jnp.where(strict, g[0:H2, 0:H2], 0.0))
        a_ak.append(jnp.where(strict, g[0:H2, H2:], 0.0).astype(BF16))
        a_rb.append(jnp.where(incl, g[H2:, 0:H2], 0.0).astype(BF16))
        a_rk.append(jnp.where(incl, g[H2:, H2:], 0.0).astype(BF16))
        lhs_a.append(la)
        lhs_r.append(lr)
        hat_b.append(_stack_heads(b_p * p_end, lane_lo).astype(BF16))
        p_all.append(jnp.exp(cs_last))
        vt = _stack_heads(pre_ref[V_, rs, sl], lane_lo).T.astype(BF16)
        v_t.append(vt)
        vk.append(_mm(vt, _stack_heads(k_p * p_end, lane_lo).astype(BF16)))

    t_inv = [t.astype(BF16) for t in _unit_lower_inverse_many(a_ab, row, col, eye)]
    t_a = [_mm(t, la).astype(BF16) for t, la in zip(t_inv, lhs_a)]
    av_t = [_mm(vt, a, NT).astype(BF16) for vt, a in zip(v_t, a_ak)]
    tav_t = [_mm(x, t, NT) for x, t in zip(av_t, t_inv)]
    arkv_t = [_mm(vt, a, NT) for vt, a in zip(v_t, a_rk)]

    for c in range(TB // L):
        rs = slice(c * L, (c + 1) * L)
        idx = [c * n_pairs + q for q in range(n_pairs)]
        s0 = [st_ref[q] for q in range(n_pairs)]
        s0b = [s.astype(BF16) for s in s0]
        u_t = [_mm(s0b[q], t_a[i], NT) + tav_t[i] for q, i in enumerate(idx)]
        u_tb = [u.astype(BF16) for u in u_t]
        for q, i in enumerate(idx):
            st_ref[q] = s0[q] * p_all[i] + _mm(u_tb[q], hat_b[i]) + vk[i]
        y_t = [_mm(s0b[q], lhs_r[i], NT) + _mm(u_tb[q], a_rb[i], NT) + arkv_t[i] for q, i in enumerate(idx)]
        for q in range(n_pairs):
            sl = slice(q * PAIR, (q + 1) * PAIR)
            y_st = y_t[q].T
            y = y_st[0:L, :] + y_st[L:H2, :]
            mean = _seg_sum(y, lane_lo) * (1.0 / RWKV_HEAD_DIM)
            d = y - mean
            var = _seg_sum(d * d, lane_lo) * (1.0 / RWKV_HEAD_DIM)
            yn = d * lax.rsqrt(var + RWKV_GN_EPS) * lnw_ref[:, sl] + lnb_ref[:, sl]
            v_p = pre_ref[V_, rs, sl]
            bonus = _seg_sum(pre_ref[R_, rs, sl] * pre_ref[K_, rs, sl] * rk_ref[:, sl], lane_lo) * v_p
            o_ref[rs, sl] = (yn + bonus) * pre_ref[G_, rs, sl]


def _rwkv(p_rwkv, mu, w0, w2p, a0, a2p, g2, k_k, k_a, r_k, ln_w, ln_b, batch, seq):
    TB = RWKV_BLOCK
    nb = seq // TB
    const = lambda b, c: (0, 0)
    vec = pl.BlockSpec((1, RWKV_D), const)
    return pl.pallas_call(
        _rwkv_kernel,
        name="rwkv7",
        grid=(batch, nb),
        in_specs=[
            pl.BlockSpec((TB, RWKV_COLS), lambda b, c: (b * nb + c, 0)),
            pl.BlockSpec((1, RWKV_COLS), const),
            vec,
            pl.BlockSpec((LANES, RWKV_D), const),
            vec,
            pl.BlockSpec((LANES, RWKV_D), const),
            pl.BlockSpec((GATE_LORA, RWKV_D), const),
            vec, vec, vec, vec, vec,
        ],
        out_specs=pl.BlockSpec((TB, RWKV_D), lambda b, c: (b * nb + c, 0)),
        out_shape=jax.ShapeDtypeStruct((batch * seq, RWKV_D), F32),
        scratch_shapes=[
            pltpu.VMEM((SUBLANES + TB, RWKV_COLS), F32),
            pltpu.VMEM((RWKV_D // PAIR, PAIR, PAIR), F32),
            pltpu.VMEM((8, TB, RWKV_D), F32),
        ],
        compiler_params=pltpu.CompilerParams(
            dimension_semantics=("arbitrary", "arbitrary"), vmem_limit_bytes=VMEM_LIMIT),
    )(p_rwkv, mu, w0, w2p, a0, a2p, g2, k_k, k_a, r_k, ln_w, ln_b)


def _ffn_kernel(ys_ref, yr_ref, x_ref, wo_ref, g1_ref, g2_ref, wup_ref, cw_ref, cb_ref, wdn_ref, g3_ref,
                o_ref, ubuf_ref, *, tm):
    @pl.when(pl.program_id(1) == 0)
    def _():
        ubuf_ref[...] = jnp.zeros(ubuf_ref.shape, F32)

    mix = _mm(ys_ref[...].astype(BF16), wo_ref[0:SSD_D, :]) + _mm(yr_ref[...].astype(BF16), wo_ref[SSD_D:, :])
    h = x_ref[...] + _rms(mix, g1_ref[...])
    hn = _rms(h, g2_ref[...]).astype(BF16)
    sub = _iota2((SUBLANES, FFN_COLS), 0)

    def up(j):
        return [_mm(hn, wup_ref[:, c0:c0 + FFN_COLS]) for c0 in (j * FFN_COLS, D_FF + j * FFN_COLS)]

    def conv(u, c0):
        cs = slice(c0, c0 + FFN_COLS)
        tail = ubuf_ref[:, cs]
        out = cb_ref[:, cs] + cw_ref[FFN_CONV - 1:FFN_CONV, cs] * u
        for k in range(1, FFN_CONV):
            r = pltpu.roll(u, k, axis=0)
            head = jnp.where(sub < k, pltpu.roll(tail, k, axis=0), r[0:SUBLANES, :])
            shifted = jnp.concatenate([head, r[SUBLANES:, :]], axis=0)
            out = out + cw_ref[FFN_CONV - 1 - k:FFN_CONV - k, cs] * shifted
        ubuf_ref[:, cs] = u[tm - SUBLANES:tm, :]
        return out

    n_steps = D_FF // FFN_COLS
    u_next = up(0)
    acts = []
    for j in range(n_steps):
        u_gate, u_val = u_next
        if j + 1 < n_steps:
            u_next = up(j + 1)
        gate = conv(u_gate, j * FFN_COLS)
        val = conv(u_val, D_FF + j * FFN_COLS)
        acts.append((_silu(gate) * val).astype(BF16))
    f = _mm(jnp.concatenate(acts, axis=1), wdn_ref[...])
    o_ref[...] = h + _rms(f, g3_ref[...])


def _ffn(ys, yr, x2, wo, g1, g2, wup, cw, cb, wdn, g3, batch, seq, tm):
    nb = seq // tm
    const = lambda b, i: (0, 0)
    rows = lambda b, i: (b * nb + i, 0)
    res = functools.partial(pl.BlockSpec, index_map=const, pipeline_mode=pl.Buffered(1))
    return pl.pallas_call(
        functools.partial(_ffn_kernel, tm=tm),
        name="outproj_ffn",
        grid=(batch, nb),
        in_specs=[
            pl.BlockSpec((tm, SSD_D), rows),
            pl.BlockSpec((tm, RWKV_D), rows),
            pl.BlockSpec((tm, D_MODEL), rows),
            res((SSD_D + RWKV_D, D_MODEL)),
            pl.BlockSpec((1, D_MODEL), const),
            pl.BlockSpec((1, D_MODEL), const),
            res((D_MODEL, 2 * D_FF)),
            pl.BlockSpec((FFN_CONV, 2 * D_FF), const),
            pl.BlockSpec((1, 2 * D_FF), const),
            res((D_FF, D_MODEL)),
            pl.BlockSpec((1, D_MODEL), const),
        ],
        out_specs=pl.BlockSpec((tm, D_MODEL), rows),
        out_shape=jax.ShapeDtypeStruct((batch * seq, D_MODEL), F32),
        scratch_shapes=[
            pltpu.VMEM((SUBLANES, 2 * D_FF), F32),
        ],
        compiler_params=pltpu.CompilerParams(
            dimension_semantics=("arbitrary", "arbitrary"), vmem_limit_bytes=VMEM_LIMIT),
    )(ys, yr, x2, wo, g1, g2, wup, cw, cb, wdn, g3)


def _pad_lanes(v):
    return jnp.pad(v.astype(F32), (0, LANES - v.shape[0]))[None, :]


def _layer(h2, batch, seq, pre_mix_norm, w_in, ssd_conv_w, ssd_conv_b, ssd_dt_bias, ssd_a_log, ssd_d, ssd_norm,
           rwkv_mu, rwkv_w0, rwkv_w2, rwkv_a0, rwkv_a2, rwkv_g2, rwkv_k_k, rwkv_k_a, rwkv_r_k,
           rwkv_ln_w, rwkv_ln_b, w_out, post_mix_norm, pre_ffn_norm, ffn_w_up, ffn_conv_w,
           ffn_conv_b, ffn_w_down, post_ffn_norm):
    row = lambda v: v.astype(F32).reshape(1, -1)

    w_ssd = jnp.pad(w_in[:, :SSD_COLS], ((0, 0), (0, SSD_PCOLS - SSD_COLS))).astype(BF16)
    i1 = RWKV_D
    i2 = i1 + DECAY_LORA
    i3 = i2 + RWKV_D
    i4 = i3 + RWKV_D
    i5 = i4 + AAA_LORA
    perm = lambda t: jnp.concatenate(
        [t[..., 0:i1], t[..., i2:i3], t[..., i3:i4], t[..., i1:i2], t[..., i4:i5], t[..., i5:]], axis=-1)
    w_rwkv = perm(w_in[:, SSD_COLS:]).astype(BF16)
    mu = perm(rwkv_mu).astype(F32).reshape(1, -1)
    w2p = jnp.concatenate([rwkv_w2, jnp.zeros((AAA_LORA, RWKV_D), rwkv_w2.dtype)], axis=0).astype(BF16)
    a2p = jnp.concatenate([jnp.zeros((DECAY_LORA, RWKV_D), rwkv_a2.dtype), rwkv_a2], axis=0).astype(BF16)

    p_ssd, p_rwkv = _inproj(h2, row(pre_mix_norm), w_ssd, w_rwkv, tm=512)
    y_ssd = _ssd(p_ssd, ssd_conv_w.astype(F32), row(ssd_conv_b), _pad_lanes(ssd_dt_bias), _pad_lanes(ssd_a_log),
                 row(jnp.repeat(ssd_d, SSD_HEAD_DIM)), row(ssd_norm), batch, seq)
    y_rwkv = _rwkv(p_rwkv, mu, row(rwkv_w0), w2p, row(rwkv_a0), a2p, rwkv_g2.astype(BF16), row(rwkv_k_k),
                   row(rwkv_k_a), row(rwkv_r_k), row(rwkv_ln_w), row(rwkv_ln_b), batch, seq)
    return _ffn(y_ssd, y_rwkv, h2, w_out.astype(BF16), row(post_mix_norm), row(pre_ffn_norm),
                ffn_w_up.astype(BF16), ffn_conv_w.astype(F32), row(ffn_conv_b), ffn_w_down.astype(BF16),
                row(post_ffn_norm), batch, seq, tm=512)


def kernel(x, pre_mix_norm, w_in, ssd_conv_w, ssd_conv_b, ssd_dt_bias, ssd_a_log, ssd_d, ssd_norm, rwkv_mu, rwkv_w0, rwkv_w2, rwkv_a0, rwkv_a2, rwkv_g2, rwkv_k_k, rwkv_k_a, rwkv_r_k, rwkv_ln_w, rwkv_ln_b, w_out, post_mix_norm, pre_ffn_norm, ffn_w_up, ffn_conv_w, ffn_conv_b, ffn_w_down, post_ffn_norm):
    batch, seq, d = x.shape
    h2 = x.reshape(batch * seq, d)
    params = (pre_mix_norm, w_in, ssd_conv_w, ssd_conv_b, ssd_dt_bias, ssd_a_log, ssd_d, ssd_norm, rwkv_mu,
              rwkv_w0, rwkv_w2, rwkv_a0, rwkv_a2, rwkv_g2, rwkv_k_k, rwkv_k_a, rwkv_r_k, rwkv_ln_w, rwkv_ln_b,
              w_out, post_mix_norm, pre_ffn_norm, ffn_w_up, ffn_conv_w, ffn_conv_b, ffn_w_down, post_ffn_norm)
    for l in range(pre_mix_norm.shape[0]):
        h2 = _layer(h2, batch, seq, *(t[l] for t in params))
    return h2.reshape(batch, seq, d)
```

```python
import functools

import jax
import jax.numpy as jnp
from jax import lax
from jax.experimental import pallas as pl
from jax.experimental.pallas import tpu as pltpu

F32 = jnp.float32
BF16 = jnp.bfloat16

D_MODEL = 1024
SSD_HEADS = 8
SSD_HEAD_DIM = 64
SSD_D = SSD_HEADS * SSD_HEAD_DIM
SSD_GROUPS = 2
SSD_STATE = 128
SSD_CONV = 4
SSD_CHUNK = 128
SSD_CONV_DIM = SSD_D + 2 * SSD_GROUPS * SSD_STATE
SSD_COLS = SSD_D + SSD_CONV_DIM + SSD_HEADS
RWKV_HEADS = 8
RWKV_HEAD_DIM = 64
RWKV_D = RWKV_HEADS * RWKV_HEAD_DIM
DECAY_LORA = 64
AAA_LORA = 64
GATE_LORA = 128
RWKV_COLS = 3 * RWKV_D + DECAY_LORA + AAA_LORA + GATE_LORA
RWKV_GN_EPS = 64e-5
D_FF = 2816
FFN_CONV = 3
NORM_EPS = 1e-6

LANES = 128
SUBLANES = 8
SSD_PCOLS = SSD_D + SSD_CONV_DIM + LANES
RWKV_CHUNK = 64
RWKV_BLOCK = 512
RWKV_GROUP = 4
PAIR = 2 * RWKV_HEAD_DIM
FFN_COLS = 256
VMEM_LIMIT = 56 * 1024 * 1024

NN = (((1,), (0,)), ((), ()))
NT = (((1,), (1,)), ((), ()))
TN = (((0,), (0,)), ((), ()))


def _mm(a, b, dims=NN):
    return lax.dot_general(a, b, dims, preferred_element_type=F32)


def _dot1(a, b, dims=NN):
    return _mm(a.astype(BF16), b.astype(BF16), dims)


def _split3(a):
    a1 = a.astype(BF16)
    r1 = a - a1.astype(F32)
    a2 = r1.astype(BF16)
    a3 = (r1 - a2.astype(F32)).astype(BF16)
    return a1, a2, a3


def _dot_sel(sel_bf16, a):
    a1, a2, a3 = _split3(a)
    return _mm(sel_bf16, a1) + (_mm(sel_bf16, a2) + _mm(sel_bf16, a3))


def _a_dot_sel(a, sel_bf16):
    a1, a2, a3 = _split3(a)
    return _mm(a1, sel_bf16) + (_mm(a2, sel_bf16) + _mm(a3, sel_bf16))


def _dot_sel2(sel_bf16, a):
    hi = a.astype(BF16)
    lo = (a - hi.astype(F32)).astype(BF16)
    return _mm(sel_bf16, hi) + _mm(sel_bf16, lo)


def _shift_rows(x, tail, k):
    sub = _iota2((SUBLANES, x.shape[1]), 0)
    r = pltpu.roll(x, k, axis=0)
    head = jnp.where(sub < k, pltpu.roll(tail, k, axis=0), r[0:SUBLANES, :])
    return jnp.concatenate([head, r[SUBLANES:, :]], axis=0)


def _rms(x, g):
    return x * lax.rsqrt(jnp.mean(x * x, axis=-1, keepdims=True) + NORM_EPS) * g


def _sigmoid(x):
    return 0.5 + 0.5 * jnp.tanh(0.5 * x)


def _silu(x):
    h = 0.5 * x
    return h + h * jnp.tanh(h)


def _softplus(x):
    return jnp.maximum(x, 0.0) + jnp.log(1.0 + jnp.exp(-jnp.abs(x)))


def _iota2(shape, axis):
    return lax.broadcasted_iota(jnp.int32, shape, axis)


def _inproj_kernel(x_ref, g_ref, ws_ref, wr_ref, ps_ref, pr_ref):
    xb = _rms(x_ref[...], g_ref[...]).astype(BF16)
    ps_ref[...] = _mm(xb, ws_ref[...])
    pr_ref[...] = _mm(xb, wr_ref[...])


def _inproj(x2, g, w_ssd, w_rwkv, tm):
    n = x2.shape[0]
    const = lambda i: (0, 0)
    return pl.pallas_call(
        _inproj_kernel,
        name="inproj",
        grid=(n // tm,),
        in_specs=[
            pl.BlockSpec((tm, D_MODEL), lambda i: (i, 0)),
            pl.BlockSpec((1, D_MODEL), const),
            pl.BlockSpec((D_MODEL, SSD_PCOLS), const, pipeline_mode=pl.Buffered(1)),
            pl.BlockSpec((D_MODEL, RWKV_COLS), const, pipeline_mode=pl.Buffered(1)),
        ],
        out_specs=[
            pl.BlockSpec((tm, SSD_PCOLS), lambda i: (i, 0)),
            pl.BlockSpec((tm, RWKV_COLS), lambda i: (i, 0)),
        ],
        out_shape=[
            jax.ShapeDtypeStruct((n, SSD_PCOLS), F32),
            jax.ShapeDtypeStruct((n, RWKV_COLS), F32),
        ],
        compiler_params=pltpu.CompilerParams(
            dimension_semantics=("arbitrary",), vmem_limit_bytes=VMEM_LIMIT),
    )(x2, g, w_ssd, w_rwkv)


def _ssd_kernel(p_ref, cw_ref, cb_ref, dtb_ref, alog_ref, dsk_ref, ng_ref, o_ref, buf_ref, st_ref):
    L = SSD_CHUNK
    gw = SSD_D // SSD_GROUPS

    @pl.when(pl.program_id(1) == 0)
    def _():
        buf_ref[...] = jnp.zeros(buf_ref.shape, F32)
        st_ref[...] = jnp.zeros(st_ref.shape, F32)

    z = p_ref[:, 0:SSD_D]
    xbc_raw = p_ref[:, SSD_D:SSD_D + SSD_CONV_DIM]
    dt_raw = p_ref[:, SSD_D + SSD_CONV_DIM:SSD_PCOLS]

    tail = buf_ref[...]
    acc = cb_ref[...] + cw_ref[SSD_CONV - 1:SSD_CONV, :] * xbc_raw
    for k in range(1, SSD_CONV):
        acc = acc + cw_ref[SSD_CONV - 1 - k:SSD_CONV - k, :] * _shift_rows(xbc_raw, tail, k)
    buf_ref[...] = xbc_raw[L - SUBLANES:L, :]
    xbc = _silu(acc)
    xs = xbc[:, 0:SSD_D]

    dt = _softplus(dt_raw + dtb_ref[...])
    a = dt * (-jnp.exp(alog_ref[...]))
    row = _iota2((L, L), 0)
    col = _iota2((L, L), 1)
    causal = row >= col
    tril = jnp.where(causal, 1.0, 0.0).astype(BF16)
    a_cs = _dot_sel(tril, a)
    a_cs_t = a_cs.T

    hsel = (_iota2((LANES, SSD_D), 1) // SSD_HEAD_DIM == _iota2((LANES, SSD_D), 0))
    hsel = jnp.where(hsel, 1.0, 0.0).astype(BF16)
    dt_e = _mm(dt.astype(BF16), hsel)
    acs_e = _a_dot_sel(a_cs, hsel)
    ea_e = jnp.exp(acs_e)
    ds_e = jnp.exp(acs_e[L - 1:L, :] - acs_e)

    x_dt = xs * dt_e
    x_b = x_dt.astype(BF16)
    x_dec = (x_dt * ds_e).astype(BF16)
    lane_lo = _iota2((L, LANES), 1) < SSD_HEAD_DIM

    for g in range(SSD_GROUPS):
        b_g = xbc[:, SSD_D + g * SSD_STATE:SSD_D + (g + 1) * SSD_STATE].astype(BF16)
        c_off = SSD_D + SSD_GROUPS * SSD_STATE
        c_g = xbc[:, c_off + g * SSD_STATE:c_off + (g + 1) * SSD_STATE].astype(BF16)
        scores = _mm(c_g, b_g, NT)
        state = st_ref[g]
        y_off = _mm(c_g, state.astype(BF16)) * ea_e[:, g * gw:(g + 1) * gw]
        st_ref[g] = state * ea_e[L - 1:L, g * gw:(g + 1) * gw] + _mm(b_g, x_dec[:, g * gw:(g + 1) * gw], TN)
        y_parts = []
        for j in range(gw // LANES):
            h0 = g * (SSD_HEADS // SSD_GROUPS) + 2 * j
            ms = []
            for h in (h0, h0 + 1):
                seg = a_cs[:, h:h + 1] - a_cs_t[h:h + 1, :]
                dec = jnp.exp(jnp.where(causal, seg, -jnp.inf))
                ms.append((scores * dec).astype(BF16))
            xp = x_b[:, h0 * SSD_HEAD_DIM:h0 * SSD_HEAD_DIM + LANES]
            zero = jnp.zeros_like(xp)
            x_bd = jnp.concatenate([jnp.where(lane_lo, xp, zero), jnp.where(lane_lo, zero, xp)], axis=0)
            y_parts.append(_mm(jnp.concatenate(ms, axis=1), x_bd))
        y = jnp.concatenate(y_parts, axis=1) + y_off
        y = y + dsk_ref[:, g * gw:(g + 1) * gw] * xs[:, g * gw:(g + 1) * gw]
        zg = z[:, g * gw:(g + 1) * gw]
        y = y * _silu(zg)
        y = y * lax.rsqrt(jnp.mean(y * y, axis=-1, keepdims=True) + NORM_EPS)
        o_ref[:, g * gw:(g + 1) * gw] = (y * ng_ref[:, g * gw:(g + 1) * gw]).astype(o_ref.dtype)


def _ssd(p_ssd, cw, cb, dtb, alog, dsk, ng, batch, seq):
    L = SSD_CHUNK
    nc = seq // L
    const = lambda b, c: (0, 0)
    return pl.pallas_call(
        _ssd_kernel,
        name="ssd",
        grid=(batch, nc),
        in_specs=[
            pl.BlockSpec((L, SSD_PCOLS), lambda b, c: (b * nc + c, 0)),
            pl.BlockSpec((SSD_CONV, SSD_CONV_DIM), const),
            pl.BlockSpec((1, SSD_CONV_DIM), const),
            pl.BlockSpec((1, LANES), const),
            pl.BlockSpec((1, LANES), const),
            pl.BlockSpec((1, SSD_D), const),
            pl.BlockSpec((1, SSD_D), const),
        ],
        out_specs=pl.BlockSpec((L, SSD_D), lambda b, c: (b * nc + c, 0)),
        out_shape=jax.ShapeDtypeStruct((batch * seq, SSD_D), BF16),
        scratch_shapes=[
            pltpu.VMEM((SUBLANES, SSD_CONV_DIM), F32),
            pltpu.VMEM((SSD_GROUPS, SSD_STATE, SSD_D // SSD_GROUPS), F32),
        ],
        compiler_params=pltpu.CompilerParams(
            dimension_semantics=("arbitrary", "arbitrary"), vmem_limit_bytes=VMEM_LIMIT),
    )(p_ssd, cw, cb, dtb, alog, dsk, ng)


def _seg_sum(x, lane_lo):
    s_lo = jnp.sum(jnp.where(lane_lo, x, 0.0), axis=-1, keepdims=True)
    s_hi = jnp.sum(jnp.where(lane_lo, 0.0, x), axis=-1, keepdims=True)
    return jnp.where(lane_lo, s_lo, s_hi)


def _stack_heads(x, lane_lo):
    zero = jnp.zeros_like(x)
    return jnp.concatenate([jnp.where(lane_lo, x, zero), jnp.where(lane_lo, zero, x)], axis=0)


def _unit_lower_inverse_many(a_list, row, col, eye):
    blk8 = row // 8 == col // 8
    a8 = [jnp.where(blk8, a, 0.0).astype(BF16) for a in a_list]
    t = [eye + x.astype(F32) for x in a8]
    a2 = [_mm(x, x).astype(BF16) for x in a8]
    t = [ti + _mm(ti.astype(BF16), x) for ti, x in zip(t, a2)]
    a4 = [_mm(x, x).astype(BF16) for x in a2]
    t = [ti + _mm(ti.astype(BF16), x) for ti, x in zip(t, a4)]
    n = a_list[0].shape[0]
    for s in (8, 16, 32):
        lower_left = (row // (2 * s) == col // (2 * s)) & ((row // s) % 2 == 1) & ((col // s) % 2 == 0)
        second = [slice(r0 + s, r0 + 2 * s) for r0 in range(0, n, 2 * s)]
        tb = [ti.astype(BF16) for ti in t]
        t2 = [jnp.concatenate([ti[r, :] for r in second], axis=0) for ti in t]
        x = [_mm(t2i.astype(BF16), jnp.where(lower_left, a, 0.0).astype(BF16)).astype(BF16)
             for t2i, a in zip(t2, a_list)]
        t2 = [t2i + _mm(xi, tbi) for t2i, xi, tbi in zip(t2, x, tb)]
        t = [jnp.concatenate([piece for m, r in enumerate(second)
                              for piece in (ti[r.start - s:r.start, :], t2i[m * s:(m + 1) * s, :])], axis=0)
             for ti, t2i in zip(t, t2)]
    return t


def _rwkv_kernel(p_ref, mu_ref, w0_ref, w2_ref, a0_ref, a2_ref, g2_ref, kk_ref, ka_ref, rk_ref,
                 lnw_ref, lnb_ref, o_ref, buf_ref, st_ref, pre_ref):
    L = RWKV_CHUNK
    TB = RWKV_BLOCK
    H2 = 2 * L
    D = RWKV_D
    n_pairs = D // PAIR

    @pl.when(pl.program_id(1) == 0)
    def _():
        buf_ref[...] = jnp.zeros(buf_ref.shape, F32)
        st_ref[...] = jnp.zeros(st_ref.shape, F32)

    p = p_ref[...]
    prev = _shift_rows(p, buf_ref[...], 1)
    buf_ref[...] = p[TB - SUBLANES:TB, :]
    pm = p + (prev - p) * mu_ref[...]

    k = pm[:, D:2 * D]
    wa = pm[:, 3 * D:3 * D + LANES]
    g_lo = pm[:, 3 * D + LANES:3 * D + 2 * LANES]
    w_log = -_softplus(-(w0_ref[...] + _dot1(jnp.tanh(wa), w2_ref[...]))) - 0.5
    lw = -jnp.exp(w_log)
    alr = _sigmoid(a0_ref[...] + _dot1(wa, a2_ref[...]))
    CB = 4 * L
    blk_tril = (_iota2((CB, CB), 0) >= _iota2((CB, CB), 1)) & (_iota2((CB, CB), 0) // L == _iota2((CB, CB), 1) // L)
    blk_tril = jnp.where(blk_tril, 1.0, 0.0).astype(BF16)
    R_, K_, V_, KK_, ALR_, LW_, CS_, G_ = range(8)
    pre_ref[R_] = pm[:, 0:D]
    pre_ref[K_] = k * (1.0 + (alr - 1.0) * ka_ref[...])
    pre_ref[V_] = pm[:, 2 * D:3 * D]
    pre_ref[KK_] = k * kk_ref[...]
    pre_ref[ALR_] = alr
    pre_ref[LW_] = lw
    for r0 in range(0, TB, CB):
        pre_ref[CS_, r0:r0 + CB, :] = _dot_sel2(blk_tril, lw[r0:r0 + CB, :])
    pre_ref[G_] = _dot1(_sigmoid(g_lo), g2_ref[...])

    lane_lo = _iota2((L, PAIR), 1) < RWKV_HEAD_DIM
    row = _iota2((H2, H2), 0)
    col = _iota2((H2, H2), 1)
    eye = jnp.where(row == col, 1.0, 0.0)
    same_head = row // L == col // L
    strict = same_head & (row % L > col % L)
    incl = same_head & (row % L >= col % L)

    def independent_part(chunks):
        lhs_a, v_t, a_ab, a_ak, a_rk, lr_arb, hat_bk, p_all = ([] for _ in range(8))
        for c, q in [(c, q) for c in chunks for q in range(n_pairs)]:
            rs = slice(c * L, (c + 1) * L)
            sl = slice(q * PAIR, (q + 1) * PAIR)
            cs_p = pre_ref[CS_, rs, sl]
            cs_last = pre_ref[CS_, (c + 1) * L - 1:(c + 1) * L, sl]
            p_inv = jnp.exp(-cs_p)
            p_end = jnp.exp(cs_last - cs_p)
            kk_p = pre_ref[KK_, rs, sl]
            kk_n = kk_p * lax.rsqrt(jnp.maximum(_seg_sum(kk_p * kk_p, lane_lo), 1e-24))
            b_p = kk_n * pre_ref[ALR_, rs, sl]
            k_p = pre_ref[K_, rs, sl]
            la = _stack_heads(-kk_n * jnp.exp(cs_p - pre_ref[LW_, rs, sl]), lane_lo).astype(BF16)
            lr = _stack_heads(pre_ref[R_, rs, sl] * jnp.exp(cs_p), lane_lo).astype(BF16)
            bt = (b_p * p_inv).astype(BF16)
            kt = (k_p * p_inv).astype(BF16)
            g = _mm(jnp.concatenate([la, lr], axis=0), jnp.concatenate([bt, bt, kt, kt], axis=0), NT)
            a_ab.append(jnp.where(strict, g[0:H2, 0:H2], 0.0))
            a_ak.append(jnp.where(strict, g[0:H2, H2:], 0.0).astype(BF16))
            a_rk.append(jnp.where(incl, g[H2:, H2:], 0.0).astype(BF16))
            lr_arb.append(jnp.concatenate([lr, jnp.where(incl, g[H2:, 0:H2], 0.0).astype(BF16)], axis=1))
            lhs_a.append(la)
            hat_bk.append(jnp.concatenate([_stack_heads(b_p * p_end, lane_lo), _stack_heads(k_p * p_end, lane_lo)],
                                          axis=0).astype(BF16))
            p_all.append(jnp.exp(cs_last))
            v_t.append(_stack_heads(pre_ref[V_, rs, sl], lane_lo).T.astype(BF16))
        t_inv = [t.astype(BF16) for t in _unit_lower_inverse_many(a_ab, row, col, eye)]
        t_a = [_mm(t, la).astype(BF16) for t, la in zip(t_inv, lhs_a)]
        av_t = [_mm(vt, a, NT).astype(BF16) for vt, a in zip(v_t, a_ak)]
        tav_t = [_mm(x, t, NT) for x, t in zip(av_t, t_inv)]
        arkv_t = [_mm(vt, a, NT) for vt, a in zip(v_t, a_rk)]
        return dict(t_a=t_a, tav_t=tav_t, arkv_t=arkv_t, lr_arb=lr_arb, hat_bk=hat_bk, v_t=v_t, p_all=p_all)

    def state_part(chunks, d):
        for ci, c in enumerate(chunks):
            rs = slice(c * L, (c + 1) * L)
            idx = [ci * n_pairs + q for q in range(n_pairs)]
            s0 = [st_ref[q] for q in range(n_pairs)]
            s0b = [s.astype(BF16) for s in s0]
            u_tb = [(_mm(s0b[q], d["t_a"][i], NT) + d["tav_t"][i]).astype(BF16) for q, i in enumerate(idx)]
            for q, i in enumerate(idx):
                st_ref[q] = s0[q] * d["p_all"][i] + _mm(jnp.concatenate([u_tb[q], d["v_t"][i]], axis=1),
                                                        d["hat_bk"][i])
            y_t = [_mm(jnp.concatenate([s0b[q], u_tb[q]], axis=1), d["lr_arb"][i], NT) + d["arkv_t"][i]
                   for q, i in enumerate(idx)]
            for q in range(n_pairs):
                sl = slice(q * PAIR, (q + 1) * PAIR)
                y_st = y_t[q].T
                y = y_st[0:L, :] + y_st[L:H2, :]
                mean = _seg_sum(y, lane_lo) * (1.0 / RWKV_HEAD_DIM)
                dev = y - mean
                var = _seg_sum(dev * dev, lane_lo) * (1.0 / RWKV_HEAD_DIM)
                yn = dev * lax.rsqrt(var + RWKV_GN_EPS) * lnw_ref[:, sl] + lnb_ref[:, sl]
                v_p = pre_ref[V_, rs, sl]
                bonus = _seg_sum(pre_ref[R_, rs, sl] * pre_ref[K_, rs, sl] * rk_ref[:, sl], lane_lo) * v_p
                o_ref[rs, sl] = ((yn + bonus) * pre_ref[G_, rs, sl]).astype(o_ref.dtype)

    groups = [list(range(c0, c0 + RWKV_GROUP)) for c0 in range(0, TB // L, RWKV_GROUP)]
    ready = [independent_part(g) for g in groups]
    for g, d in zip(groups, ready):
        state_part(g, d)


def _rwkv(p_rwkv, mu, w0, w2p, a0, a2p, g2, k_k, k_a, r_k, ln_w, ln_b, batch, seq):
    TB = RWKV_BLOCK
    nb = seq // TB
    const = lambda b, c: (0, 0)
    vec = pl.BlockSpec((1, RWKV_D), const)
    return pl.pallas_call(
        _rwkv_kernel,
        name="rwkv7",
        grid=(batch, nb),
        in_specs=[
            pl.BlockSpec((TB, RWKV_COLS), lambda b, c: (b * nb + c, 0)),
            pl.BlockSpec((1, RWKV_COLS), const),
            vec,
            pl.BlockSpec((LANES, RWKV_D), const),
            vec,
            pl.BlockSpec((LANES, RWKV_D), const),
            pl.BlockSpec((GATE_LORA, RWKV_D), const),
            vec, vec, vec, vec, vec,
        ],
        out_specs=pl.BlockSpec((TB, RWKV_D), lambda b, c: (b * nb + c, 0)),
        out_shape=jax.ShapeDtypeStruct((batch * seq, RWKV_D), BF16),
        scratch_shapes=[
            pltpu.VMEM((SUBLANES, RWKV_COLS), F32),
            pltpu.VMEM((RWKV_D // PAIR, PAIR, PAIR), F32),
            pltpu.VMEM((8, TB, RWKV_D), F32),
        ],
        compiler_params=pltpu.CompilerParams(
            dimension_semantics=("arbitrary", "arbitrary"), vmem_limit_bytes=VMEM_LIMIT),
    )(p_rwkv, mu, w0, w2p, a0, a2p, g2, k_k, k_a, r_k, ln_w, ln_b)


def _ffn_kernel(ys_ref, yr_ref, x_ref, wo_ref, g1_ref, g2_ref, wup_ref, cw_ref, cb_ref, wdn_ref, g3_ref,
                o_ref, ubuf_ref, *, tm):
    @pl.when(pl.program_id(1) == 0)
    def _():
        ubuf_ref[...] = jnp.zeros(ubuf_ref.shape, F32)

    mix = _mm(ys_ref[...], wo_ref[0:SSD_D, :]) + _mm(yr_ref[...], wo_ref[SSD_D:, :])
    h = x_ref[...] + _rms(mix, g1_ref[...])
    hn = _rms(h, g2_ref[...]).astype(BF16)
    sub = _iota2((SUBLANES, FFN_COLS), 0)

    def up(j):
        return [_mm(hn, wup_ref[:, c0:c0 + FFN_COLS]) for c0 in (j * FFN_COLS, D_FF + j * FFN_COLS)]

    def conv(u, c0):
        cs = slice(c0, c0 + FFN_COLS)
        tail = ubuf_ref[:, cs]
        out = cb_ref[:, cs] + cw_ref[FFN_CONV - 1:FFN_CONV, cs] * u
        for k in range(1, FFN_CONV):
            r = pltpu.roll(u, k, axis=0)
            head = jnp.where(sub < k, pltpu.roll(tail, k, axis=0), r[0:SUBLANES, :])
            shifted = jnp.concatenate([head, r[SUBLANES:, :]], axis=0)
            out = out + cw_ref[FFN_CONV - 1 - k:FFN_CONV - k, cs] * shifted
        ubuf_ref[:, cs] = u[tm - SUBLANES:tm, :]
        return out

    n_steps = D_FF // FFN_COLS
    u_next = up(0)
    acts = []
    for j in range(n_steps):
        u_gate, u_val = u_next
        if j + 1 < n_steps:
            u_next = up(j + 1)
        gate = conv(u_gate, j * FFN_COLS)
        val = conv(u_val, D_FF + j * FFN_COLS)
        acts.append((_silu(gate) * val).astype(BF16))
    f = _mm(jnp.concatenate(acts, axis=1), wdn_ref[...])
    o_ref[...] = h + _rms(f, g3_ref[...])


def _ffn(ys, yr, x2, wo, g1, g2, wup, cw, cb, wdn, g3, batch, seq, tm):
    nb = seq // tm
    const = lambda b, i: (0, 0)
    rows = lambda b, i: (b * nb + i, 0)
    res = functools.partial(pl.BlockSpec, index_map=const, pipeline_mode=pl.Buffered(1))
    return pl.pallas_call(
        functools.partial(_ffn_kernel, tm=tm),
        name="outproj_ffn",
        grid=(batch, nb),
        in_specs=[
            pl.BlockSpec((tm, SSD_D), rows),
            pl.BlockSpec((tm, RWKV_D), rows),
            pl.BlockSpec((tm, D_MODEL), rows),
            res((SSD_D + RWKV_D, D_MODEL)),
            pl.BlockSpec((1, D_MODEL), const),
            pl.BlockSpec((1, D_MODEL), const),
            res((D_MODEL, 2 * D_FF)),
            pl.BlockSpec((FFN_CONV, 2 * D_FF), const),
            pl.BlockSpec((1, 2 * D_FF), const),
            res((D_FF, D_MODEL)),
            pl.BlockSpec((1, D_MODEL), const),
        ],
        out_specs=pl.BlockSpec((tm, D_MODEL), rows),
        out_shape=jax.ShapeDtypeStruct((batch * seq, D_MODEL), F32),
        scratch_shapes=[
            pltpu.VMEM((SUBLANES, 2 * D_FF), F32),
        ],
        compiler_params=pltpu.CompilerParams(
            dimension_semantics=("arbitrary", "arbitrary"), vmem_limit_bytes=VMEM_LIMIT),
    )(ys, yr, x2, wo, g1, g2, wup, cw, cb, wdn, g3)


def _pad_lanes(v):
    return jnp.pad(v.astype(F32), (0, LANES - v.shape[0]))[None, :]


def _layer(h2, batch, seq, pre_mix_norm, w_in, ssd_conv_w, ssd_conv_b, ssd_dt_bias, ssd_a_log, ssd_d, ssd_norm,
           rwkv_mu, rwkv_w0, rwkv_w2, rwkv_a0, rwkv_a2, rwkv_g2, rwkv_k_k, rwkv_k_a, rwkv_r_k,
           rwkv_ln_w, rwkv_ln_b, w_out, post_mix_norm, pre_ffn_norm, ffn_w_up, ffn_conv_w,
           ffn_conv_b, ffn_w_down, post_ffn_norm):
    row = lambda v: v.astype(F32).reshape(1, -1)

    w_ssd = jnp.pad(w_in[:, :SSD_COLS], ((0, 0), (0, SSD_PCOLS - SSD_COLS))).astype(BF16)
    i1 = RWKV_D
    i2 = i1 + DECAY_LORA
    i3 = i2 + RWKV_D
    i4 = i3 + RWKV_D
    i5 = i4 + AAA_LORA
    perm = lambda t: jnp.concatenate(
        [t[..., 0:i1], t[..., i2:i3], t[..., i3:i4], t[..., i1:i2], t[..., i4:i5], t[..., i5:]], axis=-1)
    w_rwkv = perm(w_in[:, SSD_COLS:]).astype(BF16)
    mu = perm(rwkv_mu).astype(F32).reshape(1, -1)
    w2p = jnp.concatenate([rwkv_w2, jnp.zeros((AAA_LORA, RWKV_D), rwkv_w2.dtype)], axis=0).astype(BF16)
    a2p = jnp.concatenate([jnp.zeros((DECAY_LORA, RWKV_D), rwkv_a2.dtype), rwkv_a2], axis=0).astype(BF16)

    p_ssd, p_rwkv = _inproj(h2, row(pre_mix_norm), w_ssd, w_rwkv, tm=512)
    y_ssd = _ssd(p_ssd, ssd_conv_w.astype(F32), row(ssd_conv_b), _pad_lanes(ssd_dt_bias), _pad_lanes(ssd_a_log),
                 row(jnp.repeat(ssd_d, SSD_HEAD_DIM)), row(ssd_norm), batch, seq)
    y_rwkv = _rwkv(p_rwkv, mu, row(rwkv_w0), w2p, row(rwkv_a0), a2p, rwkv_g2.astype(BF16), row(rwkv_k_k),
                   row(rwkv_k_a), row(rwkv_r_k), row(rwkv_ln_w), row(rwkv_ln_b), batch, seq)
    return _ffn(y_ssd, y_rwkv, h2, w_out.astype(BF16), row(post_mix_norm), row(pre_ffn_norm),
                ffn_w_up.astype(BF16), ffn_conv_w.astype(F32), row(ffn_conv_b), ffn_w_down.astype(BF16),
                row(post_ffn_norm), batch, seq, tm=512)


def kernel(x, pre_mix_norm, w_in, ssd_conv_w, ssd_conv_b, ssd_dt_bias, ssd_a_log, ssd_d, ssd_norm, rwkv_mu, rwkv_w0, rwkv_w2, rwkv_a0, rwkv_a2, rwkv_g2, rwkv_k_k, rwkv_k_a, rwkv_r_k, rwkv_ln_w, rwkv_ln_b, w_out, post_mix_norm, pre_ffn_norm, ffn_w_up, ffn_conv_w, ffn_conv_b, ffn_w_down, post_ffn_norm):
    batch, seq, d = x.shape
    h2 = x.reshape(batch * seq, d)
    params = (pre_mix_norm, w_in, ssd_conv_w, ssd_conv_b, ssd_dt_bias, ssd_a_log, ssd_d, ssd_norm, rwkv_mu,
              rwkv_w0, rwkv_w2, rwkv_a0, rwkv_a2, rwkv_g2, rwkv_k_k, rwkv_k_a, rwkv_r_k, rwkv_ln_w, rwkv_ln_b,
              w_out, post_mix_norm, pre_ffn_norm, ffn_w_up, ffn_conv_w, ffn_conv_b, ffn_w_down, post_ffn_norm)
    for l in range(pre_mix_norm.shape[0]):
        h2 = _layer(h2, batch, seq, *(t[l] for t in params))
    return h2.reshape(batch, seq, d)
```

```python
import functools

import jax
import jax.numpy as jnp
from jax import lax
from jax.experimental import pallas as pl
from jax.experimental.pallas import tpu as pltpu

F32 = jnp.float32
BF16 = jnp.bfloat16

D_MODEL = 1024
SSD_HEADS = 8
SSD_HEAD_DIM = 64
SSD_D = SSD_HEADS * SSD_HEAD_DIM
SSD_GROUPS = 2
SSD_STATE = 128
SSD_CONV = 4
SSD_CHUNK = 128
SSD_CONV_DIM = SSD_D + 2 * SSD_GROUPS * SSD_STATE
SSD_COLS = SSD_D + SSD_CONV_DIM + SSD_HEADS
RWKV_HEADS = 8
RWKV_HEAD_DIM = 64
RWKV_D = RWKV_HEADS * RWKV_HEAD_DIM
DECAY_LORA = 64
AAA_LORA = 64
GATE_LORA = 128
RWKV_COLS = 3 * RWKV_D + DECAY_LORA + AAA_LORA + GATE_LORA
RWKV_GN_EPS = 64e-5
D_FF = 2816
FFN_CONV = 3
NORM_EPS = 1e-6
LOG2_E = 1.4426950408889634

LANES = 128
SUBLANES = 8
SSD_PCOLS = SSD_D + SSD_CONV_DIM + LANES
SSD_BLOCK = 512
RWKV_CHUNK = 64
RWKV_BLOCK = 256
RWKV_GROUP = 4
PAIR = 2 * RWKV_HEAD_DIM
FFN_COLS = 256
VMEM_LIMIT = 56 * 1024 * 1024

NN = (((1,), (0,)), ((), ()))
NT = (((1,), (1,)), ((), ()))
TN = (((0,), (0,)), ((), ()))


def _mm(a, b, dims=NN):
    return lax.dot_general(a, b, dims, preferred_element_type=F32)


def _dot1(a, b, dims=NN):
    return _mm(a.astype(BF16), b.astype(BF16), dims)


def _split3(a):
    a1 = a.astype(BF16)
    r1 = a - a1.astype(F32)
    a2 = r1.astype(BF16)
    a3 = (r1 - a2.astype(F32)).astype(BF16)
    return a1, a2, a3


def _dot_sel(sel_bf16, a):
    a1, a2, a3 = _split3(a)
    return _mm(sel_bf16, a1) + (_mm(sel_bf16, a2) + _mm(sel_bf16, a3))


def _a_dot_sel(a, sel_bf16):
    a1, a2, a3 = _split3(a)
    return _mm(a1, sel_bf16) + (_mm(a2, sel_bf16) + _mm(a3, sel_bf16))


def _dot_sel2(sel_bf16, a):
    hi = a.astype(BF16)
    lo = (a - hi.astype(F32)).astype(BF16)
    return _mm(sel_bf16, hi) + _mm(sel_bf16, lo)


def _shift_rows(x, tail, k):
    sub = _iota2((SUBLANES, x.shape[1]), 0)
    r = pltpu.roll(x, k, axis=0)
    head = jnp.where(sub < k, pltpu.roll(tail, k, axis=0), r[0:SUBLANES, :])
    return jnp.concatenate([head, r[SUBLANES:, :]], axis=0)


def _rms(x, g):
    return x * lax.rsqrt(jnp.mean(x * x, axis=-1, keepdims=True) + NORM_EPS) * g


def _sigmoid(x):
    return 0.5 + 0.5 * jnp.tanh(0.5 * x)


def _silu(x):
    h = 0.5 * x
    return h + h * jnp.tanh(h)


def _softplus(x):
    return jnp.maximum(x, 0.0) + jnp.log(1.0 + jnp.exp(-jnp.abs(x)))


def _iota2(shape, axis):
    return lax.broadcasted_iota(jnp.int32, shape, axis)


def _inproj_kernel(x_ref, g_ref, ws_ref, wr_ref, ps_ref, pr_ref):
    xb = _rms(x_ref[...], g_ref[...]).astype(BF16)
    ps_ref[...] = _mm(xb, ws_ref[...])
    pr_ref[...] = _mm(xb, wr_ref[...])


def _inproj(x2, g, w_ssd, w_rwkv, tm):
    n = x2.shape[0]
    const = lambda i: (0, 0)
    return pl.pallas_call(
        _inproj_kernel,
        name="inproj",
        grid=(n // tm,),
        in_specs=[
            pl.BlockSpec((tm, D_MODEL), lambda i: (i, 0)),
            pl.BlockSpec((1, D_MODEL), const),
            pl.BlockSpec((D_MODEL, SSD_PCOLS), const, pipeline_mode=pl.Buffered(1)),
            pl.BlockSpec((D_MODEL, RWKV_COLS), const, pipeline_mode=pl.Buffered(1)),
        ],
        out_specs=[
            pl.BlockSpec((tm, SSD_PCOLS), lambda i: (i, 0)),
            pl.BlockSpec((tm, RWKV_COLS), lambda i: (i, 0)),
        ],
        out_shape=[
            jax.ShapeDtypeStruct((n, SSD_PCOLS), F32),
            jax.ShapeDtypeStruct((n, RWKV_COLS), F32),
        ],
        compiler_params=pltpu.CompilerParams(
            dimension_semantics=("arbitrary",), vmem_limit_bytes=VMEM_LIMIT),
    )(x2, g, w_ssd, w_rwkv)


def _ssd_kernel(p_ref, cw_ref, cb_ref, dtb_ref, alog_ref, dsk_ref, ng_ref, o_ref, buf_ref, st_ref):
    L = SSD_CHUNK
    gw = SSD_D // SSD_GROUPS

    @pl.when(pl.program_id(1) == 0)
    def _():
        buf_ref[...] = jnp.zeros(buf_ref.shape, F32)
        st_ref[...] = jnp.zeros(st_ref.shape, F32)

    row = _iota2((L, L), 0)
    col = _iota2((L, L), 1)
    causal = row >= col
    tril = jnp.where(causal, 1.0, 0.0).astype(BF16)
    hsel = (_iota2((LANES, SSD_D), 1) // SSD_HEAD_DIM == _iota2((LANES, SSD_D), 0))
    hsel = jnp.where(hsel, 1.0, 0.0).astype(BF16)
    lane_lo = _iota2((L, LANES), 1) < SSD_HEAD_DIM
    xbc_cols = slice(SSD_D, SSD_D + SSD_CONV_DIM)

    for c in range(SSD_BLOCK // L):
        rs = slice(c * L, (c + 1) * L)
        z = p_ref[rs, 0:SSD_D]
        xbc_raw = p_ref[rs, xbc_cols]
        dt_raw = p_ref[rs, SSD_D + SSD_CONV_DIM:SSD_PCOLS]

        tail = buf_ref[...] if c == 0 else p_ref[c * L - SUBLANES:c * L, xbc_cols]
        acc = cb_ref[...] + cw_ref[SSD_CONV - 1:SSD_CONV, :] * xbc_raw
        for k in range(1, SSD_CONV):
            acc = acc + cw_ref[SSD_CONV - 1 - k:SSD_CONV - k, :] * _shift_rows(xbc_raw, tail, k)
        xbc = _silu(acc)
        xs = xbc[:, 0:SSD_D]

        dt = _softplus(dt_raw + dtb_ref[...])
        a = dt * (-LOG2_E * jnp.exp(alog_ref[...]))
        a_cs = _dot_sel(tril, a)
        a_cs_t = a_cs.T
        dt_e = _mm(dt.astype(BF16), hsel)
        acs_e = _a_dot_sel(a_cs, hsel)
        ea_e = jnp.exp2(acs_e)
        ds_e = jnp.exp2(acs_e[L - 1:L, :] - acs_e)

        x_dt = xs * dt_e
        x_b = x_dt.astype(BF16)
        x_dec = (x_dt * ds_e).astype(BF16)

        for g in range(SSD_GROUPS):
            gs = slice(g * gw, (g + 1) * gw)
            b_g = xbc[:, SSD_D + g * SSD_STATE:SSD_D + (g + 1) * SSD_STATE].astype(BF16)
            c_off = SSD_D + SSD_GROUPS * SSD_STATE
            c_g = xbc[:, c_off + g * SSD_STATE:c_off + (g + 1) * SSD_STATE].astype(BF16)
            scores = _mm(c_g, b_g, NT)
            state = st_ref[g]
            y_off = _mm(c_g, state.astype(BF16)) * ea_e[:, gs]
            st_ref[g] = state * ea_e[L - 1:L, gs] + _mm(b_g, x_dec[:, gs], TN)
            y_parts = []
            for j in range(gw // LANES):
                h0 = g * (SSD_HEADS // SSD_GROUPS) + 2 * j
                ms = []
                for h in (h0, h0 + 1):
                    seg = a_cs[:, h:h + 1] - a_cs_t[h:h + 1, :]
                    dec = jnp.exp2(jnp.where(causal, seg, -jnp.inf))
                    ms.append((scores * dec).astype(BF16))
                xp = x_b[:, h0 * SSD_HEAD_DIM:h0 * SSD_HEAD_DIM + LANES]
                zero = jnp.zeros_like(xp)
                x_bd = jnp.concatenate([jnp.where(lane_lo, xp, zero), jnp.where(lane_lo, zero, xp)], axis=0)
                y_parts.append(_mm(jnp.concatenate(ms, axis=1), x_bd))
            y = jnp.concatenate(y_parts, axis=1) + y_off
            y = y + dsk_ref[:, gs] * xs[:, gs]
            y = y * _silu(z[:, gs])
            y = y * lax.rsqrt(jnp.mean(y * y, axis=-1, keepdims=True) + NORM_EPS)
            o_ref[rs, gs] = (y * ng_ref[:, gs]).astype(o_ref.dtype)
    buf_ref[...] = p_ref[SSD_BLOCK - SUBLANES:SSD_BLOCK, xbc_cols]


def _ssd(p_ssd, cw, cb, dtb, alog, dsk, ng, batch, seq):
    L = SSD_BLOCK
    nc = seq // L
    const = lambda b, c: (0, 0)
    return pl.pallas_call(
        _ssd_kernel,
        name="ssd",
        grid=(batch, nc),
        in_specs=[
            pl.BlockSpec((L, SSD_PCOLS), lambda b, c: (b * nc + c, 0)),
            pl.BlockSpec((SSD_CONV, SSD_CONV_DIM), const),
            pl.BlockSpec((1, SSD_CONV_DIM), const),
            pl.BlockSpec((1, LANES), const),
            pl.BlockSpec((1, LANES), const),
            pl.BlockSpec((1, SSD_D), const),
            pl.BlockSpec((1, SSD_D), const),
        ],
        out_specs=pl.BlockSpec((L, SSD_D), lambda b, c: (b * nc + c, 0)),
        out_shape=jax.ShapeDtypeStruct((batch * seq, SSD_D), BF16),
        scratch_shapes=[
            pltpu.VMEM((SUBLANES, SSD_CONV_DIM), F32),
            pltpu.VMEM((SSD_GROUPS, SSD_STATE, SSD_D // SSD_GROUPS), F32),
        ],
        compiler_params=pltpu.CompilerParams(
            dimension_semantics=("arbitrary", "arbitrary"), vmem_limit_bytes=VMEM_LIMIT),
    )(p_ssd, cw, cb, dtb, alog, dsk, ng)


def _seg_sum(x, lane_lo):
    s_lo = jnp.sum(jnp.where(lane_lo, x, 0.0), axis=-1, keepdims=True)
    s_hi = jnp.sum(jnp.where(lane_lo, 0.0, x), axis=-1, keepdims=True)
    return jnp.where(lane_lo, s_lo, s_hi)


def _stack_heads(x, lane_lo):
    zero = jnp.zeros_like(x)
    return jnp.concatenate([jnp.where(lane_lo, x, zero), jnp.where(lane_lo, zero, x)], axis=0)


def _unit_lower_inverse_many(a_list, row, col, eye):
    blk8 = row // 8 == col // 8
    a8 = [jnp.where(blk8, a, 0.0).astype(BF16) for a in a_list]
    t = [eye + x.astype(F32) for x in a8]
    a2 = [_mm(x, x).astype(BF16) for x in a8]
    t = [ti + _mm(ti.astype(BF16), x) for ti, x in zip(t, a2)]
    a4 = [_mm(x, x).astype(BF16) for x in a2]
    t = [ti + _mm(ti.astype(BF16), x) for ti, x in zip(t, a4)]
    n = a_list[0].shape[0]
    for s in (8, 16, 32):
        lower_left = (row // (2 * s) == col // (2 * s)) & ((row // s) % 2 == 1) & ((col // s) % 2 == 0)
        second = [slice(r0 + s, r0 + 2 * s) for r0 in range(0, n, 2 * s)]
        tb = [ti.astype(BF16) for ti in t]
        t2 = [jnp.concatenate([ti[r, :] for r in second], axis=0) for ti in t]
        x = [_mm(t2i.astype(BF16), jnp.where(lower_left, a, 0.0).astype(BF16)).astype(BF16)
             for t2i, a in zip(t2, a_list)]
        t2 = [t2i + _mm(xi, tbi) for t2i, xi, tbi in zip(t2, x, tb)]
        t = [jnp.concatenate([piece for m, r in enumerate(second)
                              for piece in (ti[r.start - s:r.start, :], t2i[m * s:(m + 1) * s, :])], axis=0)
             for ti, t2i in zip(t, t2)]
    return t


def _rwkv_kernel(p_ref, mu_ref, w0_ref, w2_ref, a0_ref, a2_ref, g2_ref, kk_ref, ka_ref, rk_ref,
                 lnw_ref, lnb_ref, o_ref, buf_ref, st_ref, pre_ref):
    L = RWKV_CHUNK
    TB = RWKV_BLOCK
    H2 = 2 * L
    D = RWKV_D
    n_pairs = D // PAIR

    @pl.when(pl.program_id(1) == 0)
    def _():
        buf_ref[...] = jnp.zeros(buf_ref.shape, F32)
        st_ref[...] = jnp.zeros(st_ref.shape, F32)

    p = p_ref[...]
    prev = _shift_rows(p, buf_ref[...], 1)
    buf_ref[...] = p[TB - SUBLANES:TB, :]
    pm = p + (prev - p) * mu_ref[...]

    k = pm[:, D:2 * D]
    wa = pm[:, 3 * D:3 * D + LANES]
    g_lo = pm[:, 3 * D + LANES:3 * D + 2 * LANES]
    w_log = -_softplus(-(w0_ref[...] + _dot1(jnp.tanh(wa), w2_ref[...]))) - 0.5
    lw = -jnp.exp(w_log)
    alr = _sigmoid(a0_ref[...] + _dot1(wa, a2_ref[...]))
    CB = 4 * L
    blk_tril = (_iota2((CB, CB), 0) >= _iota2((CB, CB), 1)) & (_iota2((CB, CB), 0) // L == _iota2((CB, CB), 1) // L)
    blk_tril = jnp.where(blk_tril, 1.0, 0.0).astype(BF16)
    R_, K_, V_, KK_, ALR_, LW_, CS_, G_ = range(8)
    pre_ref[R_] = pm[:, 0:D]
    pre_ref[K_] = k * (1.0 + (alr - 1.0) * ka_ref[...])
    pre_ref[V_] = pm[:, 2 * D:3 * D]
    pre_ref[KK_] = k * kk_ref[...]
    pre_ref[ALR_] = alr
    pre_ref[LW_] = lw
    for r0 in range(0, TB, CB):
        pre_ref[CS_, r0:r0 + CB, :] = _dot_sel2(blk_tril, lw[r0:r0 + CB, :])
    pre_ref[G_] = _dot1(_sigmoid(g_lo), g2_ref[...])

    lane_lo = _iota2((L, PAIR), 1) < RWKV_HEAD_DIM
    row = _iota2((H2, H2), 0)
    col = _iota2((H2, H2), 1)
    eye = jnp.where(row == col, 1.0, 0.0)
    same_head = row // L == col // L
    strict = same_head & (row % L > col % L)
    incl = same_head & (row % L >= col % L)

    def independent_part(chunks):
        lhs_a, v_t, a_ab, a_ak, a_rk, lr_arb, hat_bk, p_all = ([] for _ in range(8))
        for c, q in [(c, q) for c in chunks for q in range(n_pairs)]:
            rs = slice(c * L, (c + 1) * L)
            sl = slice(q * PAIR, (q + 1) * PAIR)
            cs_p = pre_ref[CS_, rs, sl]
            cs_last = pre_ref[CS_, (c + 1) * L - 1:(c + 1) * L, sl]
            p_inv = jnp.exp(-cs_p)
            p_end = jnp.exp(cs_last - cs_p)
            kk_p = pre_ref[KK_, rs, sl]
            kk_n = kk_p * lax.rsqrt(jnp.maximum(_seg_sum(kk_p * kk_p, lane_lo), 1e-24))
            b_p = kk_n * pre_ref[ALR_, rs, sl]
            k_p = pre_ref[K_, rs, sl]
            la = _stack_heads(-kk_n * jnp.exp(cs_p - pre_ref[LW_, rs, sl]), lane_lo).astype(BF16)
            lr = _stack_heads(pre_ref[R_, rs, sl] * jnp.exp(cs_p), lane_lo).astype(BF16)
            bt = (b_p * p_inv).astype(BF16)
            kt = (k_p * p_inv).astype(BF16)
            g = _mm(jnp.concatenate([la, lr], axis=0), jnp.concatenate([bt, bt, kt, kt], axis=0), NT)
            a_ab.append(jnp.where(strict, g[0:H2, 0:H2], 0.0))
            a_ak.append(jnp.where(strict, g[0:H2, H2:], 0.0).astype(BF16))
            a_rk.append(jnp.where(incl, g[H2:, H2:], 0.0).astype(BF16))
            lr_arb.append(jnp.concatenate([lr, jnp.where(incl, g[H2:, 0:H2], 0.0).astype(BF16)], axis=1))
            lhs_a.append(la)
            hat_bk.append(jnp.concatenate([_stack_heads(b_p * p_end, lane_lo), _stack_heads(k_p * p_end, lane_lo)],
                                          axis=0).astype(BF16))
            p_all.append(jnp.exp(cs_last))
            v_t.append(_stack_heads(pre_ref[V_, rs, sl], lane_lo).T.astype(BF16))
        t_inv = [t.astype(BF16) for t in _unit_lower_inverse_many(a_ab, row, col, eye)]
        t_a = [_mm(t, la).astype(BF16) for t, la in zip(t_inv, lhs_a)]
        av_t = [_mm(vt, a, NT).astype(BF16) for vt, a in zip(v_t, a_ak)]
        tav_t = [_mm(x, t, NT) for x, t in zip(av_t, t_inv)]
        arkv_t = [_mm(vt, a, NT) for vt, a in zip(v_t, a_rk)]
        return dict(t_a=t_a, tav_t=tav_t, arkv_t=arkv_t, lr_arb=lr_arb, hat_bk=hat_bk, v_t=v_t, p_all=p_all)

    def state_part(chunks, d):
        for ci, c in enumerate(chunks):
            rs = slice(c * L, (c + 1) * L)
            idx = [ci * n_pairs + q for q in range(n_pairs)]
            s0 = [st_ref[q] for q in range(n_pairs)]
            s0b = [s.astype(BF16) for s in s0]
            u_tb = [(_mm(s0b[q], d["t_a"][i], NT) + d["tav_t"][i]).astype(BF16) for q, i in enumerate(idx)]
            for q, i in enumerate(idx):
                st_ref[q] = s0[q] * d["p_all"][i] + _mm(jnp.concatenate([u_tb[q], d["v_t"][i]], axis=1),
                                                        d["hat_bk"][i])
            y_t = [_mm(jnp.concatenate([s0b[q], u_tb[q]], axis=1), d["lr_arb"][i], NT) + d["arkv_t"][i]
                   for q, i in enumerate(idx)]
            for q in range(n_pairs):
                sl = slice(q * PAIR, (q + 1) * PAIR)
                y_st = y_t[q].T
                y = y_st[0:L, :] + y_st[L:H2, :]
                mean = _seg_sum(y, lane_lo) * (1.0 / RWKV_HEAD_DIM)
                dev = y - mean
                var = _seg_sum(dev * dev, lane_lo) * (1.0 / RWKV_HEAD_DIM)
                yn = dev * lax.rsqrt(var + RWKV_GN_EPS) * lnw_ref[:, sl] + lnb_ref[:, sl]
                v_p = pre_ref[V_, rs, sl]
                bonus = _seg_sum(pre_ref[R_, rs, sl] * pre_ref[K_, rs, sl] * rk_ref[:, sl], lane_lo) * v_p
                o_ref[rs, sl] = ((yn + bonus) * pre_ref[G_, rs, sl]).astype(o_ref.dtype)

    groups = [list(range(c0, c0 + RWKV_GROUP)) for c0 in range(0, TB // L, RWKV_GROUP)]
    ready = [independent_part(g) for g in groups]
    for g, d in zip(groups, ready):
        state_part(g, d)


def _rwkv(p_rwkv, mu, w0, w2p, a0, a2p, g2, k_k, k_a, r_k, ln_w, ln_b, batch, seq):
    TB = RWKV_BLOCK
    nb = seq // TB
    const = lambda b, c: (0, 0)
    vec = pl.BlockSpec((1, RWKV_D), const)
    return pl.pallas_call(
        _rwkv_kernel,
        name="rwkv7",
        grid=(batch, nb),
        in_specs=[
            pl.BlockSpec((TB, RWKV_COLS), lambda b, c: (b * nb + c, 0)),
            pl.BlockSpec((1, RWKV_COLS), const),
            vec,
            pl.BlockSpec((LANES, RWKV_D), const),
            vec,
            pl.BlockSpec((LANES, RWKV_D), const),
            pl.BlockSpec((GATE_LORA, RWKV_D), const),
            vec, vec, vec, vec, vec,
        ],
        out_specs=pl.BlockSpec((TB, RWKV_D), lambda b, c: (b * nb + c, 0)),
        out_shape=jax.ShapeDtypeStruct((batch * seq, RWKV_D), BF16),
        scratch_shapes=[
            pltpu.VMEM((SUBLANES, RWKV_COLS), F32),
            pltpu.VMEM((RWKV_D // PAIR, PAIR, PAIR), F32),
            pltpu.VMEM((8, TB, RWKV_D), F32),
        ],
        compiler_params=pltpu.CompilerParams(
            dimension_semantics=("arbitrary", "arbitrary"), vmem_limit_bytes=VMEM_LIMIT),
    )(p_rwkv, mu, w0, w2p, a0, a2p, g2, k_k, k_a, r_k, ln_w, ln_b)


def _ffn_kernel(ys_ref, yr_ref, x_ref, wo_ref, g1_ref, g2_ref, wup_ref, cw_ref, cb_ref, wdn_ref, g3_ref,
                o_ref, ubuf_ref, *, tm):
    @pl.when(pl.program_id(1) == 0)
    def _():
        ubuf_ref[...] = jnp.zeros(ubuf_ref.shape, F32)

    mix = _mm(ys_ref[...], wo_ref[0:SSD_D, :]) + _mm(yr_ref[...], wo_ref[SSD_D:, :])
    h = x_ref[...] + _rms(mix, g1_ref[...])
    hn = _rms(h, g2_ref[...]).astype(BF16)
    sub = _iota2((SUBLANES, FFN_COLS), 0)

    def up(j):
        return [_mm(hn, wup_ref[:, c0:c0 + FFN_COLS]) for c0 in (j * FFN_COLS, D_FF + j * FFN_COLS)]

    def conv(u, c0):
        cs = slice(c0, c0 + FFN_COLS)
        tail = ubuf_ref[:, cs]
        out = cb_ref[:, cs] + cw_ref[FFN_CONV - 1:FFN_CONV, cs] * u
        for k in range(1, FFN_CONV):
            r = pltpu.roll(u, k, axis=0)
            head = jnp.where(sub < k, pltpu.roll(tail, k, axis=0), r[0:SUBLANES, :])
            shifted = jnp.concatenate([head, r[SUBLANES:, :]], axis=0)
            out = out + cw_ref[FFN_CONV - 1 - k:FFN_CONV - k, cs] * shifted
        ubuf_ref[:, cs] = u[tm - SUBLANES:tm, :]
        return out

    n_steps = D_FF // FFN_COLS
    u_next = up(0)
    acts = []
    for j in range(n_steps):
        u_gate, u_val = u_next
        if j + 1 < n_steps:
            u_next = up(j + 1)
        gate = conv(u_gate, j * FFN_COLS)
        val = conv(u_val, D_FF + j * FFN_COLS)
        acts.append((_silu(gate) * val).astype(BF16))
    f = _mm(jnp.concatenate(acts, axis=1), wdn_ref[...])
    o_ref[...] = h + _rms(f, g3_ref[...])


def _ffn(ys, yr, x2, wo, g1, g2, wup, cw, cb, wdn, g3, batch, seq, tm):
    nb = seq // tm
    const = lambda b, i: (0, 0)
    rows = lambda b, i: (b * nb + i, 0)
    res = functools.partial(pl.BlockSpec, index_map=const, pipeline_mode=pl.Buffered(1))
    return pl.pallas_call(
        functools.partial(_ffn_kernel, tm=tm),
        name="outproj_ffn",
        grid=(batch, nb),
        in_specs=[
            pl.BlockSpec((tm, SSD_D), rows),
            pl.BlockSpec((tm, RWKV_D), rows),
            pl.BlockSpec((tm, D_MODEL), rows),
            res((SSD_D + RWKV_D, D_MODEL)),
            pl.BlockSpec((1, D_MODEL), const),
            pl.BlockSpec((1, D_MODEL), const),
            res((D_MODEL, 2 * D_FF)),
            pl.BlockSpec((FFN_CONV, 2 * D_FF), const),
            pl.BlockSpec((1, 2 * D_FF), const),
            res((D_FF, D_MODEL)),
            pl.BlockSpec((1, D_MODEL), const),
        ],
        out_specs=pl.BlockSpec((tm, D_MODEL), rows),
        out_shape=jax.ShapeDtypeStruct((batch * seq, D_MODEL), F32),
        scratch_shapes=[
            pltpu.VMEM((SUBLANES, 2 * D_FF), F32),
        ],
        compiler_params=pltpu.CompilerParams(
            dimension_semantics=("arbitrary", "arbitrary"), vmem_limit_bytes=VMEM_LIMIT),
    )(ys, yr, x2, wo, g1, g2, wup, cw, cb, wdn, g3)


def _pad_lanes(v):
    return jnp.pad(v.astype(F32), (0, LANES - v.shape[0]))[None, :]


def _layer(h2, batch, seq, pre_mix_norm, w_in, ssd_conv_w, ssd_conv_b, ssd_dt_bias, ssd_a_log, ssd_d, ssd_norm,
           rwkv_mu, rwkv_w0, rwkv_w2, rwkv_a0, rwkv_a2, rwkv_g2, rwkv_k_k, rwkv_k_a, rwkv_r_k,
           rwkv_ln_w, rwkv_ln_b, w_out, post_mix_norm, pre_ffn_norm, ffn_w_up, ffn_conv_w,
           ffn_conv_b, ffn_w_down, post_ffn_norm):
    row = lambda v: v.astype(F32).reshape(1, -1)

    w_ssd = jnp.pad(w_in[:, :SSD_COLS], ((0, 0), (0, SSD_PCOLS - SSD_COLS))).astype(BF16)
    i1 = RWKV_D
    i2 = i1 + DECAY_LORA
    i3 = i2 + RWKV_D
    i4 = i3 + RWKV_D
    i5 = i4 + AAA_LORA
    perm = lambda t: jnp.concatenate(
        [t[..., 0:i1], t[..., i2:i3], t[..., i3:i4], t[..., i1:i2], t[..., i4:i5], t[..., i5:]], axis=-1)
    w_rwkv = perm(w_in[:, SSD_COLS:]).astype(BF16)
    mu = perm(rwkv_mu).astype(F32).reshape(1, -1)
    w2p = jnp.concatenate([rwkv_w2, jnp.zeros((AAA_LORA, RWKV_D), rwkv_w2.dtype)], axis=0).astype(BF16)
    a2p = jnp.concatenate([jnp.zeros((DECAY_LORA, RWKV_D), rwkv_a2.dtype), rwkv_a2], axis=0).astype(BF16)

    p_ssd, p_rwkv = _inproj(h2, row(pre_mix_norm), w_ssd, w_rwkv, tm=512)
    y_ssd = _ssd(p_ssd, ssd_conv_w.astype(F32), row(ssd_conv_b), _pad_lanes(ssd_dt_bias), _pad_lanes(ssd_a_log),
                 row(jnp.repeat(ssd_d, SSD_HEAD_DIM)), row(ssd_norm), batch, seq)
    y_rwkv = _rwkv(p_rwkv, mu, row(rwkv_w0), w2p, row(rwkv_a0), a2p, rwkv_g2.astype(BF16), row(rwkv_k_k),
                   row(rwkv_k_a), row(rwkv_r_k), row(rwkv_ln_w), row(rwkv_ln_b), batch, seq)
    return _ffn(y_ssd, y_rwkv, h2, w_out.astype(BF16), row(post_mix_norm), row(pre_ffn_norm),
                ffn_w_up.astype(BF16), ffn_conv_w.astype(F32), row(ffn_conv_b), ffn_w_down.astype(BF16),
                row(post_ffn_norm), batch, seq, tm=512)


def kernel(x, pre_mix_norm, w_in, ssd_conv_w, ssd_conv_b, ssd_dt_bias, ssd_a_log, ssd_d, ssd_norm, rwkv_mu, rwkv_w0, rwkv_w2, rwkv_a0, rwkv_a2, rwkv_g2, rwkv_k_k, rwkv_k_a, rwkv_r_k, rwkv_ln_w, rwkv_ln_b, w_out, post_mix_norm, pre_ffn_norm, ffn_w_up, ffn_conv_w, ffn_conv_b, ffn_w_down, post_ffn_norm):
    batch, seq, d = x.shape
    h2 = x.reshape(batch * seq, d)
    params = (pre_mix_norm, w_in, ssd_conv_w, ssd_conv_b, ssd_dt_bias, ssd_a_log, ssd_d, ssd_norm, rwkv_mu,
              rwkv_w0, rwkv_w2, rwkv_a0, rwkv_a2, rwkv_g2, rwkv_k_k, rwkv_k_a, rwkv_r_k, rwkv_ln_w, rwkv_ln_b,
              w_out, post_mix_norm, pre_ffn_norm, ffn_w_up, ffn_conv_w, ffn_conv_b, ffn_w_down, post_ffn_norm)
    for l in range(pre_mix_norm.shape[0]):
        h2 = _layer(h2, batch, seq, *(t[l] for t in params))
    return h2.reshape(batch, seq, d)
```

```python
import functools

import jax
import jax.numpy as jnp
from jax import lax
from jax.experimental import pallas as pl
from jax.experimental.pallas import tpu as pltpu

F32 = jnp.float32
BF16 = jnp.bfloat16

D_MODEL = 1024
SSD_HEADS = 8
SSD_HEAD_DIM = 64
SSD_D = SSD_HEADS * SSD_HEAD_DIM
SSD_GROUPS = 2
SSD_STATE = 128
SSD_CONV = 4
SSD_CHUNK = 128
SSD_CONV_DIM = SSD_D + 2 * SSD_GROUPS * SSD_STATE
SSD_COLS = SSD_D + SSD_CONV_DIM + SSD_HEADS
RWKV_HEADS = 8
RWKV_HEAD_DIM = 64
RWKV_D = RWKV_HEADS * RWKV_HEAD_DIM
DECAY_LORA = 64
AAA_LORA = 64
GATE_LORA = 128
RWKV_COLS = 3 * RWKV_D + DECAY_LORA + AAA_LORA + GATE_LORA
RWKV_GN_EPS = 64e-5
D_FF = 2816
FFN_CONV = 3
NORM_EPS = 1e-6
LOG2_E = 1.4426950408889634
EXP_M_HALF = 0.6065306597126334

LANES = 128
SUBLANES = 8
SSD_PCOLS = SSD_D + SSD_CONV_DIM + LANES
SSD_BLOCK = 512
RWKV_CHUNK = 64
RWKV_BLOCK = 512
INPROJ_COLS = 256
RWKV_GROUP = 4
PAIR = 2 * RWKV_HEAD_DIM
FFN_COLS = 256
VMEM_LIMIT = 56 * 1024 * 1024

NN = (((1,), (0,)), ((), ()))
NT = (((1,), (1,)), ((), ()))
TN = (((0,), (0,)), ((), ()))


def _mm(a, b, dims=NN):
    return lax.dot_general(a, b, dims, preferred_element_type=F32)


def _dot1(a, b, dims=NN):
    return _mm(a.astype(BF16), b.astype(BF16), dims)


def _split3(a):
    a1 = a.astype(BF16)
    r1 = a - a1.astype(F32)
    a2 = r1.astype(BF16)
    a3 = (r1 - a2.astype(F32)).astype(BF16)
    return a1, a2, a3


def _dot_sel(sel_bf16, a):
    a1, a2, a3 = _split3(a)
    return _mm(sel_bf16, a1) + (_mm(sel_bf16, a2) + _mm(sel_bf16, a3))


def _a_dot_sel(a, sel_bf16):
    a1, a2, a3 = _split3(a)
    return _mm(a1, sel_bf16) + (_mm(a2, sel_bf16) + _mm(a3, sel_bf16))


def _dot_sel2(sel_bf16, a):
    hi = a.astype(BF16)
    lo = (a - hi.astype(F32)).astype(BF16)
    return _mm(sel_bf16, hi) + _mm(sel_bf16, lo)


def _shift_rows(x, tail, k):
    sub = _iota2((SUBLANES, x.shape[1]), 0)
    r = pltpu.roll(x, k, axis=0)
    head = jnp.where(sub < k, pltpu.roll(tail, k, axis=0), r[0:SUBLANES, :])
    return jnp.concatenate([head, r[SUBLANES:, :]], axis=0)


def _rms(x, g):
    return x * lax.rsqrt(jnp.mean(x * x, axis=-1, keepdims=True) + NORM_EPS) * g


def _sigmoid(x):
    return 0.5 + 0.5 * jnp.tanh(0.5 * x)


def _silu(x):
    h = 0.5 * x
    return h + h * jnp.tanh(h)


def _softplus(x):
    return jnp.maximum(x, 0.0) + jnp.log(1.0 + jnp.exp(-jnp.abs(x)))


def _iota2(shape, axis):
    return lax.broadcasted_iota(jnp.int32, shape, axis)


def _inproj_kernel(x_ref, g_ref, ws_ref, wr_ref, cw_ref, cb_ref, mu_ref, ps_ref, pr_ref, tail_s_ref, tail_r_ref,
                   *, tm, blocks_per_seq):
    i = pl.program_id(0)

    @pl.when(i == 0)
    def _():
        tail_s_ref[...] = jnp.zeros(tail_s_ref.shape, F32)
        tail_r_ref[...] = jnp.zeros(tail_r_ref.shape, F32)

    first = i % blocks_per_seq == 0
    xb = _rms(x_ref[...], g_ref[...]).astype(BF16)
    xbc0 = SSD_D
    dt0 = SSD_D + SSD_CONV_DIM
    ps_ref[:, 0:xbc0] = _mm(xb, ws_ref[:, 0:xbc0])
    ps_ref[:, dt0:SSD_PCOLS] = _mm(xb, ws_ref[:, dt0:SSD_PCOLS])
    for c0 in range(0, SSD_CONV_DIM, INPROJ_COLS):
        ts = slice(c0, c0 + INPROJ_COLS)
        cs = slice(xbc0 + c0, xbc0 + c0 + INPROJ_COLS)
        u = _mm(xb, ws_ref[:, cs])
        tail = jnp.where(first, 0.0, tail_s_ref[:, ts])
        acc = cb_ref[:, ts] + cw_ref[SSD_CONV - 1:SSD_CONV, ts] * u
        for k in range(1, SSD_CONV):
            acc = acc + cw_ref[SSD_CONV - 1 - k:SSD_CONV - k, ts] * _shift_rows(u, tail, k)
        tail_s_ref[:, ts] = u[tm - SUBLANES:tm, :]
        ps_ref[:, cs] = _silu(acc)
    for c0 in range(0, RWKV_COLS, INPROJ_COLS):
        cs = slice(c0, c0 + INPROJ_COLS)
        p = _mm(xb, wr_ref[:, cs])
        tail = jnp.where(first, 0.0, tail_r_ref[:, cs])
        prev = _shift_rows(p, tail, 1)
        tail_r_ref[:, cs] = p[tm - SUBLANES:tm, :]
        pr_ref[:, cs] = p + (prev - p) * mu_ref[:, cs]


def _inproj(x2, g, w_ssd, w_rwkv, cw, cb, mu, seq, tm):
    n = x2.shape[0]
    const = lambda i: (0, 0)
    return pl.pallas_call(
        functools.partial(_inproj_kernel, tm=tm, blocks_per_seq=seq // tm),
        name="inproj",
        grid=(n // tm,),
        in_specs=[
            pl.BlockSpec((tm, D_MODEL), lambda i: (i, 0)),
            pl.BlockSpec((1, D_MODEL), const),
            pl.BlockSpec((D_MODEL, SSD_PCOLS), const, pipeline_mode=pl.Buffered(1)),
            pl.BlockSpec((D_MODEL, RWKV_COLS), const, pipeline_mode=pl.Buffered(1)),
            pl.BlockSpec((SSD_CONV, SSD_CONV_DIM), const),
            pl.BlockSpec((1, SSD_CONV_DIM), const),
            pl.BlockSpec((1, RWKV_COLS), const),
        ],
        out_specs=[
            pl.BlockSpec((tm, SSD_PCOLS), lambda i: (i, 0)),
            pl.BlockSpec((tm, RWKV_COLS), lambda i: (i, 0)),
        ],
        out_shape=[
            jax.ShapeDtypeStruct((n, SSD_PCOLS), F32),
            jax.ShapeDtypeStruct((n, RWKV_COLS), F32),
        ],
        scratch_shapes=[
            pltpu.VMEM((SUBLANES, SSD_CONV_DIM), F32),
            pltpu.VMEM((SUBLANES, RWKV_COLS), F32),
        ],
        compiler_params=pltpu.CompilerParams(
            dimension_semantics=("arbitrary",), vmem_limit_bytes=VMEM_LIMIT),
    )(x2, g, w_ssd, w_rwkv, cw, cb, mu)


def _ssd_parts(p_ref, dtb_ref, alog_ref, dsk_ref, ng_ref, o_ref, st_ref):
    L = SSD_CHUNK
    gw = SSD_D // SSD_GROUPS

    @pl.when(pl.program_id(1) == 0)
    def _():
        st_ref[...] = jnp.zeros(st_ref.shape, F32)

    row = _iota2((L, L), 0)
    col = _iota2((L, L), 1)
    causal = row >= col
    tril = jnp.where(causal, 1.0, 0.0).astype(BF16)
    hsel = (_iota2((LANES, SSD_D), 1) // SSD_HEAD_DIM == _iota2((LANES, SSD_D), 0))
    hsel = jnp.where(hsel, 1.0, 0.0).astype(BF16)
    lane_lo = _iota2((L, LANES), 1) < SSD_HEAD_DIM
    xbc_cols = slice(SSD_D, SSD_D + SSD_CONV_DIM)

    def chunk(c):
        rs = slice(c * L, (c + 1) * L)
        z = p_ref[rs, 0:SSD_D]
        xbc = p_ref[rs, xbc_cols]
        dt_raw = p_ref[rs, SSD_D + SSD_CONV_DIM:SSD_PCOLS]
        xs = xbc[:, 0:SSD_D]

        dt = _softplus(dt_raw + dtb_ref[...])
        a = dt * (-LOG2_E * jnp.exp(alog_ref[...]))
        a_cs = _dot_sel(tril, a)
        a_cs_t = a_cs.T
        dt_e = _mm(dt.astype(BF16), hsel)
        acs_e = _a_dot_sel(a_cs, hsel)
        ea_e = jnp.exp2(acs_e)
        ds_e = jnp.exp2(acs_e[L - 1:L, :] - acs_e)

        x_dt = xs * dt_e
        x_b = x_dt.astype(BF16)
        x_dec = (x_dt * ds_e).astype(BF16)

        for g in range(SSD_GROUPS):
            gs = slice(g * gw, (g + 1) * gw)
            b_g = xbc[:, SSD_D + g * SSD_STATE:SSD_D + (g + 1) * SSD_STATE].astype(BF16)
            c_off = SSD_D + SSD_GROUPS * SSD_STATE
            c_g = xbc[:, c_off + g * SSD_STATE:c_off + (g + 1) * SSD_STATE].astype(BF16)
            scores = _mm(c_g, b_g, NT)
            state = st_ref[g]
            y_off = _mm(c_g, state.astype(BF16)) * ea_e[:, gs]
            st_ref[g] = state * ea_e[L - 1:L, gs] + _mm(b_g, x_dec[:, gs], TN)
            y_parts = []
            for j in range(gw // LANES):
                h0 = g * (SSD_HEADS // SSD_GROUPS) + 2 * j
                ms = []
                for h in (h0, h0 + 1):
                    seg = a_cs[:, h:h + 1] - a_cs_t[h:h + 1, :]
                    dec = jnp.exp2(jnp.where(causal, seg, -jnp.inf))
                    ms.append((scores * dec).astype(BF16))
                xp = x_b[:, h0 * SSD_HEAD_DIM:h0 * SSD_HEAD_DIM + LANES]
                zero = jnp.zeros_like(xp)
                x_bd = jnp.concatenate([jnp.where(lane_lo, xp, zero), jnp.where(lane_lo, zero, xp)], axis=0)
                y_parts.append(_mm(jnp.concatenate(ms, axis=1), x_bd))
            y = jnp.concatenate(y_parts, axis=1) + y_off
            y = y + dsk_ref[:, gs] * xs[:, gs]
            y = y * _silu(z[:, gs])
            y = y * lax.rsqrt(jnp.mean(y * y, axis=-1, keepdims=True) + NORM_EPS)
            o_ref[rs, gs] = (y * ng_ref[:, gs]).astype(o_ref.dtype)
    return chunk


def _seg_sum(x, lane_lo):
    s_lo = jnp.sum(jnp.where(lane_lo, x, 0.0), axis=-1, keepdims=True)
    s_hi = jnp.sum(jnp.where(lane_lo, 0.0, x), axis=-1, keepdims=True)
    return jnp.where(lane_lo, s_lo, s_hi)


def _stack_heads(x, lane_lo):
    zero = jnp.zeros_like(x)
    return jnp.concatenate([jnp.where(lane_lo, x, zero), jnp.where(lane_lo, zero, x)], axis=0)


def _unit_lower_inverse_many(a_list, row, col, eye):
    blk8 = row // 8 == col // 8
    a8f = [jnp.where(blk8, a, 0.0) for a in a_list]
    a8 = [x.astype(BF16) for x in a8f]
    t = [eye + x for x in a8f]
    a2 = [_mm(x, x).astype(BF16) for x in a8]
    t = [ti + _mm(ti.astype(BF16), x) for ti, x in zip(t, a2)]
    a4 = [_mm(x, x).astype(BF16) for x in a2]
    t = [ti + _mm(ti.astype(BF16), x) for ti, x in zip(t, a4)]
    n = a_list[0].shape[0]
    for s in (8, 16, 32):
        lower_left = (row // (2 * s) == col // (2 * s)) & ((row // s) % 2 == 1) & ((col // s) % 2 == 0)
        second = [slice(r0 + s, r0 + 2 * s) for r0 in range(0, n, 2 * s)]
        tb = [ti.astype(BF16) for ti in t]
        t2 = [jnp.concatenate([ti[r, :] for r in second], axis=0) for ti in t]
        x = [_mm(t2i.astype(BF16), jnp.where(lower_left, a, 0.0).astype(BF16)).astype(BF16)
             for t2i, a in zip(t2, a_list)]
        t2 = [t2i + _mm(xi, tbi) for t2i, xi, tbi in zip(t2, x, tb)]
        t = [jnp.concatenate([piece for m, r in enumerate(second)
                              for piece in (ti[r.start - s:r.start, :], t2i[m * s:(m + 1) * s, :])], axis=0)
             for ti, t2i in zip(t, t2)]
    return t


def _rwkv_parts(p_ref, w0_ref, w2_ref, a0_ref, a2_ref, g2_ref, kk_ref, ka_ref, rk_ref,
                lnw_ref, lnb_ref, o_ref, st_ref, pre_ref):
    L = RWKV_CHUNK
    TB = RWKV_BLOCK
    H2 = 2 * L
    D = RWKV_D
    n_pairs = D // PAIR

    @pl.when(pl.program_id(1) == 0)
    def _():
        st_ref[...] = jnp.zeros(st_ref.shape, F32)

    k = p_ref[:, D:2 * D]
    wa = p_ref[:, 3 * D:3 * D + LANES]
    g_lo = p_ref[:, 3 * D + LANES:3 * D + 2 * LANES]
    lw = -EXP_M_HALF * _sigmoid(w0_ref[...] + _dot1(jnp.tanh(wa), w2_ref[...]))
    alr = _sigmoid(a0_ref[...] + _dot1(wa, a2_ref[...]))
    CB = 4 * L
    blk_tril = (_iota2((CB, CB), 0) >= _iota2((CB, CB), 1)) & (_iota2((CB, CB), 0) // L == _iota2((CB, CB), 1) // L)
    blk_tril = jnp.where(blk_tril, 1.0, 0.0).astype(BF16)
    K_, KK_, ALR_, LW_, CS_, G_ = range(6)
    v_cols = lambda sl: slice(2 * D + sl.start, 2 * D + sl.stop)
    pre_ref[K_] = k * (1.0 + (alr - 1.0) * ka_ref[...])
    pre_ref[KK_] = k * kk_ref[...]
    pre_ref[ALR_] = alr
    pre_ref[LW_] = lw
    for r0 in range(0, TB, CB):
        pre_ref[CS_, r0:r0 + CB, :] = _dot_sel2(blk_tril, lw[r0:r0 + CB, :])
    pre_ref[G_] = _dot1(_sigmoid(g_lo), g2_ref[...])

    lane_lo = _iota2((L, PAIR), 1) < RWKV_HEAD_DIM
    row = _iota2((H2, H2), 0)
    col = _iota2((H2, H2), 1)
    eye = jnp.where(row == col, 1.0, 0.0)
    same_head = row // L == col // L
    strict = same_head & (row % L > col % L)
    incl = same_head & (row % L >= col % L)

    def independent_part(chunks):
        lhs_a, v_t, a_ab, a_ak, a_rk, lr_arb, hat_bk, p_all = ([] for _ in range(8))
        for c, q in [(c, q) for c in chunks for q in range(n_pairs)]:
            rs = slice(c * L, (c + 1) * L)
            sl = slice(q * PAIR, (q + 1) * PAIR)
            cs_p = pre_ref[CS_, rs, sl]
            cs_last = pre_ref[CS_, (c + 1) * L - 1:(c + 1) * L, sl]
            p_inv = jnp.exp(-cs_p)
            p_end = jnp.exp(cs_last - cs_p)
            kk_p = pre_ref[KK_, rs, sl]
            kk_n = kk_p * lax.rsqrt(jnp.maximum(_seg_sum(kk_p * kk_p, lane_lo), 1e-24))
            b_p = kk_n * pre_ref[ALR_, rs, sl]
            k_p = pre_ref[K_, rs, sl]
            la = _stack_heads(-kk_n * jnp.exp(cs_p - pre_ref[LW_, rs, sl]), lane_lo).astype(BF16)
            lr = _stack_heads(p_ref[rs, sl] * jnp.exp(cs_p), lane_lo).astype(BF16)
            bt = (b_p * p_inv).astype(BF16)
            kt = (k_p * p_inv).astype(BF16)
            g = _mm(jnp.concatenate([la, lr], axis=0), jnp.concatenate([bt, bt, kt, kt], axis=0), NT)
            a_ab.append(jnp.where(strict, g[0:H2, 0:H2], 0.0))
            a_ak.append(jnp.where(strict, g[0:H2, H2:], 0.0).astype(BF16))
            a_rk.append(jnp.where(incl, g[H2:, H2:], 0.0).astype(BF16))
            lr_arb.append(jnp.concatenate([lr, jnp.where(incl, g[H2:, 0:H2], 0.0).astype(BF16)], axis=1))
            lhs_a.append(la)
            hat_bk.append(jnp.concatenate([_stack_heads(b_p * p_end, lane_lo), _stack_heads(k_p * p_end, lane_lo)],
                                          axis=0).astype(BF16))
            p_all.append(jnp.exp(cs_last))
            v_t.append(_stack_heads(p_ref[rs, v_cols(sl)], lane_lo).T.astype(BF16))
        t_inv = [t.astype(BF16) for t in _unit_lower_inverse_many(a_ab, row, col, eye)]
        t_a = [_mm(t, la).astype(BF16) for t, la in zip(t_inv, lhs_a)]
        av_t = [_mm(vt, a, NT).astype(BF16) for vt, a in zip(v_t, a_ak)]
        tav_t = [_mm(x, t, NT) for x, t in zip(av_t, t_inv)]
        arkv_t = [_mm(vt, a, NT) for vt, a in zip(v_t, a_rk)]
        return dict(t_a=t_a, tav_t=tav_t, arkv_t=arkv_t, lr_arb=lr_arb, hat_bk=hat_bk, v_t=v_t, p_all=p_all)

    def state_part(chunks, d):
        for ci, c in enumerate(chunks):
            rs = slice(c * L, (c + 1) * L)
            idx = [ci * n_pairs + q for q in range(n_pairs)]
            s0 = [st_ref[q] for q in range(n_pairs)]
            s0b = [s.astype(BF16) for s in s0]
            u_tb = [(_mm(s0b[q], d["t_a"][i], NT) + d["tav_t"][i]).astype(BF16) for q, i in enumerate(idx)]
            for q, i in enumerate(idx):
                st_ref[q] = s0[q] * d["p_all"][i] + _mm(jnp.concatenate([u_tb[q], d["v_t"][i]], axis=1),
                                                        d["hat_bk"][i])
            y_t = [_mm(jnp.concatenate([s0b[q], u_tb[q]], axis=1), d["lr_arb"][i], NT) + d["arkv_t"][i]
                   for q, i in enumerate(idx)]
            for q in range(n_pairs):
                sl = slice(q * PAIR, (q + 1) * PAIR)
                y_st = y_t[q].T
                y = y_st[0:L, :] + y_st[L:H2, :]
                mean = _seg_sum(y, lane_lo) * (1.0 / RWKV_HEAD_DIM)
                dev = y - mean
                var = _seg_sum(dev * dev, lane_lo) * (1.0 / RWKV_HEAD_DIM)
                yn = dev * lax.rsqrt(var + RWKV_GN_EPS) * lnw_ref[:, sl] + lnb_ref[:, sl]
                v_p = p_ref[rs, v_cols(sl)]
                bonus = _seg_sum(p_ref[rs, sl] * pre_ref[K_, rs, sl] * rk_ref[:, sl], lane_lo) * v_p
                o_ref[rs, sl] = ((yn + bonus) * pre_ref[G_, rs, sl]).astype(o_ref.dtype)

    groups = [list(range(c0, c0 + RWKV_GROUP)) for c0 in range(0, TB // L, RWKV_GROUP)]
    return groups, independent_part, state_part


def _mixers_kernel(ps_ref, pr_ref, dtb_ref, alog_ref, dsk_ref, ng_ref,
                   w0_ref, w2_ref, a0_ref, a2_ref, g2_ref, kk_ref, ka_ref, rk_ref, lnw_ref, lnb_ref,
                   os_ref, or_ref, sst_ref, rst_ref, pre_ref):
    ssd_chunk = _ssd_parts(ps_ref, dtb_ref, alog_ref, dsk_ref, ng_ref, os_ref, sst_ref)
    groups, independent_part, state_part = _rwkv_parts(
        pr_ref, w0_ref, w2_ref, a0_ref, a2_ref, g2_ref, kk_ref, ka_ref, rk_ref, lnw_ref, lnb_ref,
        or_ref, rst_ref, pre_ref)
    ssd_chunks = iter(range(SSD_BLOCK // SSD_CHUNK))
    per_group = (SSD_BLOCK // SSD_CHUNK) // len(groups)
    ready = []
    for g in groups:
        ready.append(independent_part(g))
        for _ in range(per_group):
            ssd_chunk(next(ssd_chunks))
    for c in ssd_chunks:
        ssd_chunk(c)
    for g, d in zip(groups, ready):
        state_part(g, d)


def _mixers(p_ssd, p_rwkv, dtb, alog, dsk, ng, w0, w2p, a0, a2p, g2, k_k, k_a, r_k, ln_w, ln_b, batch, seq):
    assert SSD_BLOCK == RWKV_BLOCK
    TB = RWKV_BLOCK
    nb = seq // TB
    const = lambda b, c: (0, 0)
    rows = lambda b, c: (b * nb + c, 0)
    vec = pl.BlockSpec((1, RWKV_D), const)
    return pl.pallas_call(
        _mixers_kernel,
        name="mixers",
        grid=(batch, nb),
        in_specs=[
            pl.BlockSpec((TB, SSD_PCOLS), rows),
            pl.BlockSpec((TB, RWKV_COLS), rows),
            pl.BlockSpec((1, LANES), const),
            pl.BlockSpec((1, LANES), const),
            pl.BlockSpec((1, SSD_D), const),
            pl.BlockSpec((1, SSD_D), const),
            vec,
            pl.BlockSpec((LANES, RWKV_D), const),
            vec,
            pl.BlockSpec((LANES, RWKV_D), const),
            pl.BlockSpec((GATE_LORA, RWKV_D), const),
            vec, vec, vec, vec, vec,
        ],
        out_specs=[pl.BlockSpec((TB, SSD_D), rows), pl.BlockSpec((TB, RWKV_D), rows)],
        out_shape=[jax.ShapeDtypeStruct((batch * seq, SSD_D), BF16),
                   jax.ShapeDtypeStruct((batch * seq, RWKV_D), BF16)],
        scratch_shapes=[
            pltpu.VMEM((SSD_GROUPS, SSD_STATE, SSD_D // SSD_GROUPS), F32),
            pltpu.VMEM((RWKV_D // PAIR, PAIR, PAIR), F32),
            pltpu.VMEM((6, TB, RWKV_D), F32),
        ],
        compiler_params=pltpu.CompilerParams(
            dimension_semantics=("arbitrary", "arbitrary"), vmem_limit_bytes=VMEM_LIMIT),
    )(p_ssd, p_rwkv, dtb, alog, dsk, ng, w0, w2p, a0, a2p, g2, k_k, k_a, r_k, ln_w, ln_b)


def _ffn_kernel(ys_ref, yr_ref, x_ref, wo_ref, g1_ref, g2_ref, wup_ref, cw_ref, cb_ref, wdn_ref, g3_ref,
                o_ref, ubuf_ref, *, tm):
    @pl.when(pl.program_id(1) == 0)
    def _():
        ubuf_ref[...] = jnp.zeros(ubuf_ref.shape, F32)

    mix = _mm(ys_ref[...], wo_ref[0:SSD_D, :]) + _mm(yr_ref[...], wo_ref[SSD_D:, :])
    h = x_ref[...] + _rms(mix, g1_ref[...])
    hn = _rms(h, g2_ref[...]).astype(BF16)
    sub = _iota2((SUBLANES, FFN_COLS), 0)

    def up(j):
        return [_mm(hn, wup_ref[:, c0:c0 + FFN_COLS]) for c0 in (j * FFN_COLS, D_FF + j * FFN_COLS)]

    def conv(u, c0):
        cs = slice(c0, c0 + FFN_COLS)
        tail = ubuf_ref[:, cs]
        out = cb_ref[:, cs] + cw_ref[FFN_CONV - 1:FFN_CONV, cs] * u
        for k in range(1, FFN_CONV):
            r = pltpu.roll(u, k, axis=0)
            head = jnp.where(sub < k, pltpu.roll(tail, k, axis=0), r[0:SUBLANES, :])
            shifted = jnp.concatenate([head, r[SUBLANES:, :]], axis=0)
            out = out + cw_ref[FFN_CONV - 1 - k:FFN_CONV - k, cs] * shifted
        ubuf_ref[:, cs] = u[tm - SUBLANES:tm, :]
        return out

    n_steps = D_FF // FFN_COLS
    u_next = up(0)
    acts = []
    for j in range(n_steps):
        u_gate, u_val = u_next
        if j + 1 < n_steps:
            u_next = up(j + 1)
        gate = conv(u_gate, j * FFN_COLS)
        val = conv(u_val, D_FF + j * FFN_COLS)
        acts.append((_silu(gate) * val).astype(BF16))
    f = _mm(jnp.concatenate(acts, axis=1), wdn_ref[...])
    o_ref[...] = h + _rms(f, g3_ref[...])


def _ffn(ys, yr, x2, wo, g1, g2, wup, cw, cb, wdn, g3, batch, seq, tm):
    nb = seq // tm
    const = lambda b, i: (0, 0)
    rows = lambda b, i: (b * nb + i, 0)
    res = functools.partial(pl.BlockSpec, index_map=const, pipeline_mode=pl.Buffered(1))
    return pl.pallas_call(
        functools.partial(_ffn_kernel, tm=tm),
        name="outproj_ffn",
        grid=(batch, nb),
        in_specs=[
            pl.BlockSpec((tm, SSD_D), rows),
            pl.BlockSpec((tm, RWKV_D), rows),
            pl.BlockSpec((tm, D_MODEL), rows),
            res((SSD_D + RWKV_D, D_MODEL)),
            pl.BlockSpec((1, D_MODEL), const),
            pl.BlockSpec((1, D_MODEL), const),
            res((D_MODEL, 2 * D_FF)),
            pl.BlockSpec((FFN_CONV, 2 * D_FF), const),
            pl.BlockSpec((1, 2 * D_FF), const),
            res((D_FF, D_MODEL)),
            pl.BlockSpec((1, D_MODEL), const),
        ],
        out_specs=pl.BlockSpec((tm, D_MODEL), rows),
        out_shape=jax.ShapeDtypeStruct((batch * seq, D_MODEL), F32),
        scratch_shapes=[
            pltpu.VMEM((SUBLANES, 2 * D_FF), F32),
        ],
        compiler_params=pltpu.CompilerParams(
            dimension_semantics=("arbitrary", "arbitrary"), vmem_limit_bytes=VMEM_LIMIT),
    )(ys, yr, x2, wo, g1, g2, wup, cw, cb, wdn, g3)


def _pad_lanes(v):
    return jnp.pad(v.astype(F32), (0, LANES - v.shape[0]))[None, :]


def _layer(h2, batch, seq, pre_mix_norm, w_in, ssd_conv_w, ssd_conv_b, ssd_dt_bias, ssd_a_log, ssd_d, ssd_norm,
           rwkv_mu, rwkv_w0, rwkv_w2, rwkv_a0, rwkv_a2, rwkv_g2, rwkv_k_k, rwkv_k_a, rwkv_r_k,
           rwkv_ln_w, rwkv_ln_b, w_out, post_mix_norm, pre_ffn_norm, ffn_w_up, ffn_conv_w,
           ffn_conv_b, ffn_w_down, post_ffn_norm):
    row = lambda v: v.astype(F32).reshape(1, -1)

    w_ssd = jnp.pad(w_in[:, :SSD_COLS], ((0, 0), (0, SSD_PCOLS - SSD_COLS))).astype(BF16)
    i1 = RWKV_D
    i2 = i1 + DECAY_LORA
    i3 = i2 + RWKV_D
    i4 = i3 + RWKV_D
    i5 = i4 + AAA_LORA
    perm = lambda t: jnp.concatenate(
        [t[..., 0:i1], t[..., i2:i3], t[..., i3:i4], t[..., i1:i2], t[..., i4:i5], t[..., i5:]], axis=-1)
    w_rwkv = perm(w_in[:, SSD_COLS:]).astype(BF16)
    mu = perm(rwkv_mu).astype(F32).reshape(1, -1)
    w2p = jnp.concatenate([rwkv_w2, jnp.zeros((AAA_LORA, RWKV_D), rwkv_w2.dtype)], axis=0).astype(BF16)
    a2p = jnp.concatenate([jnp.zeros((DECAY_LORA, RWKV_D), rwkv_a2.dtype), rwkv_a2], axis=0).astype(BF16)

    p_ssd, p_rwkv = _inproj(h2, row(pre_mix_norm), w_ssd, w_rwkv, ssd_conv_w.astype(F32), row(ssd_conv_b), mu,
                            seq, tm=512)
    y_ssd, y_rwkv = _mixers(
        p_ssd, p_rwkv, _pad_lanes(ssd_dt_bias), _pad_lanes(ssd_a_log), row(jnp.repeat(ssd_d, SSD_HEAD_DIM)),
        row(ssd_norm), row(rwkv_w0), w2p, row(rwkv_a0), a2p, rwkv_g2.astype(BF16), row(rwkv_k_k),
        row(rwkv_k_a), row(rwkv_r_k), row(rwkv_ln_w), row(rwkv_ln_b), batch, seq)
    return _ffn(y_ssd, y_rwkv, h2, w_out.astype(BF16), row(post_mix_norm), row(pre_ffn_norm),
                ffn_w_up.astype(BF16), ffn_conv_w.astype(F32), row(ffn_conv_b), ffn_w_down.astype(BF16),
                row(post_ffn_norm), batch, seq, tm=512)


def kernel(x, pre_mix_norm, w_in, ssd_conv_w, ssd_conv_b, ssd_dt_bias, ssd_a_log, ssd_d, ssd_norm, rwkv_mu, rwkv_w0, rwkv_w2, rwkv_a0, rwkv_a2, rwkv_g2, rwkv_k_k, rwkv_k_a, rwkv_r_k, rwkv_ln_w, rwkv_ln_b, w_out, post_mix_norm, pre_ffn_norm, ffn_w_up, ffn_conv_w, ffn_conv_b, ffn_w_down, post_ffn_norm):
    batch, seq, d = x.shape
    h2 = x.reshape(batch * seq, d)
    params = (pre_mix_norm, w_in, ssd_conv_w, ssd_conv_b, ssd_dt_bias, ssd_a_log, ssd_d, ssd_norm, rwkv_mu,
              rwkv_w0, rwkv_w2, rwkv_a0, rwkv_a2, rwkv_g2, rwkv_k_k, rwkv_k_a, rwkv_r_k, rwkv_ln_w, rwkv_ln_b,
              w_out, post_mix_norm, pre_ffn_norm, ffn_w_up, ffn_conv_w, ffn_conv_b, ffn_w_down, post_ffn_norm)
    for l in range(pre_mix_norm.shape[0]):
        h2 = _layer(h2, batch, seq, *(t[l] for t in params))
    return h2.reshape(batch, seq, d)
```

```python
import functools

import jax
import jax.numpy as jnp
from jax import lax
from jax.experimental import pallas as pl
from jax.experimental.pallas import tpu as pltpu

F32 = jnp.float32
BF16 = jnp.bfloat16

D_MODEL = 1024
SSD_HEADS = 8
SSD_HEAD_DIM = 64
SSD_D = SSD_HEADS * SSD_HEAD_DIM
SSD_GROUPS = 2
SSD_STATE = 128
SSD_CONV = 4
SSD_CHUNK = 128
SSD_CONV_DIM = SSD_D + 2 * SSD_GROUPS * SSD_STATE
SSD_COLS = SSD_D + SSD_CONV_DIM + SSD_HEADS
RWKV_HEADS = 8
RWKV_HEAD_DIM = 64
RWKV_D = RWKV_HEADS * RWKV_HEAD_DIM
DECAY_LORA = 64
AAA_LORA = 64
GATE_LORA = 128
RWKV_COLS = 3 * RWKV_D + DECAY_LORA + AAA_LORA + GATE_LORA
RWKV_GN_EPS = 64e-5
D_FF = 2816
FFN_CONV = 3
NORM_EPS = 1e-6
LOG2_E = 1.4426950408889634
EXP_M_HALF = 0.6065306597126334

LANES = 128
SUBLANES = 8
SSD_PCOLS = SSD_D + SSD_CONV_DIM + LANES
SSD_BLOCK = 512
RWKV_CHUNK = 64
RWKV_BLOCK = 512
INPROJ_COLS = 256
RWKV_GROUP = 4
PAIR = 2 * RWKV_HEAD_DIM
FFN_COLS = 512
INPROJ_ROWS = 1024
FFN_ROWS = 512
VMEM_LIMIT = 56 * 1024 * 1024

NN = (((1,), (0,)), ((), ()))
NT = (((1,), (1,)), ((), ()))
TN = (((0,), (0,)), ((), ()))


def _mm(a, b, dims=NN):
    return lax.dot_general(a, b, dims, preferred_element_type=F32)


def _dot1(a, b, dims=NN):
    return _mm(a.astype(BF16), b.astype(BF16), dims)


def _split3(a):
    a1 = a.astype(BF16)
    r1 = a - a1.astype(F32)
    a2 = r1.astype(BF16)
    a3 = (r1 - a2.astype(F32)).astype(BF16)
    return a1, a2, a3


def _dot_sel(sel_bf16, a):
    a1, a2, a3 = _split3(a)
    return _mm(sel_bf16, a1) + (_mm(sel_bf16, a2) + _mm(sel_bf16, a3))


def _a_dot_sel(a, sel_bf16):
    a1, a2, a3 = _split3(a)
    return _mm(a1, sel_bf16) + (_mm(a2, sel_bf16) + _mm(a3, sel_bf16))


def _dot_sel2(sel_bf16, a):
    hi = a.astype(BF16)
    lo = (a - hi.astype(F32)).astype(BF16)
    return _mm(sel_bf16, hi) + _mm(sel_bf16, lo)


def _shift_rows(x, tail, k):
    sub = _iota2((SUBLANES, x.shape[1]), 0)
    r = pltpu.roll(x, k, axis=0)
    head = jnp.where(sub < k, pltpu.roll(tail, k, axis=0), r[0:SUBLANES, :])
    return jnp.concatenate([head, r[SUBLANES:, :]], axis=0)


def _rms(x, g):
    return x * lax.rsqrt(jnp.mean(x * x, axis=-1, keepdims=True) + NORM_EPS) * g


def _sigmoid(x):
    return 0.5 + 0.5 * jnp.tanh(0.5 * x)


def _silu(x):
    h = 0.5 * x
    return h + h * jnp.tanh(h)


def _softplus(x):
    return jnp.maximum(x, 0.0) + jnp.log(1.0 + jnp.exp(-jnp.abs(x)))


def _iota2(shape, axis):
    return lax.broadcasted_iota(jnp.int32, shape, axis)


def _inproj_kernel(x_ref, g_ref, ws_ref, wr_ref, cw_ref, cb_ref, mu_ref, ps_ref, pr_ref, tail_s_ref, tail_r_ref,
                   *, tm, blocks_per_seq):
    i = pl.program_id(0)

    @pl.when(i == 0)
    def _():
        tail_s_ref[...] = jnp.zeros(tail_s_ref.shape, F32)
        tail_r_ref[...] = jnp.zeros(tail_r_ref.shape, F32)

    first = i % blocks_per_seq == 0
    xb = _rms(x_ref[...], g_ref[...]).astype(BF16)
    xbc0 = SSD_D
    dt0 = SSD_D + SSD_CONV_DIM
    ps_ref[:, 0:xbc0] = _mm(xb, ws_ref[:, 0:xbc0])
    ps_ref[:, dt0:SSD_PCOLS] = _mm(xb, ws_ref[:, dt0:SSD_PCOLS])
    for c0 in range(0, SSD_CONV_DIM, INPROJ_COLS):
        ts = slice(c0, c0 + INPROJ_COLS)
        cs = slice(xbc0 + c0, xbc0 + c0 + INPROJ_COLS)
        u = _mm(xb, ws_ref[:, cs])
        tail = jnp.where(first, 0.0, tail_s_ref[:, ts])
        acc = cb_ref[:, ts] + cw_ref[SSD_CONV - 1:SSD_CONV, ts] * u
        for k in range(1, SSD_CONV):
            acc = acc + cw_ref[SSD_CONV - 1 - k:SSD_CONV - k, ts] * _shift_rows(u, tail, k)
        tail_s_ref[:, ts] = u[tm - SUBLANES:tm, :]
        ps_ref[:, cs] = _silu(acc)
    for c0 in range(0, RWKV_COLS, INPROJ_COLS):
        cs = slice(c0, c0 + INPROJ_COLS)
        p = _mm(xb, wr_ref[:, cs])
        tail = jnp.where(first, 0.0, tail_r_ref[:, cs])
        prev = _shift_rows(p, tail, 1)
        tail_r_ref[:, cs] = p[tm - SUBLANES:tm, :]
        pr_ref[:, cs] = p + (prev - p) * mu_ref[:, cs]


def _inproj(x2, g, w_ssd, w_rwkv, cw, cb, mu, seq, tm):
    n = x2.shape[0]
    const = lambda i: (0, 0)
    return pl.pallas_call(
        functools.partial(_inproj_kernel, tm=tm, blocks_per_seq=seq // tm),
        name="inproj",
        grid=(n // tm,),
        in_specs=[
            pl.BlockSpec((tm, D_MODEL), lambda i: (i, 0)),
            pl.BlockSpec((1, D_MODEL), const),
            pl.BlockSpec((D_MODEL, SSD_PCOLS), const, pipeline_mode=pl.Buffered(1)),
            pl.BlockSpec((D_MODEL, RWKV_COLS), const, pipeline_mode=pl.Buffered(1)),
            pl.BlockSpec((SSD_CONV, SSD_CONV_DIM), const),
            pl.BlockSpec((1, SSD_CONV_DIM), const),
            pl.BlockSpec((1, RWKV_COLS), const),
        ],
        out_specs=[
            pl.BlockSpec((tm, SSD_PCOLS), lambda i: (i, 0)),
            pl.BlockSpec((tm, RWKV_COLS), lambda i: (i, 0)),
        ],
        out_shape=[
            jax.ShapeDtypeStruct((n, SSD_PCOLS), F32),
            jax.ShapeDtypeStruct((n, RWKV_COLS), F32),
        ],
        scratch_shapes=[
            pltpu.VMEM((SUBLANES, SSD_CONV_DIM), F32),
            pltpu.VMEM((SUBLANES, RWKV_COLS), F32),
        ],
        compiler_params=pltpu.CompilerParams(
            dimension_semantics=("arbitrary",), vmem_limit_bytes=VMEM_LIMIT),
    )(x2, g, w_ssd, w_rwkv, cw, cb, mu)


def _ssd_parts(p_ref, dtb_ref, alog_ref, dsk_ref, ng_ref, o_ref, st_ref):
    L = SSD_CHUNK
    gw = SSD_D // SSD_GROUPS

    @pl.when(pl.program_id(1) == 0)
    def _():
        st_ref[...] = jnp.zeros(st_ref.shape, F32)

    row = _iota2((L, L), 0)
    col = _iota2((L, L), 1)
    causal = row >= col
    tril = jnp.where(causal, 1.0, 0.0).astype(BF16)
    hsel = (_iota2((LANES, SSD_D), 1) // SSD_HEAD_DIM == _iota2((LANES, SSD_D), 0))
    hsel = jnp.where(hsel, 1.0, 0.0).astype(BF16)
    lane_lo = _iota2((L, LANES), 1) < SSD_HEAD_DIM
    xbc_cols = slice(SSD_D, SSD_D + SSD_CONV_DIM)

    def chunk(c):
        rs = slice(c * L, (c + 1) * L)
        z = p_ref[rs, 0:SSD_D]
        xbc = p_ref[rs, xbc_cols]
        dt_raw = p_ref[rs, SSD_D + SSD_CONV_DIM:SSD_PCOLS]
        xs = xbc[:, 0:SSD_D]

        dt = _softplus(dt_raw + dtb_ref[...])
        a = dt * (-LOG2_E * jnp.exp(alog_ref[...]))
        a_cs = _dot_sel(tril, a)
        a_cs_t = a_cs.T
        dt_e = _mm(dt.astype(BF16), hsel)
        acs_e = _a_dot_sel(a_cs, hsel)
        ea_e = jnp.exp2(acs_e)
        ds_e = jnp.exp2(acs_e[L - 1:L, :] - acs_e)

        x_dt = xs * dt_e
        x_b = x_dt.astype(BF16)
        x_dec = (x_dt * ds_e).astype(BF16)

        for g in range(SSD_GROUPS):
            gs = slice(g * gw, (g + 1) * gw)
            b_g = xbc[:, SSD_D + g * SSD_STATE:SSD_D + (g + 1) * SSD_STATE].astype(BF16)
            c_off = SSD_D + SSD_GROUPS * SSD_STATE
            c_g = xbc[:, c_off + g * SSD_STATE:c_off + (g + 1) * SSD_STATE].astype(BF16)
            scores = _mm(c_g, b_g, NT)
            state = st_ref[g]
            y_off = _mm(c_g, state.astype(BF16)) * ea_e[:, gs]
            st_ref[g] = state * ea_e[L - 1:L, gs] + _mm(b_g, x_dec[:, gs], TN)
            y_parts = []
            for j in range(gw // LANES):
                h0 = g * (SSD_HEADS // SSD_GROUPS) + 2 * j
                ms = []
                for h in (h0, h0 + 1):
                    seg = a_cs[:, h:h + 1] - a_cs_t[h:h + 1, :]
                    dec = jnp.exp2(jnp.where(causal, seg, -jnp.inf))
                    ms.append((scores * dec).astype(BF16))
                xp = x_b[:, h0 * SSD_HEAD_DIM:h0 * SSD_HEAD_DIM + LANES]
                zero = jnp.zeros_like(xp)
                x_bd = jnp.concatenate([jnp.where(lane_lo, xp, zero), jnp.where(lane_lo, zero, xp)], axis=0)
                y_parts.append(_mm(jnp.concatenate(ms, axis=1), x_bd))
            y = jnp.concatenate(y_parts, axis=1) + y_off
            y = y + dsk_ref[:, gs] * xs[:, gs]
            y = y * _silu(z[:, gs])
            y = y * lax.rsqrt(jnp.mean(y * y, axis=-1, keepdims=True) + NORM_EPS)
            o_ref[rs, gs] = (y * ng_ref[:, gs]).astype(o_ref.dtype)
    return chunk


def _seg_sum(x, lane_lo):
    s_lo = jnp.sum(jnp.where(lane_lo, x, 0.0), axis=-1, keepdims=True)
    s_hi = jnp.sum(jnp.where(lane_lo, 0.0, x), axis=-1, keepdims=True)
    return jnp.where(lane_lo, s_lo, s_hi)


def _stack_heads(x, lane_lo):
    zero = jnp.zeros_like(x)
    return jnp.concatenate([jnp.where(lane_lo, x, zero), jnp.where(lane_lo, zero, x)], axis=0)


def _unit_lower_inverse_many(a_list, row, col, eye):
    blk8 = row // 8 == col // 8
    a8f = [jnp.where(blk8, a, 0.0) for a in a_list]
    a8 = [x.astype(BF16) for x in a8f]
    t = [eye + x for x in a8f]
    a2 = [_mm(x, x).astype(BF16) for x in a8]
    t = [ti + _mm(ti.astype(BF16), x) for ti, x in zip(t, a2)]
    a4 = [_mm(x, x).astype(BF16) for x in a2]
    t = [ti + _mm(ti.astype(BF16), x) for ti, x in zip(t, a4)]
    n = a_list[0].shape[0]
    for s in (8, 16, 32):
        lower_left = (row // (2 * s) == col // (2 * s)) & ((row // s) % 2 == 1) & ((col // s) % 2 == 0)
        second = [slice(r0 + s, r0 + 2 * s) for r0 in range(0, n, 2 * s)]
        tb = [ti.astype(BF16) for ti in t]
        t2 = [jnp.concatenate([ti[r, :] for r in second], axis=0) for ti in t]
        x = [_mm(t2i.astype(BF16), jnp.where(lower_left, a, 0.0).astype(BF16)).astype(BF16)
             for t2i, a in zip(t2, a_list)]
        t2 = [t2i + _mm(xi, tbi) for t2i, xi, tbi in zip(t2, x, tb)]
        t = [jnp.concatenate([piece for m, r in enumerate(second)
                              for piece in (ti[r.start - s:r.start, :], t2i[m * s:(m + 1) * s, :])], axis=0)
             for ti, t2i in zip(t, t2)]
    return t


def _rwkv_parts(p_ref, w0_ref, w2_ref, a0_ref, a2_ref, g2_ref, kk_ref, ka_ref, rk_ref,
                lnw_ref, lnb_ref, o_ref, st_ref, pre_ref):
    L = RWKV_CHUNK
    TB = RWKV_BLOCK
    H2 = 2 * L
    D = RWKV_D
    n_pairs = D // PAIR

    @pl.when(pl.program_id(1) == 0)
    def _():
        st_ref[...] = jnp.zeros(st_ref.shape, F32)

    k = p_ref[:, D:2 * D]
    wa = p_ref[:, 3 * D:3 * D + LANES]
    g_lo = p_ref[:, 3 * D + LANES:3 * D + 2 * LANES]
    lw = -EXP_M_HALF * _sigmoid(w0_ref[...] + _dot1(jnp.tanh(wa), w2_ref[...]))
    alr = _sigmoid(a0_ref[...] + _dot1(wa, a2_ref[...]))
    CB = 4 * L
    blk_tril = (_iota2((CB, CB), 0) >= _iota2((CB, CB), 1)) & (_iota2((CB, CB), 0) // L == _iota2((CB, CB), 1) // L)
    blk_tril = jnp.where(blk_tril, 1.0, 0.0).astype(BF16)
    K_, KK_, ALR_, LW_, CS_, G_ = range(6)
    v_cols = lambda sl: slice(2 * D + sl.start, 2 * D + sl.stop)
    pre_ref[K_] = k * (1.0 + (alr - 1.0) * ka_ref[...])
    pre_ref[KK_] = k * kk_ref[...]
    pre_ref[ALR_] = alr
    pre_ref[LW_] = lw
    for r0 in range(0, TB, CB):
        pre_ref[CS_, r0:r0 + CB, :] = _dot_sel2(blk_tril, lw[r0:r0 + CB, :])
    pre_ref[G_] = _dot1(_sigmoid(g_lo), g2_ref[...])

    lane_lo = _iota2((L, PAIR), 1) < RWKV_HEAD_DIM
    row = _iota2((H2, H2), 0)
    col = _iota2((H2, H2), 1)
    eye = jnp.where(row == col, 1.0, 0.0)
    same_head = row // L == col // L
    strict = same_head & (row % L > col % L)
    incl = same_head & (row % L >= col % L)

    def independent_part(chunks):
        lhs_a, v_t, a_ab, a_ak, a_rk, lr_arb, hat_bk, p_all = ([] for _ in range(8))
        for c, q in [(c, q) for c in chunks for q in range(n_pairs)]:
            rs = slice(c * L, (c + 1) * L)
            sl = slice(q * PAIR, (q + 1) * PAIR)
            cs_p = pre_ref[CS_, rs, sl]
            cs_last = pre_ref[CS_, (c + 1) * L - 1:(c + 1) * L, sl]
            p_inv = jnp.exp(-cs_p)
            p_end = jnp.exp(cs_last - cs_p)
            kk_p = pre_ref[KK_, rs, sl]
            kk_n = kk_p * lax.rsqrt(jnp.maximum(_seg_sum(kk_p * kk_p, lane_lo), 1e-24))
            b_p = kk_n * pre_ref[ALR_, rs, sl]
            k_p = pre_ref[K_, rs, sl]
            la = _stack_heads(-kk_n * jnp.exp(cs_p - pre_ref[LW_, rs, sl]), lane_lo).astype(BF16)
            lr = _stack_heads(p_ref[rs, sl] * jnp.exp(cs_p), lane_lo).astype(BF16)
            bt = (b_p * p_inv).astype(BF16)
            kt = (k_p * p_inv).astype(BF16)
            g = _mm(jnp.concatenate([la, lr], axis=0), jnp.concatenate([bt, bt, kt, kt], axis=0), NT)
            a_ab.append(jnp.where(strict, g[0:H2, 0:H2], 0.0))
            a_ak.append(jnp.where(strict, g[0:H2, H2:], 0.0).astype(BF16))
            a_rk.append(jnp.where(incl, g[H2:, H2:], 0.0).astype(BF16))
            lr_arb.append(jnp.concatenate([lr, jnp.where(incl, g[H2:, 0:H2], 0.0).astype(BF16)], axis=1))
            lhs_a.append(la)
            hat_bk.append(jnp.concatenate([_stack_heads(b_p * p_end, lane_lo), _stack_heads(k_p * p_end, lane_lo)],
                                          axis=0).astype(BF16))
            p_all.append(jnp.exp(cs_last))
            v_t.append(_stack_heads(p_ref[rs, v_cols(sl)], lane_lo).T.astype(BF16))
        t_inv = [t.astype(BF16) for t in _unit_lower_inverse_many(a_ab, row, col, eye)]
        t_a = [_mm(t, la).astype(BF16) for t, la in zip(t_inv, lhs_a)]
        av_t = [_mm(vt, a, NT).astype(BF16) for vt, a in zip(v_t, a_ak)]
        tav_t = [_mm(x, t, NT) for x, t in zip(av_t, t_inv)]
        arkv_t = [_mm(vt, a, NT) for vt, a in zip(v_t, a_rk)]
        return dict(t_a=t_a, tav_t=tav_t, arkv_t=arkv_t, lr_arb=lr_arb, hat_bk=hat_bk, v_t=v_t, p_all=p_all)

    def state_part(chunks, d):
        for ci, c in enumerate(chunks):
            rs = slice(c * L, (c + 1) * L)
            idx = [ci * n_pairs + q for q in range(n_pairs)]
            s0 = [st_ref[q] for q in range(n_pairs)]
            s0b = [s.astype(BF16) for s in s0]
            u_tb = [(_mm(s0b[q], d["t_a"][i], NT) + d["tav_t"][i]).astype(BF16) for q, i in enumerate(idx)]
            for q, i in enumerate(idx):
                st_ref[q] = s0[q] * d["p_all"][i] + _mm(jnp.concatenate([u_tb[q], d["v_t"][i]], axis=1),
                                                        d["hat_bk"][i])
            y_t = [_mm(jnp.concatenate([s0b[q], u_tb[q]], axis=1), d["lr_arb"][i], NT) + d["arkv_t"][i]
                   for q, i in enumerate(idx)]
            for q in range(n_pairs):
                sl = slice(q * PAIR, (q + 1) * PAIR)
                y_st = y_t[q].T
                y = y_st[0:L, :] + y_st[L:H2, :]
                mean = _seg_sum(y, lane_lo) * (1.0 / RWKV_HEAD_DIM)
                dev = y - mean
                var = _seg_sum(dev * dev, lane_lo) * (1.0 / RWKV_HEAD_DIM)
                yn = dev * lax.rsqrt(var + RWKV_GN_EPS) * lnw_ref[:, sl] + lnb_ref[:, sl]
                v_p = p_ref[rs, v_cols(sl)]
                bonus = _seg_sum(p_ref[rs, sl] * pre_ref[K_, rs, sl] * rk_ref[:, sl], lane_lo) * v_p
                o_ref[rs, sl] = ((yn + bonus) * pre_ref[G_, rs, sl]).astype(o_ref.dtype)

    groups = [list(range(c0, c0 + RWKV_GROUP)) for c0 in range(0, TB // L, RWKV_GROUP)]
    return groups, independent_part, state_part


def _mixers_kernel(ps_ref, pr_ref, dtb_ref, alog_ref, dsk_ref, ng_ref,
                   w0_ref, w2_ref, a0_ref, a2_ref, g2_ref, kk_ref, ka_ref, rk_ref, lnw_ref, lnb_ref,
                   os_ref, or_ref, sst_ref, rst_ref, pre_ref):
    ssd_chunk = _ssd_parts(ps_ref, dtb_ref, alog_ref, dsk_ref, ng_ref, os_ref, sst_ref)
    groups, independent_part, state_part = _rwkv_parts(
        pr_ref, w0_ref, w2_ref, a0_ref, a2_ref, g2_ref, kk_ref, ka_ref, rk_ref, lnw_ref, lnb_ref,
        or_ref, rst_ref, pre_ref)
    ssd_chunks = iter(range(SSD_BLOCK // SSD_CHUNK))
    per_group = (SSD_BLOCK // SSD_CHUNK) // len(groups)
    ready = []
    for g in groups:
        ready.append(independent_part(g))
        for _ in range(per_group):
            ssd_chunk(next(ssd_chunks))
    for c in ssd_chunks:
        ssd_chunk(c)
    for g, d in zip(groups, ready):
        state_part(g, d)


def _mixers(p_ssd, p_rwkv, dtb, alog, dsk, ng, w0, w2p, a0, a2p, g2, k_k, k_a, r_k, ln_w, ln_b, batch, seq):
    assert SSD_BLOCK == RWKV_BLOCK
    TB = RWKV_BLOCK
    nb = seq // TB
    const = lambda b, c: (0, 0)
    rows = lambda b, c: (b * nb + c, 0)
    vec = pl.BlockSpec((1, RWKV_D), const)
    return pl.pallas_call(
        _mixers_kernel,
        name="mixers",
        grid=(batch, nb),
        in_specs=[
            pl.BlockSpec((TB, SSD_PCOLS), rows),
            pl.BlockSpec((TB, RWKV_COLS), rows),
            pl.BlockSpec((1, LANES), const),
            pl.BlockSpec((1, LANES), const),
            pl.BlockSpec((1, SSD_D), const),
            pl.BlockSpec((1, SSD_D), const),
            vec,
            pl.BlockSpec((LANES, RWKV_D), const),
            vec,
            pl.BlockSpec((LANES, RWKV_D), const),
            pl.BlockSpec((GATE_LORA, RWKV_D), const),
            vec, vec, vec, vec, vec,
        ],
        out_specs=[pl.BlockSpec((TB, SSD_D), rows), pl.BlockSpec((TB, RWKV_D), rows)],
        out_shape=[jax.ShapeDtypeStruct((batch * seq, SSD_D), BF16),
                   jax.ShapeDtypeStruct((batch * seq, RWKV_D), BF16)],
        scratch_shapes=[
            pltpu.VMEM((SSD_GROUPS, SSD_STATE, SSD_D // SSD_GROUPS), F32),
            pltpu.VMEM((RWKV_D // PAIR, PAIR, PAIR), F32),
            pltpu.VMEM((6, TB, RWKV_D), F32),
        ],
        compiler_params=pltpu.CompilerParams(
            dimension_semantics=("arbitrary", "arbitrary"), vmem_limit_bytes=VMEM_LIMIT),
    )(p_ssd, p_rwkv, dtb, alog, dsk, ng, w0, w2p, a0, a2p, g2, k_k, k_a, r_k, ln_w, ln_b)


def _ffn_kernel(ys_ref, yr_ref, x_ref, wo_ref, g1_ref, g2_ref, wup_ref, cw_ref, cb_ref, wdn_ref, g3_ref,
                o_ref, ubuf_ref, *, tm):
    @pl.when(pl.program_id(1) == 0)
    def _():
        ubuf_ref[...] = jnp.zeros(ubuf_ref.shape, F32)

    mix = _mm(ys_ref[...], wo_ref[0:SSD_D, :]) + _mm(yr_ref[...], wo_ref[SSD_D:, :])
    h = x_ref[...] + _rms(mix, g1_ref[...])
    hn = _rms(h, g2_ref[...]).astype(BF16)

    chunks = [(c0, min(FFN_COLS, D_FF - c0)) for c0 in range(0, D_FF, FFN_COLS)]

    def up(j):
        c0, width = chunks[j]
        return [_mm(hn, wup_ref[:, c:c + width]) for c in (c0, D_FF + c0)]

    def conv(u, c0):
        cs = slice(c0, c0 + u.shape[1])
        tail = ubuf_ref[:, cs]
        out = cb_ref[:, cs] + cw_ref[FFN_CONV - 1:FFN_CONV, cs] * u
        for k in range(1, FFN_CONV):
            out = out + cw_ref[FFN_CONV - 1 - k:FFN_CONV - k, cs] * _shift_rows(u, tail, k)
        ubuf_ref[:, cs] = u[tm - SUBLANES:tm, :]
        return out

    n_steps = len(chunks)
    u_next = up(0)
    acts = []
    for j in range(n_steps):
        u_gate, u_val = u_next
        if j + 1 < n_steps:
            u_next = up(j + 1)
        gate = conv(u_gate, chunks[j][0])
        val = conv(u_val, D_FF + chunks[j][0])
        acts.append((_silu(gate) * val).astype(BF16))
    f = _mm(jnp.concatenate(acts, axis=1), wdn_ref[...])
    o_ref[...] = h + _rms(f, g3_ref[...])


def _ffn(ys, yr, x2, wo, g1, g2, wup, cw, cb, wdn, g3, batch, seq, tm):
    nb = seq // tm
    const = lambda b, i: (0, 0)
    rows = lambda b, i: (b * nb + i, 0)
    res = functools.partial(pl.BlockSpec, index_map=const, pipeline_mode=pl.Buffered(1))
    return pl.pallas_call(
        functools.partial(_ffn_kernel, tm=tm),
        name="outproj_ffn",
        grid=(batch, nb),
        in_specs=[
            pl.BlockSpec((tm, SSD_D), rows),
            pl.BlockSpec((tm, RWKV_D), rows),
            pl.BlockSpec((tm, D_MODEL), rows),
            res((SSD_D + RWKV_D, D_MODEL)),
            pl.BlockSpec((1, D_MODEL), const),
            pl.BlockSpec((1, D_MODEL), const),
            res((D_MODEL, 2 * D_FF)),
            pl.BlockSpec((FFN_CONV, 2 * D_FF), const),
            pl.BlockSpec((1, 2 * D_FF), const),
            res((D_FF, D_MODEL)),
            pl.BlockSpec((1, D_MODEL), const),
        ],
        out_specs=pl.BlockSpec((tm, D_MODEL), rows),
        out_shape=jax.ShapeDtypeStruct((batch * seq, D_MODEL), F32),
        scratch_shapes=[
            pltpu.VMEM((SUBLANES, 2 * D_FF), F32),
        ],
        compiler_params=pltpu.CompilerParams(
            dimension_semantics=("arbitrary", "arbitrary"), vmem_limit_bytes=VMEM_LIMIT),
    )(ys, yr, x2, wo, g1, g2, wup, cw, cb, wdn, g3)


def _pad_lanes(v):
    return jnp.pad(v.astype(F32), (0, LANES - v.shape[0]))[None, :]


def _layer(h2, batch, seq, pre_mix_norm, w_in, ssd_conv_w, ssd_conv_b, ssd_dt_bias, ssd_a_log, ssd_d, ssd_norm,
           rwkv_mu, rwkv_w0, rwkv_w2, rwkv_a0, rwkv_a2, rwkv_g2, rwkv_k_k, rwkv_k_a, rwkv_r_k,
           rwkv_ln_w, rwkv_ln_b, w_out, post_mix_norm, pre_ffn_norm, ffn_w_up, ffn_conv_w,
           ffn_conv_b, ffn_w_down, post_ffn_norm):
    row = lambda v: v.astype(F32).reshape(1, -1)

    w_in = w_in.astype(BF16)
    w_ssd = jnp.pad(w_in[:, :SSD_COLS], ((0, 0), (0, SSD_PCOLS - SSD_COLS)))
    i1 = RWKV_D
    i2 = i1 + DECAY_LORA
    i3 = i2 + RWKV_D
    i4 = i3 + RWKV_D
    i5 = i4 + AAA_LORA
    perm = lambda t: jnp.concatenate(
        [t[..., 0:i1], t[..., i2:i3], t[..., i3:i4], t[..., i1:i2], t[..., i4:i5], t[..., i5:]], axis=-1)
    w_rwkv = perm(w_in[:, SSD_COLS:])
    mu = perm(rwkv_mu).astype(F32).reshape(1, -1)
    w2p = jnp.concatenate([rwkv_w2, jnp.zeros((AAA_LORA, RWKV_D), rwkv_w2.dtype)], axis=0).astype(BF16)
    a2p = jnp.concatenate([jnp.zeros((DECAY_LORA, RWKV_D), rwkv_a2.dtype), rwkv_a2], axis=0).astype(BF16)

    p_ssd, p_rwkv = _inproj(h2, row(pre_mix_norm), w_ssd, w_rwkv, ssd_conv_w.astype(F32), row(ssd_conv_b), mu,
                            seq, tm=INPROJ_ROWS)
    y_ssd, y_rwkv = _mixers(
        p_ssd, p_rwkv, _pad_lanes(ssd_dt_bias), _pad_lanes(ssd_a_log), row(jnp.repeat(ssd_d, SSD_HEAD_DIM)),
        row(ssd_norm), row(rwkv_w0), w2p, row(rwkv_a0), a2p, rwkv_g2.astype(BF16), row(rwkv_k_k),
        row(rwkv_k_a), row(rwkv_r_k), row(rwkv_ln_w), row(rwkv_ln_b), batch, seq)
    return _ffn(y_ssd, y_rwkv, h2, w_out.astype(BF16), row(post_mix_norm), row(pre_ffn_norm),
                ffn_w_up.astype(BF16), ffn_conv_w.astype(F32), row(ffn_conv_b), ffn_w_down.astype(BF16),
                row(post_ffn_norm), batch, seq, tm=FFN_ROWS)


def kernel(x, pre_mix_norm, w_in, ssd_conv_w, ssd_conv_b, ssd_dt_bias, ssd_a_log, ssd_d, ssd_norm, rwkv_mu, rwkv_w0, rwkv_w2, rwkv_a0, rwkv_a2, rwkv_g2, rwkv_k_k, rwkv_k_a, rwkv_r_k, rwkv_ln_w, rwkv_ln_b, w_out, post_mix_norm, pre_ffn_norm, ffn_w_up, ffn_conv_w, ffn_conv_b, ffn_w_down, post_ffn_norm):
    batch, seq, d = x.shape
    h2 = x.reshape(batch * seq, d)
    params = (pre_mix_norm, w_in, ssd_conv_w, ssd_conv_b, ssd_dt_bias, ssd_a_log, ssd_d, ssd_norm, rwkv_mu,
              rwkv_w0, rwkv_w2, rwkv_a0, rwkv_a2, rwkv_g2, rwkv_k_k, rwkv_k_a, rwkv_r_k, rwkv_ln_w, rwkv_ln_b,
              w_out, post_mix_norm, pre_ffn_norm, ffn_w_up, ffn_conv_w, ffn_conv_b, ffn_w_down, post_ffn_norm)
    for l in range(pre_mix_norm.shape[0]):
        h2 = _layer(h2, batch, seq, *(t[l] for t in params))
    return h2.reshape(batch, seq, d)
```

```python
import functools

import jax
import jax.numpy as jnp
from jax import lax
from jax.experimental import pallas as pl
from jax.experimental.pallas import tpu as pltpu

F32 = jnp.float32
BF16 = jnp.bfloat16

D_MODEL = 1024
SSD_HEADS = 8
SSD_HEAD_DIM = 64
SSD_D = SSD_HEADS * SSD_HEAD_DIM
SSD_GROUPS = 2
SSD_STATE = 128
SSD_CONV = 4
SSD_CHUNK = 128
SSD_CONV_DIM = SSD_D + 2 * SSD_GROUPS * SSD_STATE
SSD_COLS = SSD_D + SSD_CONV_DIM + SSD_HEADS
RWKV_HEADS = 8
RWKV_HEAD_DIM = 64
RWKV_D = RWKV_HEADS * RWKV_HEAD_DIM
DECAY_LORA = 64
AAA_LORA = 64
GATE_LORA = 128
RWKV_COLS = 3 * RWKV_D + DECAY_LORA + AAA_LORA + GATE_LORA
RWKV_GN_EPS = 64e-5
D_FF = 2816
FFN_CONV = 3
NORM_EPS = 1e-6
LOG2_E = 1.4426950408889634
EXP_M_HALF = 0.6065306597126334

LANES = 128
SUBLANES = 8
SSD_PCOLS = SSD_D + SSD_CONV_DIM + LANES
SSD_BLOCK = 512
RWKV_CHUNK = 64
RWKV_BLOCK = 512
INPROJ_COLS = 256
RWKV_GROUP = 4
PAIR = 2 * RWKV_HEAD_DIM
FFN_COLS = 512
INPROJ_ROWS = 1024
FFN_ROWS = 1024
VMEM_LIMIT = 56 * 1024 * 1024

NN = (((1,), (0,)), ((), ()))
NT = (((1,), (1,)), ((), ()))
TN = (((0,), (0,)), ((), ()))


def _mm(a, b, dims=NN):
    return lax.dot_general(a, b, dims, preferred_element_type=F32)


def _dot1(a, b, dims=NN):
    return _mm(a.astype(BF16), b.astype(BF16), dims)


def _split3(a):
    a1 = a.astype(BF16)
    r1 = a - a1.astype(F32)
    a2 = r1.astype(BF16)
    a3 = (r1 - a2.astype(F32)).astype(BF16)
    return a1, a2, a3


def _dot_sel(sel_bf16, a):
    a1, a2, a3 = _split3(a)
    return _mm(sel_bf16, a1) + (_mm(sel_bf16, a2) + _mm(sel_bf16, a3))


def _a_dot_sel(a, sel_bf16):
    a1, a2, a3 = _split3(a)
    return _mm(a1, sel_bf16) + (_mm(a2, sel_bf16) + _mm(a3, sel_bf16))


def _dot_sel2(sel_bf16, a):
    hi = a.astype(BF16)
    lo = (a - hi.astype(F32)).astype(BF16)
    return _mm(sel_bf16, hi) + _mm(sel_bf16, lo)


def _shift_rows(x, tail, k):
    sub = _iota2((SUBLANES, x.shape[1]), 0)
    r = pltpu.roll(x, k, axis=0)
    head = jnp.where(sub < k, pltpu.roll(tail, k, axis=0), r[0:SUBLANES, :])
    return jnp.concatenate([head, r[SUBLANES:, :]], axis=0)


def _rms(x, g):
    return x * lax.rsqrt(jnp.mean(x * x, axis=-1, keepdims=True) + NORM_EPS) * g


def _sigmoid(x):
    return 0.5 + 0.5 * jnp.tanh(0.5 * x)


def _silu(x):
    h = 0.5 * x
    return h + h * jnp.tanh(h)


def _softplus(x):
    return jnp.maximum(x, 0.0) + jnp.log(1.0 + jnp.exp(-jnp.abs(x)))


def _iota2(shape, axis):
    return lax.broadcasted_iota(jnp.int32, shape, axis)


def _inproj_kernel(x_ref, g_ref, ws_ref, wr_ref, cw_ref, cb_ref, mu_ref, ps_ref, pr_ref, tail_s_ref, tail_r_ref,
                   *, tm, blocks_per_seq):
    i = pl.program_id(0)

    @pl.when(i == 0)
    def _():
        tail_s_ref[...] = jnp.zeros(tail_s_ref.shape, F32)
        tail_r_ref[...] = jnp.zeros(tail_r_ref.shape, F32)

    first = i % blocks_per_seq == 0
    xb = _rms(x_ref[...], g_ref[...]).astype(BF16)
    xbc0 = SSD_D
    dt0 = SSD_D + SSD_CONV_DIM
    ps_ref[:, 0:xbc0] = _mm(xb, ws_ref[:, 0:xbc0])
    ps_ref[:, dt0:SSD_PCOLS] = _mm(xb, ws_ref[:, dt0:SSD_PCOLS])
    for c0 in range(0, SSD_CONV_DIM, INPROJ_COLS):
        ts = slice(c0, c0 + INPROJ_COLS)
        cs = slice(xbc0 + c0, xbc0 + c0 + INPROJ_COLS)
        u = _mm(xb, ws_ref[:, cs])
        tail = jnp.where(first, 0.0, tail_s_ref[:, ts])
        acc = cb_ref[:, ts] + cw_ref[SSD_CONV - 1:SSD_CONV, ts] * u
        for k in range(1, SSD_CONV):
            acc = acc + cw_ref[SSD_CONV - 1 - k:SSD_CONV - k, ts] * _shift_rows(u, tail, k)
        tail_s_ref[:, ts] = u[tm - SUBLANES:tm, :]
        ps_ref[:, cs] = _silu(acc)
    for c0 in range(0, RWKV_COLS, INPROJ_COLS):
        cs = slice(c0, c0 + INPROJ_COLS)
        p = _mm(xb, wr_ref[:, cs])
        tail = jnp.where(first, 0.0, tail_r_ref[:, cs])
        prev = _shift_rows(p, tail, 1)
        tail_r_ref[:, cs] = p[tm - SUBLANES:tm, :]
        pr_ref[:, cs] = p + (prev - p) * mu_ref[:, cs]


def _inproj(x2, g, w_ssd, w_rwkv, cw, cb, mu, seq, tm):
    n = x2.shape[0]
    const = lambda i: (0, 0)
    return pl.pallas_call(
        functools.partial(_inproj_kernel, tm=tm, blocks_per_seq=seq // tm),
        name="inproj",
        grid=(n // tm,),
        in_specs=[
            pl.BlockSpec((tm, D_MODEL), lambda i: (i, 0)),
            pl.BlockSpec((1, D_MODEL), const),
            pl.BlockSpec((D_MODEL, SSD_PCOLS), const, pipeline_mode=pl.Buffered(1)),
            pl.BlockSpec((D_MODEL, RWKV_COLS), const, pipeline_mode=pl.Buffered(1)),
            pl.BlockSpec((SSD_CONV, SSD_CONV_DIM), const),
            pl.BlockSpec((1, SSD_CONV_DIM), const),
            pl.BlockSpec((1, RWKV_COLS), const),
        ],
        out_specs=[
            pl.BlockSpec((tm, SSD_PCOLS), lambda i: (i, 0)),
            pl.BlockSpec((tm, RWKV_COLS), lambda i: (i, 0)),
        ],
        out_shape=[
            jax.ShapeDtypeStruct((n, SSD_PCOLS), F32),
            jax.ShapeDtypeStruct((n, RWKV_COLS), F32),
        ],
        scratch_shapes=[
            pltpu.VMEM((SUBLANES, SSD_CONV_DIM), F32),
            pltpu.VMEM((SUBLANES, RWKV_COLS), F32),
        ],
        compiler_params=pltpu.CompilerParams(
            dimension_semantics=("arbitrary",), vmem_limit_bytes=VMEM_LIMIT),
    )(x2, g, w_ssd, w_rwkv, cw, cb, mu)


def _ssd_parts(p_ref, dtb_ref, alog_ref, dsk_ref, ng_ref, o_ref, st_ref):
    L = SSD_CHUNK
    gw = SSD_D // SSD_GROUPS

    @pl.when(pl.program_id(1) == 0)
    def _():
        st_ref[...] = jnp.zeros(st_ref.shape, F32)

    row = _iota2((L, L), 0)
    col = _iota2((L, L), 1)
    causal = row >= col
    tril = jnp.where(causal, 1.0, 0.0).astype(BF16)
    hsel = (_iota2((LANES, SSD_D), 1) // SSD_HEAD_DIM == _iota2((LANES, SSD_D), 0))
    hsel = jnp.where(hsel, 1.0, 0.0).astype(BF16)
    lane_lo = _iota2((L, LANES), 1) < SSD_HEAD_DIM
    xbc_cols = slice(SSD_D, SSD_D + SSD_CONV_DIM)

    def chunk(c):
        rs = slice(c * L, (c + 1) * L)
        z = p_ref[rs, 0:SSD_D]
        xbc = p_ref[rs, xbc_cols]
        dt_raw = p_ref[rs, SSD_D + SSD_CONV_DIM:SSD_PCOLS]
        xs = xbc[:, 0:SSD_D]

        dt = _softplus(dt_raw + dtb_ref[...])
        a = dt * (-LOG2_E * jnp.exp(alog_ref[...]))
        a_cs = _dot_sel(tril, a)
        a_cs_t = a_cs.T
        dt_e = _mm(dt.astype(BF16), hsel)
        acs_e = _a_dot_sel(a_cs, hsel)
        ea_e = jnp.exp2(acs_e)
        ds_e = jnp.exp2(acs_e[L - 1:L, :] - acs_e)

        x_dt = xs * dt_e
        x_b = x_dt.astype(BF16)
        x_dec = (x_dt * ds_e).astype(BF16)

        for g in range(SSD_GROUPS):
            gs = slice(g * gw, (g + 1) * gw)
            b_g = xbc[:, SSD_D + g * SSD_STATE:SSD_D + (g + 1) * SSD_STATE].astype(BF16)
            c_off = SSD_D + SSD_GROUPS * SSD_STATE
            c_g = xbc[:, c_off + g * SSD_STATE:c_off + (g + 1) * SSD_STATE].astype(BF16)
            scores = _mm(c_g, b_g, NT)
            state = st_ref[g]
            y_off = _mm(c_g, state.astype(BF16)) * ea_e[:, gs]
            st_ref[g] = state * ea_e[L - 1:L, gs] + _mm(b_g, x_dec[:, gs], TN)
            y_parts = []
            for j in range(gw // LANES):
                h0 = g * (SSD_HEADS // SSD_GROUPS) + 2 * j
                ms = []
                for h in (h0, h0 + 1):
                    seg = a_cs[:, h:h + 1] - a_cs_t[h:h + 1, :]
                    dec = jnp.exp2(jnp.where(causal, seg, -jnp.inf))
                    ms.append((scores * dec).astype(BF16))
                xp = x_b[:, h0 * SSD_HEAD_DIM:h0 * SSD_HEAD_DIM + LANES]
                zero = jnp.zeros_like(xp)
                x_bd = jnp.concatenate([jnp.where(lane_lo, xp, zero), jnp.where(lane_lo, zero, xp)], axis=0)
                y_parts.append(_mm(jnp.concatenate(ms, axis=1), x_bd))
            y = jnp.concatenate(y_parts, axis=1) + y_off
            y = y + dsk_ref[:, gs] * xs[:, gs]
            y = y * _silu(z[:, gs])
            y = y * lax.rsqrt(jnp.mean(y * y, axis=-1, keepdims=True) + NORM_EPS)
            o_ref[rs, gs] = (y * ng_ref[:, gs]).astype(o_ref.dtype)
    return chunk


def _seg_sum(x, lane_lo):
    s_lo = jnp.sum(jnp.where(lane_lo, x, 0.0), axis=-1, keepdims=True)
    s_hi = jnp.sum(jnp.where(lane_lo, 0.0, x), axis=-1, keepdims=True)
    return jnp.where(lane_lo, s_lo, s_hi)


def _stack_heads(x, lane_lo):
    zero = jnp.zeros_like(x)
    return jnp.concatenate([jnp.where(lane_lo, x, zero), jnp.where(lane_lo, zero, x)], axis=0)


def _unit_lower_inverse_many(a_list, row, col, eye):
    blk8 = row // 8 == col // 8
    a8f = [jnp.where(blk8, a, 0.0) for a in a_list]
    a8 = [x.astype(BF16) for x in a8f]
    t = [eye + x for x in a8f]
    a2 = [_mm(x, x).astype(BF16) for x in a8]
    t = [ti + _mm(ti.astype(BF16), x) for ti, x in zip(t, a2)]
    a4 = [_mm(x, x).astype(BF16) for x in a2]
    t = [ti + _mm(ti.astype(BF16), x) for ti, x in zip(t, a4)]
    n = a_list[0].shape[0]
    for s in (8, 16, 32):
        lower_left = (row // (2 * s) == col // (2 * s)) & ((row // s) % 2 == 1) & ((col // s) % 2 == 0)
        second = [slice(r0 + s, r0 + 2 * s) for r0 in range(0, n, 2 * s)]
        tb = [ti.astype(BF16) for ti in t]
        t2 = [jnp.concatenate([ti[r, :] for r in second], axis=0) for ti in t]
        x = [_mm(t2i.astype(BF16), jnp.where(lower_left, a, 0.0).astype(BF16)).astype(BF16)
             for t2i, a in zip(t2, a_list)]
        t2 = [t2i + _mm(xi, tbi) for t2i, xi, tbi in zip(t2, x, tb)]
        t = [jnp.concatenate([piece for m, r in enumerate(second)
                              for piece in (ti[r.start - s:r.start, :], t2i[m * s:(m + 1) * s, :])], axis=0)
             for ti, t2i in zip(t, t2)]
    return t


def _rwkv_parts(p_ref, w0_ref, w2_ref, a0_ref, a2_ref, g2_ref, kk_ref, ka_ref, rk_ref,
                lnw_ref, lnb_ref, o_ref, st_ref, pre_ref):
    L = RWKV_CHUNK
    TB = RWKV_BLOCK
    H2 = 2 * L
    D = RWKV_D
    n_pairs = D // PAIR

    @pl.when(pl.program_id(1) == 0)
    def _():
        st_ref[...] = jnp.zeros(st_ref.shape, F32)

    k = p_ref[:, D:2 * D]
    wa = p_ref[:, 3 * D:3 * D + LANES]
    g_lo = p_ref[:, 3 * D + LANES:3 * D + 2 * LANES]
    lw = -EXP_M_HALF * _sigmoid(w0_ref[...] + _dot1(jnp.tanh(wa), w2_ref[...]))
    alr = _sigmoid(a0_ref[...] + _dot1(wa, a2_ref[...]))
    CB = 4 * L
    blk_tril = (_iota2((CB, CB), 0) >= _iota2((CB, CB), 1)) & (_iota2((CB, CB), 0) // L == _iota2((CB, CB), 1) // L)
    blk_tril = jnp.where(blk_tril, 1.0, 0.0).astype(BF16)
    K_, KK_, ALR_, LW_, CS_, G_ = range(6)
    v_cols = lambda sl: slice(2 * D + sl.start, 2 * D + sl.stop)
    pre_ref[K_] = k * (1.0 + (alr - 1.0) * ka_ref[...])
    pre_ref[KK_] = k * kk_ref[...]
    pre_ref[ALR_] = alr
    pre_ref[LW_] = lw
    for r0 in range(0, TB, CB):
        pre_ref[CS_, r0:r0 + CB, :] = _dot_sel2(blk_tril, lw[r0:r0 + CB, :])
    pre_ref[G_] = _dot1(_sigmoid(g_lo), g2_ref[...])

    lane_lo = _iota2((L, PAIR), 1) < RWKV_HEAD_DIM
    row = _iota2((H2, H2), 0)
    col = _iota2((H2, H2), 1)
    eye = jnp.where(row == col, 1.0, 0.0)
    same_head = row // L == col // L
    strict = same_head & (row % L > col % L)
    incl = same_head & (row % L >= col % L)

    def independent_part(chunks):
        lhs_a, v_t, a_ab, a_ak, a_rk, lr_arb, hat_bk, p_all = ([] for _ in range(8))
        for c, q in [(c, q) for c in chunks for q in range(n_pairs)]:
            rs = slice(c * L, (c + 1) * L)
            sl = slice(q * PAIR, (q + 1) * PAIR)
            cs_p = pre_ref[CS_, rs, sl]
            cs_last = pre_ref[CS_, (c + 1) * L - 1:(c + 1) * L, sl]
            p_inv = jnp.exp(-cs_p)
            p_end = jnp.exp(cs_last - cs_p)
            kk_p = pre_ref[KK_, rs, sl]
            kk_n = kk_p * lax.rsqrt(jnp.maximum(_seg_sum(kk_p * kk_p, lane_lo), 1e-24))
            b_p = kk_n * pre_ref[ALR_, rs, sl]
            k_p = pre_ref[K_, rs, sl]
            la = _stack_heads(-kk_n * jnp.exp(cs_p - pre_ref[LW_, rs, sl]), lane_lo).astype(BF16)
            lr = _stack_heads(p_ref[rs, sl] * jnp.exp(cs_p), lane_lo).astype(BF16)
            bt = (b_p * p_inv).astype(BF16)
            kt = (k_p * p_inv).astype(BF16)
            g = _mm(jnp.concatenate([la, lr], axis=0), jnp.concatenate([bt, bt, kt, kt], axis=0), NT)
            a_ab.append(jnp.where(strict, g[0:H2, 0:H2], 0.0))
            a_ak.append(jnp.where(strict, g[0:H2, H2:], 0.0).astype(BF16))
            a_rk.append(jnp.where(incl, g[H2:, H2:], 0.0).astype(BF16))
            lr_arb.append(jnp.concatenate([lr, jnp.where(incl, g[H2:, 0:H2], 0.0).astype(BF16)], axis=1))
            lhs_a.append(la)
            hat_bk.append(jnp.concatenate([_stack_heads(b_p * p_end, lane_lo), _stack_heads(k_p * p_end, lane_lo)],
                                          axis=0).astype(BF16))
            p_all.append(jnp.exp(cs_last))
            v_t.append(_stack_heads(p_ref[rs, v_cols(sl)], lane_lo).T.astype(BF16))
        t_inv = [t.astype(BF16) for t in _unit_lower_inverse_many(a_ab, row, col, eye)]
        t_a = [_mm(t, la).astype(BF16) for t, la in zip(t_inv, lhs_a)]
        av_t = [_mm(vt, a, NT).astype(BF16) for vt, a in zip(v_t, a_ak)]
        tav_t = [_mm(x, t, NT) for x, t in zip(av_t, t_inv)]
        arkv_t = [_mm(vt, a, NT) for vt, a in zip(v_t, a_rk)]
        return dict(t_a=t_a, tav_t=tav_t, arkv_t=arkv_t, lr_arb=lr_arb, hat_bk=hat_bk, v_t=v_t, p_all=p_all)

    def state_part(chunks, d):
        for ci, c in enumerate(chunks):
            rs = slice(c * L, (c + 1) * L)
            idx = [ci * n_pairs + q for q in range(n_pairs)]
            s0 = [st_ref[q] for q in range(n_pairs)]
            s0b = [s.astype(BF16) for s in s0]
            u_tb = [(_mm(s0b[q], d["t_a"][i], NT) + d["tav_t"][i]).astype(BF16) for q, i in enumerate(idx)]
            for q, i in enumerate(idx):
                st_ref[q] = s0[q] * d["p_all"][i] + _mm(jnp.concatenate([u_tb[q], d["v_t"][i]], axis=1),
                                                        d["hat_bk"][i])
            y_t = [_mm(jnp.concatenate([s0b[q], u_tb[q]], axis=1), d["lr_arb"][i], NT) + d["arkv_t"][i]
                   for q, i in enumerate(idx)]
            for q in range(n_pairs):
                sl = slice(q * PAIR, (q + 1) * PAIR)
                y_st = y_t[q].T
                y = y_st[0:L, :] + y_st[L:H2, :]
                mean = _seg_sum(y, lane_lo) * (1.0 / RWKV_HEAD_DIM)
                dev = y - mean
                var = _seg_sum(dev * dev, lane_lo) * (1.0 / RWKV_HEAD_DIM)
                yn = dev * lax.rsqrt(var + RWKV_GN_EPS) * lnw_ref[:, sl] + lnb_ref[:, sl]
                v_p = p_ref[rs, v_cols(sl)]
                bonus = _seg_sum(p_ref[rs, sl] * pre_ref[K_, rs, sl] * rk_ref[:, sl], lane_lo) * v_p
                o_ref[rs, sl] = ((yn + bonus) * pre_ref[G_, rs, sl]).astype(o_ref.dtype)

    groups = [list(range(c0, c0 + RWKV_GROUP)) for c0 in range(0, TB // L, RWKV_GROUP)]
    return groups, independent_part, state_part


def _mixers_kernel(ps_ref, pr_ref, dtb_ref, alog_ref, dsk_ref, ng_ref,
                   w0_ref, w2_ref, a0_ref, a2_ref, g2_ref, kk_ref, ka_ref, rk_ref, lnw_ref, lnb_ref,
                   os_ref, or_ref, sst_ref, rst_ref, pre_ref):
    ssd_chunk = _ssd_parts(ps_ref, dtb_ref, alog_ref, dsk_ref, ng_ref, os_ref, sst_ref)
    groups, independent_part, state_part = _rwkv_parts(
        pr_ref, w0_ref, w2_ref, a0_ref, a2_ref, g2_ref, kk_ref, ka_ref, rk_ref, lnw_ref, lnb_ref,
        or_ref, rst_ref, pre_ref)
    ssd_chunks = iter(range(SSD_BLOCK // SSD_CHUNK))
    per_group = (SSD_BLOCK // SSD_CHUNK) // len(groups)
    ready = []
    for g in groups:
        ready.append(independent_part(g))
        for _ in range(per_group):
            ssd_chunk(next(ssd_chunks))
    for c in ssd_chunks:
        ssd_chunk(c)
    for g, d in zip(groups, ready):
        state_part(g, d)


def _mixers(p_ssd, p_rwkv, dtb, alog, dsk, ng, w0, w2p, a0, a2p, g2, k_k, k_a, r_k, ln_w, ln_b, batch, seq):
    assert SSD_BLOCK == RWKV_BLOCK
    TB = RWKV_BLOCK
    nb = seq // TB
    const = lambda b, c: (0, 0)
    rows = lambda b, c: (b * nb + c, 0)
    vec = pl.BlockSpec((1, RWKV_D), const)
    return pl.pallas_call(
        _mixers_kernel,
        name="mixers",
        grid=(batch, nb),
        in_specs=[
            pl.BlockSpec((TB, SSD_PCOLS), rows),
            pl.BlockSpec((TB, RWKV_COLS), rows),
            pl.BlockSpec((1, LANES), const),
            pl.BlockSpec((1, LANES), const),
            pl.BlockSpec((1, SSD_D), const),
            pl.BlockSpec((1, SSD_D), const),
            vec,
            pl.BlockSpec((LANES, RWKV_D), const),
            vec,
            pl.BlockSpec((LANES, RWKV_D), const),
            pl.BlockSpec((GATE_LORA, RWKV_D), const),
            vec, vec, vec, vec, vec,
        ],
        out_specs=[pl.BlockSpec((TB, SSD_D), rows), pl.BlockSpec((TB, RWKV_D), rows)],
        out_shape=[jax.ShapeDtypeStruct((batch * seq, SSD_D), BF16),
                   jax.ShapeDtypeStruct((batch * seq, RWKV_D), BF16)],
        scratch_shapes=[
            pltpu.VMEM((SSD_GROUPS, SSD_STATE, SSD_D // SSD_GROUPS), F32),
            pltpu.VMEM((RWKV_D // PAIR, PAIR, PAIR), F32),
            pltpu.VMEM((6, TB, RWKV_D), F32),
        ],
        compiler_params=pltpu.CompilerParams(
            dimension_semantics=("arbitrary", "arbitrary"), vmem_limit_bytes=VMEM_LIMIT),
    )(p_ssd, p_rwkv, dtb, alog, dsk, ng, w0, w2p, a0, a2p, g2, k_k, k_a, r_k, ln_w, ln_b)


def _ffn_kernel(ys_ref, yr_ref, x_ref, wo_ref, g1_ref, g2_ref, wup_ref, cw_ref, cb_ref, wdn_ref, g3_ref,
                o_ref, ubuf_ref, *, tm):
    @pl.when(pl.program_id(1) == 0)
    def _():
        ubuf_ref[...] = jnp.zeros(ubuf_ref.shape, F32)

    mix = _mm(ys_ref[...], wo_ref[0:SSD_D, :]) + _mm(yr_ref[...], wo_ref[SSD_D:, :])
    h = x_ref[...] + _rms(mix, g1_ref[...])
    hn = _rms(h, g2_ref[...]).astype(BF16)

    chunks = [(c0, min(FFN_COLS, D_FF - c0)) for c0 in range(0, D_FF, FFN_COLS)]

    def up(j):
        c0, width = chunks[j]
        return [_mm(hn, wup_ref[:, c:c + width]) for c in (c0, D_FF + c0)]

    def conv(u, c0):
        cs = slice(c0, c0 + u.shape[1])
        tail = ubuf_ref[:, cs]
        out = cb_ref[:, cs] + cw_ref[FFN_CONV - 1:FFN_CONV, cs] * u
        for k in range(1, FFN_CONV):
            out = out + cw_ref[FFN_CONV - 1 - k:FFN_CONV - k, cs] * _shift_rows(u, tail, k)
        ubuf_ref[:, cs] = u[tm - SUBLANES:tm, :]
        return out

    n_steps = len(chunks)
    u_next = up(0)
    acts = []
    for j in range(n_steps):
        u_gate, u_val = u_next
        if j + 1 < n_steps:
            u_next = up(j + 1)
        gate = conv(u_gate, chunks[j][0])
        val = conv(u_val, D_FF + chunks[j][0])
        acts.append((_silu(gate) * val).astype(BF16))
    f = _mm(jnp.concatenate(acts, axis=1), wdn_ref[...])
    o_ref[...] = h + _rms(f, g3_ref[...])


def _ffn(ys, yr, x2, wo, g1, g2, wup, cw, cb, wdn, g3, batch, seq, tm):
    nb = seq // tm
    const = lambda b, i: (0, 0)
    rows = lambda b, i: (b * nb + i, 0)
    res = functools.partial(pl.BlockSpec, index_map=const, pipeline_mode=pl.Buffered(1))
    return pl.pallas_call(
        functools.partial(_ffn_kernel, tm=tm),
        name="outproj_ffn",
        grid=(batch, nb),
        in_specs=[
            pl.BlockSpec((tm, SSD_D), rows),
            pl.BlockSpec((tm, RWKV_D), rows),
            pl.BlockSpec((tm, D_MODEL), rows),
            res((SSD_D + RWKV_D, D_MODEL)),
            pl.BlockSpec((1, D_MODEL), const),
            pl.BlockSpec((1, D_MODEL), const),
            res((D_MODEL, 2 * D_FF)),
            pl.BlockSpec((FFN_CONV, 2 * D_FF), const),
            pl.BlockSpec((1, 2 * D_FF), const),
            res((D_FF, D_MODEL)),
            pl.BlockSpec((1, D_MODEL), const),
        ],
        out_specs=pl.BlockSpec((tm, D_MODEL), rows),
        out_shape=jax.ShapeDtypeStruct((batch * seq, D_MODEL), F32),
        scratch_shapes=[
            pltpu.VMEM((SUBLANES, 2 * D_FF), F32),
        ],
        compiler_params=pltpu.CompilerParams(
            dimension_semantics=("arbitrary", "arbitrary"), vmem_limit_bytes=VMEM_LIMIT),
    )(ys, yr, x2, wo, g1, g2, wup, cw, cb, wdn, g3)


def _pad_lanes(v):
    return jnp.pad(v.astype(F32), (0, LANES - v.shape[0]))[None, :]


def _layer(h2, batch, seq, pre_mix_norm, w_in, ssd_conv_w, ssd_conv_b, ssd_dt_bias, ssd_a_log, ssd_d, ssd_norm,
           rwkv_mu, rwkv_w0, rwkv_w2, rwkv_a0, rwkv_a2, rwkv_g2, rwkv_k_k, rwkv_k_a, rwkv_r_k,
           rwkv_ln_w, rwkv_ln_b, w_out, post_mix_norm, pre_ffn_norm, ffn_w_up, ffn_conv_w,
           ffn_conv_b, ffn_w_down, post_ffn_norm):
    row = lambda v: v.astype(F32).reshape(1, -1)

    w_in = w_in.astype(BF16)
    w_ssd = jnp.pad(w_in[:, :SSD_COLS], ((0, 0), (0, SSD_PCOLS - SSD_COLS)))
    i1 = RWKV_D
    i2 = i1 + DECAY_LORA
    i3 = i2 + RWKV_D
    i4 = i3 + RWKV_D
    i5 = i4 + AAA_LORA
    perm = lambda t: jnp.concatenate(
        [t[..., 0:i1], t[..., i2:i3], t[..., i3:i4], t[..., i1:i2], t[..., i4:i5], t[..., i5:]], axis=-1)
    w_rwkv = perm(w_in[:, SSD_COLS:])
    mu = perm(rwkv_mu).astype(F32).reshape(1, -1)
    w2p = jnp.concatenate([rwkv_w2, jnp.zeros((AAA_LORA, RWKV_D), rwkv_w2.dtype)], axis=0).astype(BF16)
    a2p = jnp.concatenate([jnp.zeros((DECAY_LORA, RWKV_D), rwkv_a2.dtype), rwkv_a2], axis=0).astype(BF16)

    p_ssd, p_rwkv = _inproj(h2, row(pre_mix_norm), w_ssd, w_rwkv, ssd_conv_w.astype(F32), row(ssd_conv_b), mu,
                            seq, tm=INPROJ_ROWS)
    y_ssd, y_rwkv = _mixers(
        p_ssd, p_rwkv, _pad_lanes(ssd_dt_bias), _pad_lanes(ssd_a_log), row(jnp.repeat(ssd_d, SSD_HEAD_DIM)),
        row(ssd_norm), row(rwkv_w0), w2p, row(rwkv_a0), a2p, rwkv_g2.astype(BF16), row(rwkv_k_k),
        row(rwkv_k_a), row(rwkv_r_k), row(rwkv_ln_w), row(rwkv_ln_b), batch, seq)
    return _ffn(y_ssd, y_rwkv, h2, w_out.astype(BF16), row(post_mix_norm), row(pre_ffn_norm),
                ffn_w_up.astype(BF16), ffn_conv_w.astype(F32), row(ffn_conv_b), ffn_w_down.astype(BF16),
                row(post_ffn_norm), batch, seq, tm=FFN_ROWS)


def kernel(x, pre_mix_norm, w_in, ssd_conv_w, ssd_conv_b, ssd_dt_bias, ssd_a_log, ssd_d, ssd_norm, rwkv_mu, rwkv_w0, rwkv_w2, rwkv_a0, rwkv_a2, rwkv_g2, rwkv_k_k, rwkv_k_a, rwkv_r_k, rwkv_ln_w, rwkv_ln_b, w_out, post_mix_norm, pre_ffn_norm, ffn_w_up, ffn_conv_w, ffn_conv_b, ffn_w_down, post_ffn_norm):
    batch, seq, d = x.shape
    h2 = x.reshape(batch * seq, d)
    params = (pre_mix_norm, w_in, ssd_conv_w, ssd_conv_b, ssd_dt_bias, ssd_a_log, ssd_d, ssd_norm, rwkv_mu,
              rwkv_w0, rwkv_w2, rwkv_a0, rwkv_a2, rwkv_g2, rwkv_k_k, rwkv_k_a, rwkv_r_k, rwkv_ln_w, rwkv_ln_b,
              w_out, post_mix_norm, pre_ffn_norm, ffn_w_up, ffn_conv_w, ffn_conv_b, ffn_w_down, post_ffn_norm)
    for l in range(pre_mix_norm.shape[0]):
        h2 = _layer(h2, batch, seq, *(t[l] for t in params))
    return h2.reshape(batch, seq, d)
```

```python
import functools

import jax
import jax.numpy as jnp
from jax import lax
from jax.experimental import pallas as pl
from jax.experimental.pallas import tpu as pltpu

F32 = jnp.float32
BF16 = jnp.bfloat16

D_MODEL = 1024
SSD_HEADS = 8
SSD_HEAD_DIM = 64
SSD_D = SSD_HEADS * SSD_HEAD_DIM
SSD_GROUPS = 2
SSD_STATE = 128
SSD_CONV = 4
SSD_CHUNK = 128
SSD_CONV_DIM = SSD_D + 2 * SSD_GROUPS * SSD_STATE
SSD_COLS = SSD_D + SSD_CONV_DIM + SSD_HEADS
RWKV_HEADS = 8
RWKV_HEAD_DIM = 64
RWKV_D = RWKV_HEADS * RWKV_HEAD_DIM
DECAY_LORA = 64
AAA_LORA = 64
GATE_LORA = 128
RWKV_COLS = 3 * RWKV_D + DECAY_LORA + AAA_LORA + GATE_LORA
RWKV_GN_EPS = 64e-5
D_FF = 2816
FFN_CONV = 3
NORM_EPS = 1e-6
LOG2_E = 1.4426950408889634
EXP_M_HALF = 0.6065306597126334

LANES = 128
SUBLANES = 8
SSD_PCOLS = SSD_D + SSD_CONV_DIM + LANES
SSD_BLOCK = 512
RWKV_CHUNK = 64
RWKV_BLOCK = 512
INPROJ_COLS = 256
RWKV_GROUP = 4
PAIR = 2 * RWKV_HEAD_DIM
FFN_COLS = 512
INPROJ_ROWS = 1024
FFN_ROWS = 1024
VMEM_LIMIT = 56 * 1024 * 1024

NN = (((1,), (0,)), ((), ()))
NT = (((1,), (1,)), ((), ()))
TN = (((0,), (0,)), ((), ()))


def _mm(a, b, dims=NN):
    return lax.dot_general(a, b, dims, preferred_element_type=F32)


def _dot1(a, b, dims=NN):
    return _mm(a.astype(BF16), b.astype(BF16), dims)


def _split3(a):
    a1 = a.astype(BF16)
    r1 = a - a1.astype(F32)
    a2 = r1.astype(BF16)
    a3 = (r1 - a2.astype(F32)).astype(BF16)
    return a1, a2, a3


def _dot_sel(sel_bf16, a):
    a1, a2, a3 = _split3(a)
    return _mm(sel_bf16, a1) + (_mm(sel_bf16, a2) + _mm(sel_bf16, a3))


def _a_dot_sel(a, sel_bf16):
    a1, a2, a3 = _split3(a)
    return _mm(a1, sel_bf16) + (_mm(a2, sel_bf16) + _mm(a3, sel_bf16))


def _dot_sel2(sel_bf16, a):
    hi = a.astype(BF16)
    lo = (a - hi.astype(F32)).astype(BF16)
    return _mm(sel_bf16, hi) + _mm(sel_bf16, lo)


def _shift_rows(x, tail, k):
    sub = _iota2((SUBLANES, x.shape[1]), 0)
    r = pltpu.roll(x, k, axis=0)
    head = jnp.where(sub < k, pltpu.roll(tail, k, axis=0), r[0:SUBLANES, :])
    return jnp.concatenate([head, r[SUBLANES:, :]], axis=0)


def _rms(x, g):
    return x * lax.rsqrt(jnp.mean(x * x, axis=-1, keepdims=True) + NORM_EPS) * g


def _sigmoid(x):
    return 0.5 + 0.5 * jnp.tanh(0.5 * x)


def _silu(x):
    h = 0.5 * x
    return h + h * jnp.tanh(h)


def _softplus(x):
    return jnp.maximum(x, 0.0) + jnp.log(1.0 + jnp.exp(-jnp.abs(x)))


def _iota2(shape, axis):
    return lax.broadcasted_iota(jnp.int32, shape, axis)


def _inproj_kernel(x_ref, g_ref, ws_ref, wr_ref, cw_ref, cb_ref, mu_ref, ps_ref, pr_ref, tail_s_ref, tail_r_ref,
                   *, tm, blocks_per_seq):
    i = pl.program_id(0)

    @pl.when(i == 0)
    def _():
        tail_s_ref[...] = jnp.zeros(tail_s_ref.shape, F32)
        tail_r_ref[...] = jnp.zeros(tail_r_ref.shape, F32)

    first = i % blocks_per_seq == 0
    xb = _rms(x_ref[...], g_ref[...]).astype(BF16)
    xbc0 = SSD_D
    dt0 = SSD_D + SSD_CONV_DIM
    def conv_cols(c0):
        ts = slice(c0, c0 + INPROJ_COLS)
        cs = slice(xbc0 + c0, xbc0 + c0 + INPROJ_COLS)
        u = _mm(xb, ws_ref[:, cs])
        tail = jnp.where(first, 0.0, tail_s_ref[:, ts])
        acc = cb_ref[:, ts] + cw_ref[SSD_CONV - 1:SSD_CONV, ts] * u
        for k in range(1, SSD_CONV):
            acc = acc + cw_ref[SSD_CONV - 1 - k:SSD_CONV - k, ts] * _shift_rows(u, tail, k)
        tail_s_ref[:, ts] = u[tm - SUBLANES:tm, :]
        ps_ref[:, cs] = _silu(acc)

    def shift_cols(c0):
        cs = slice(c0, c0 + INPROJ_COLS)
        p = _mm(xb, wr_ref[:, cs])
        tail = jnp.where(first, 0.0, tail_r_ref[:, cs])
        prev = _shift_rows(p, tail, 1)
        tail_r_ref[:, cs] = p[tm - SUBLANES:tm, :]
        pr_ref[:, cs] = p + (prev - p) * mu_ref[:, cs]

    def plain_cols(lo, hi):
        ps_ref[:, lo:hi] = _mm(xb, ws_ref[:, lo:hi])

    conv = [functools.partial(conv_cols, c0) for c0 in range(0, SSD_CONV_DIM, INPROJ_COLS)]
    light = ([functools.partial(plain_cols, c0, c0 + INPROJ_COLS) for c0 in range(0, xbc0, INPROJ_COLS)]
             + [functools.partial(plain_cols, dt0, SSD_PCOLS)]
             + [functools.partial(shift_cols, c0) for c0 in range(0, RWKV_COLS, INPROJ_COLS)])
    per_conv = 2
    for j, step in enumerate(conv):
        step()
        for fill in light[j * per_conv:(j + 1) * per_conv]:
            fill()
    for fill in light[len(conv) * per_conv:]:
        fill()


def _inproj(x2, g, w_ssd, w_rwkv, cw, cb, mu, seq, tm):
    n = x2.shape[0]
    const = lambda i: (0, 0)
    return pl.pallas_call(
        functools.partial(_inproj_kernel, tm=tm, blocks_per_seq=seq // tm),
        name="inproj",
        grid=(n // tm,),
        in_specs=[
            pl.BlockSpec((tm, D_MODEL), lambda i: (i, 0)),
            pl.BlockSpec((1, D_MODEL), const),
            pl.BlockSpec((D_MODEL, SSD_PCOLS), const, pipeline_mode=pl.Buffered(1)),
            pl.BlockSpec((D_MODEL, RWKV_COLS), const, pipeline_mode=pl.Buffered(1)),
            pl.BlockSpec((SSD_CONV, SSD_CONV_DIM), const),
            pl.BlockSpec((1, SSD_CONV_DIM), const),
            pl.BlockSpec((1, RWKV_COLS), const),
        ],
        out_specs=[
            pl.BlockSpec((tm, SSD_PCOLS), lambda i: (i, 0)),
            pl.BlockSpec((tm, RWKV_COLS), lambda i: (i, 0)),
        ],
        out_shape=[
            jax.ShapeDtypeStruct((n, SSD_PCOLS), F32),
            jax.ShapeDtypeStruct((n, RWKV_COLS), F32),
        ],
        scratch_shapes=[
            pltpu.VMEM((SUBLANES, SSD_CONV_DIM), F32),
            pltpu.VMEM((SUBLANES, RWKV_COLS), F32),
        ],
        compiler_params=pltpu.CompilerParams(
            dimension_semantics=("arbitrary",), vmem_limit_bytes=VMEM_LIMIT),
    )(x2, g, w_ssd, w_rwkv, cw, cb, mu)


def _ssd_parts(p_ref, dtb_ref, alog_ref, dsk_ref, ng_ref, o_ref, st_ref):
    L = SSD_CHUNK
    gw = SSD_D // SSD_GROUPS

    @pl.when(pl.program_id(1) == 0)
    def _():
        st_ref[...] = jnp.zeros(st_ref.shape, F32)

    row = _iota2((L, L), 0)
    col = _iota2((L, L), 1)
    causal = row >= col
    tril = jnp.where(causal, 1.0, 0.0).astype(BF16)
    hsel = (_iota2((LANES, SSD_D), 1) // SSD_HEAD_DIM == _iota2((LANES, SSD_D), 0))
    hsel = jnp.where(hsel, 1.0, 0.0).astype(BF16)
    lane_lo = _iota2((L, LANES), 1) < SSD_HEAD_DIM
    xbc_cols = slice(SSD_D, SSD_D + SSD_CONV_DIM)

    def chunk(c):
        rs = slice(c * L, (c + 1) * L)
        z = p_ref[rs, 0:SSD_D]
        xbc = p_ref[rs, xbc_cols]
        dt_raw = p_ref[rs, SSD_D + SSD_CONV_DIM:SSD_PCOLS]
        xs = xbc[:, 0:SSD_D]

        dt = _softplus(dt_raw + dtb_ref[...])
        a = dt * (-LOG2_E * jnp.exp(alog_ref[...]))
        a_cs = _dot_sel(tril, a)
        a_cs_t = a_cs.T
        dt_e = _mm(dt.astype(BF16), hsel)
        acs_e = _a_dot_sel(a_cs, hsel)
        ea_e = jnp.exp2(acs_e)
        ds_e = jnp.exp2(acs_e[L - 1:L, :] - acs_e)

        x_dt = xs * dt_e
        x_b = x_dt.astype(BF16)
        x_dec = (x_dt * ds_e).astype(BF16)

        for g in range(SSD_GROUPS):
            gs = slice(g * gw, (g + 1) * gw)
            b_g = xbc[:, SSD_D + g * SSD_STATE:SSD_D + (g + 1) * SSD_STATE].astype(BF16)
            c_off = SSD_D + SSD_GROUPS * SSD_STATE
            c_g = xbc[:, c_off + g * SSD_STATE:c_off + (g + 1) * SSD_STATE].astype(BF16)
            scores = _mm(c_g, b_g, NT)
            state = st_ref[g]
            y_off = _mm(c_g, state.astype(BF16)) * ea_e[:, gs]
            st_ref[g] = state * ea_e[L - 1:L, gs] + _mm(b_g, x_dec[:, gs], TN)
            y_parts = []
            for j in range(gw // LANES):
                h0 = g * (SSD_HEADS // SSD_GROUPS) + 2 * j
                ms = []
                for h in (h0, h0 + 1):
                    seg = a_cs[:, h:h + 1] - a_cs_t[h:h + 1, :]
                    dec = jnp.exp2(jnp.where(causal, seg, -jnp.inf))
                    ms.append((scores * dec).astype(BF16))
                xp = x_b[:, h0 * SSD_HEAD_DIM:h0 * SSD_HEAD_DIM + LANES]
                zero = jnp.zeros_like(xp)
                x_bd = jnp.concatenate([jnp.where(lane_lo, xp, zero), jnp.where(lane_lo, zero, xp)], axis=0)
                y_parts.append(_mm(jnp.concatenate(ms, axis=1), x_bd))
            y = jnp.concatenate(y_parts, axis=1) + y_off
            y = y + dsk_ref[:, gs] * xs[:, gs]
            y = y * _silu(z[:, gs])
            y = y * lax.rsqrt(jnp.mean(y * y, axis=-1, keepdims=True) + NORM_EPS)
            o_ref[rs, gs] = (y * ng_ref[:, gs]).astype(o_ref.dtype)
    return chunk


def _seg_sum(x, lane_lo):
    s_lo = jnp.sum(jnp.where(lane_lo, x, 0.0), axis=-1, keepdims=True)
    s_hi = jnp.sum(jnp.where(lane_lo, 0.0, x), axis=-1, keepdims=True)
    return jnp.where(lane_lo, s_lo, s_hi)


def _stack_heads(x, lane_lo):
    zero = jnp.zeros_like(x)
    return jnp.concatenate([jnp.where(lane_lo, x, zero), jnp.where(lane_lo, zero, x)], axis=0)


def _unit_lower_inverse_many(a_list, row, col, eye):
    blk8 = row // 8 == col // 8
    a8f = [jnp.where(blk8, a, 0.0) for a in a_list]
    a8 = [x.astype(BF16) for x in a8f]
    t = [eye + x for x in a8f]
    a2 = [_mm(x, x).astype(BF16) for x in a8]
    t = [ti + _mm(ti.astype(BF16), x) for ti, x in zip(t, a2)]
    a4 = [_mm(x, x).astype(BF16) for x in a2]
    t = [ti + _mm(ti.astype(BF16), x) for ti, x in zip(t, a4)]
    n = a_list[0].shape[0]
    for s in (8, 16, 32):
        lower_left = (row // (2 * s) == col // (2 * s)) & ((row // s) % 2 == 1) & ((col // s) % 2 == 0)
        second = [slice(r0 + s, r0 + 2 * s) for r0 in range(0, n, 2 * s)]
        tb = [ti.astype(BF16) for ti in t]
        t2 = [jnp.concatenate([ti[r, :] for r in second], axis=0) for ti in t]
        x = [_mm(t2i.astype(BF16), jnp.where(lower_left, a, 0.0).astype(BF16)).astype(BF16)
             for t2i, a in zip(t2, a_list)]
        t2 = [t2i + _mm(xi, tbi) for t2i, xi, tbi in zip(t2, x, tb)]
        t = [jnp.concatenate([piece for m, r in enumerate(second)
                              for piece in (ti[r.start - s:r.start, :], t2i[m * s:(m + 1) * s, :])], axis=0)
             for ti, t2i in zip(t, t2)]
    return t


def _rwkv_parts(p_ref, w0_ref, w2_ref, a0_ref, a2_ref, g2_ref, kk_ref, ka_ref, rk_ref,
                lnw_ref, lnb_ref, o_ref, st_ref, pre_ref):
    L = RWKV_CHUNK
    TB = RWKV_BLOCK
    H2 = 2 * L
    D = RWKV_D
    n_pairs = D // PAIR

    @pl.when(pl.program_id(1) == 0)
    def _():
        st_ref[...] = jnp.zeros(st_ref.shape, F32)

    k = p_ref[:, D:2 * D]
    wa = p_ref[:, 3 * D:3 * D + LANES]
    g_lo = p_ref[:, 3 * D + LANES:3 * D + 2 * LANES]
    lw = -EXP_M_HALF * _sigmoid(w0_ref[...] + _dot1(jnp.tanh(wa), w2_ref[...]))
    alr = _sigmoid(a0_ref[...] + _dot1(wa, a2_ref[...]))
    CB = 4 * L
    blk_tril = (_iota2((CB, CB), 0) >= _iota2((CB, CB), 1)) & (_iota2((CB, CB), 0) // L == _iota2((CB, CB), 1) // L)
    blk_tril = jnp.where(blk_tril, 1.0, 0.0).astype(BF16)
    K_, KK_, ALR_, LW_, CS_, G_ = range(6)
    v_cols = lambda sl: slice(2 * D + sl.start, 2 * D + sl.stop)
    pre_ref[K_] = k * (1.0 + (alr - 1.0) * ka_ref[...])
    pre_ref[KK_] = k * kk_ref[...]
    pre_ref[ALR_] = alr
    pre_ref[LW_] = lw
    for r0 in range(0, TB, CB):
        pre_ref[CS_, r0:r0 + CB, :] = _dot_sel2(blk_tril, lw[r0:r0 + CB, :])
    pre_ref[G_] = _dot1(_sigmoid(g_lo), g2_ref[...])

    lane_lo = _iota2((L, PAIR), 1) < RWKV_HEAD_DIM
    row = _iota2((H2, H2), 0)
    col = _iota2((H2, H2), 1)
    eye = jnp.where(row == col, 1.0, 0.0)
    same_head = row // L == col // L
    strict = same_head & (row % L > col % L)
    incl = same_head & (row % L >= col % L)

    def independent_part(chunks):
        lhs_a, v_t, a_ab, a_ak, a_rk, lr_arb, hat_bk, p_all = ([] for _ in range(8))
        for c, q in [(c, q) for c in chunks for q in range(n_pairs)]:
            rs = slice(c * L, (c + 1) * L)
            sl = slice(q * PAIR, (q + 1) * PAIR)
            cs_p = pre_ref[CS_, rs, sl]
            cs_last = pre_ref[CS_, (c + 1) * L - 1:(c + 1) * L, sl]
            p_inv = jnp.exp(-cs_p)
            p_end = jnp.exp(cs_last - cs_p)
            kk_p = pre_ref[KK_, rs, sl]
            kk_n = kk_p * lax.rsqrt(jnp.maximum(_seg_sum(kk_p * kk_p, lane_lo), 1e-24))
            b_p = kk_n * pre_ref[ALR_, rs, sl]
            k_p = pre_ref[K_, rs, sl]
            la = _stack_heads(-kk_n * jnp.exp(cs_p - pre_ref[LW_, rs, sl]), lane_lo).astype(BF16)
            lr = _stack_heads(p_ref[rs, sl] * jnp.exp(cs_p), lane_lo).astype(BF16)
            bt = (b_p * p_inv).astype(BF16)
            kt = (k_p * p_inv).astype(BF16)
            g = _mm(jnp.concatenate([la, lr], axis=0), jnp.concatenate([bt, bt, kt, kt], axis=0), NT)
            a_ab.append(jnp.where(strict, g[0:H2, 0:H2], 0.0))
            a_ak.append(jnp.where(strict, g[0:H2, H2:], 0.0).astype(BF16))
            a_rk.append(jnp.where(incl, g[H2:, H2:], 0.0).astype(BF16))
            lr_arb.append(jnp.concatenate([lr, jnp.where(incl, g[H2:, 0:H2], 0.0).astype(BF16)], axis=1))
            lhs_a.append(la)
            hat_bk.append(jnp.concatenate([_stack_heads(b_p * p_end, lane_lo), _stack_heads(k_p * p_end, lane_lo)],
                                          axis=0).astype(BF16))
            p_all.append(jnp.exp(cs_last))
            v_t.append(_stack_heads(p_ref[rs, v_cols(sl)], lane_lo).T.astype(BF16))
        t_inv = [t.astype(BF16) for t in _unit_lower_inverse_many(a_ab, row, col, eye)]
        t_a = [_mm(t, la).astype(BF16) for t, la in zip(t_inv, lhs_a)]
        av_t = [_mm(vt, a, NT).astype(BF16) for vt, a in zip(v_t, a_ak)]
        tav_t = [_mm(x, t, NT) for x, t in zip(av_t, t_inv)]
        arkv_t = [_mm(vt, a, NT) for vt, a in zip(v_t, a_rk)]
        return dict(t_a=t_a, tav_t=tav_t, arkv_t=arkv_t, lr_arb=lr_arb, hat_bk=hat_bk, v_t=v_t, p_all=p_all)

    def state_part(chunks, d):
        for ci, c in enumerate(chunks):
            rs = slice(c * L, (c + 1) * L)
            idx = [ci * n_pairs + q for q in range(n_pairs)]
            s0 = [st_ref[q] for q in range(n_pairs)]
            s0b = [s.astype(BF16) for s in s0]
            u_tb = [(_mm(s0b[q], d["t_a"][i], NT) + d["tav_t"][i]).astype(BF16) for q, i in enumerate(idx)]
            for q, i in enumerate(idx):
                st_ref[q] = s0[q] * d["p_all"][i] + _mm(jnp.concatenate([u_tb[q], d["v_t"][i]], axis=1),
                                                        d["hat_bk"][i])
            y_t = [_mm(jnp.concatenate([s0b[q], u_tb[q]], axis=1), d["lr_arb"][i], NT) + d["arkv_t"][i]
                   for q, i in enumerate(idx)]
            for q in range(n_pairs):
                sl = slice(q * PAIR, (q + 1) * PAIR)
                y_st = y_t[q].T
                y = y_st[0:L, :] + y_st[L:H2, :]
                mean = _seg_sum(y, lane_lo) * (1.0 / RWKV_HEAD_DIM)
                dev = y - mean
                var = _seg_sum(dev * dev, lane_lo) * (1.0 / RWKV_HEAD_DIM)
                yn = dev * lax.rsqrt(var + RWKV_GN_EPS) * lnw_ref[:, sl] + lnb_ref[:, sl]
                v_p = p_ref[rs, v_cols(sl)]
                bonus = _seg_sum(p_ref[rs, sl] * pre_ref[K_, rs, sl] * rk_ref[:, sl], lane_lo) * v_p
                o_ref[rs, sl] = ((yn + bonus) * pre_ref[G_, rs, sl]).astype(o_ref.dtype)

    groups = [list(range(c0, c0 + RWKV_GROUP)) for c0 in range(0, TB // L, RWKV_GROUP)]
    return groups, independent_part, state_part


def _mixers_kernel(ps_ref, pr_ref, dtb_ref, alog_ref, dsk_ref, ng_ref,
                   w0_ref, w2_ref, a0_ref, a2_ref, g2_ref, kk_ref, ka_ref, rk_ref, lnw_ref, lnb_ref,
                   os_ref, or_ref, sst_ref, rst_ref, pre_ref):
    ssd_chunk = _ssd_parts(ps_ref, dtb_ref, alog_ref, dsk_ref, ng_ref, os_ref, sst_ref)
    groups, independent_part, state_part = _rwkv_parts(
        pr_ref, w0_ref, w2_ref, a0_ref, a2_ref, g2_ref, kk_ref, ka_ref, rk_ref, lnw_ref, lnb_ref,
        or_ref, rst_ref, pre_ref)
    ssd_chunks = iter(range(SSD_BLOCK // SSD_CHUNK))
    per_group = (SSD_BLOCK // SSD_CHUNK) // len(groups)
    ready = []
    for g in groups:
        ready.append(independent_part(g))
        for _ in range(per_group):
            ssd_chunk(next(ssd_chunks))
    for c in ssd_chunks:
        ssd_chunk(c)
    for g, d in zip(groups, ready):
        state_part(g, d)


def _mixers(p_ssd, p_rwkv, dtb, alog, dsk, ng, w0, w2p, a0, a2p, g2, k_k, k_a, r_k, ln_w, ln_b, batch, seq):
    assert SSD_BLOCK == RWKV_BLOCK
    TB = RWKV_BLOCK
    nb = seq // TB
    const = lambda b, c: (0, 0)
    rows = lambda b, c: (b * nb + c, 0)
    vec = pl.BlockSpec((1, RWKV_D), const)
    return pl.pallas_call(
        _mixers_kernel,
        name="mixers",
        grid=(batch, nb),
        in_specs=[
            pl.BlockSpec((TB, SSD_PCOLS), rows),
            pl.BlockSpec((TB, RWKV_COLS), rows),
            pl.BlockSpec((1, LANES), const),
            pl.BlockSpec((1, LANES), const),
            pl.BlockSpec((1, SSD_D), const),
            pl.BlockSpec((1, SSD_D), const),
            vec,
            pl.BlockSpec((LANES, RWKV_D), const),
            vec,
            pl.BlockSpec((LANES, RWKV_D), const),
            pl.BlockSpec((GATE_LORA, RWKV_D), const),
            vec, vec, vec, vec, vec,
        ],
        out_specs=[pl.BlockSpec((TB, SSD_D), rows), pl.BlockSpec((TB, RWKV_D), rows)],
        out_shape=[jax.ShapeDtypeStruct((batch * seq, SSD_D), BF16),
                   jax.ShapeDtypeStruct((batch * seq, RWKV_D), BF16)],
        scratch_shapes=[
            pltpu.VMEM((SSD_GROUPS, SSD_STATE, SSD_D // SSD_GROUPS), F32),
            pltpu.VMEM((RWKV_D // PAIR, PAIR, PAIR), F32),
            pltpu.VMEM((6, TB, RWKV_D), F32),
        ],
        compiler_params=pltpu.CompilerParams(
            dimension_semantics=("arbitrary", "arbitrary"), vmem_limit_bytes=VMEM_LIMIT),
    )(p_ssd, p_rwkv, dtb, alog, dsk, ng, w0, w2p, a0, a2p, g2, k_k, k_a, r_k, ln_w, ln_b)


def _ffn_kernel(ys_ref, yr_ref, x_ref, wo_ref, g1_ref, g2_ref, wup_ref, cw_ref, cb_ref, wdn_ref, g3_ref,
                o_ref, ubuf_ref, *, tm):
    @pl.when(pl.program_id(1) == 0)
    def _():
        ubuf_ref[...] = jnp.zeros(ubuf_ref.shape, F32)

    mix = _mm(ys_ref[...], wo_ref[0:SSD_D, :]) + _mm(yr_ref[...], wo_ref[SSD_D:, :])
    h = x_ref[...] + _rms(mix, g1_ref[...])
    hn = _rms(h, g2_ref[...]).astype(BF16)

    chunks = [(c0, min(FFN_COLS, D_FF - c0)) for c0 in range(0, D_FF, FFN_COLS)]

    def up(j):
        c0, width = chunks[j]
        return [_mm(hn, wup_ref[:, c:c + width]) for c in (c0, D_FF + c0)]

    def conv(u, c0):
        cs = slice(c0, c0 + u.shape[1])
        tail = ubuf_ref[:, cs]
        out = cb_ref[:, cs] + cw_ref[FFN_CONV - 1:FFN_CONV, cs] * u
        for k in range(1, FFN_CONV):
            out = out + cw_ref[FFN_CONV - 1 - k:FFN_CONV - k, cs] * _shift_rows(u, tail, k)
        ubuf_ref[:, cs] = u[tm - SUBLANES:tm, :]
        return out

    n_steps = len(chunks)
    u_next = up(0)
    acts = []
    for j in range(n_steps):
        u_gate, u_val = u_next
        if j + 1 < n_steps:
            u_next = up(j + 1)
        gate = conv(u_gate, chunks[j][0])
        val = conv(u_val, D_FF + chunks[j][0])
        acts.append((_silu(gate) * val).astype(BF16))
    f = _mm(jnp.concatenate(acts, axis=1), wdn_ref[...])
    o_ref[...] = h + _rms(f, g3_ref[...])


def _ffn(ys, yr, x2, wo, g1, g2, wup, cw, cb, wdn, g3, batch, seq, tm):
    nb = seq // tm
    const = lambda b, i: (0, 0)
    rows = lambda b, i: (b * nb + i, 0)
    res = functools.partial(pl.BlockSpec, index_map=const, pipeline_mode=pl.Buffered(1))
    return pl.pallas_call(
        functools.partial(_ffn_kernel, tm=tm),
        name="outproj_ffn",
        grid=(batch, nb),
        in_specs=[
            pl.BlockSpec((tm, SSD_D), rows),
            pl.BlockSpec((tm, RWKV_D), rows),
            pl.BlockSpec((tm, D_MODEL), rows),
            res((SSD_D + RWKV_D, D_MODEL)),
            pl.BlockSpec((1, D_MODEL), const),
            pl.BlockSpec((1, D_MODEL), const),
            res((D_MODEL, 2 * D_FF)),
            pl.BlockSpec((FFN_CONV, 2 * D_FF), const),
            pl.BlockSpec((1, 2 * D_FF), const),
            res((D_FF, D_MODEL)),
            pl.BlockSpec((1, D_MODEL), const),
        ],
        out_specs=pl.BlockSpec((tm, D_MODEL), rows),
        out_shape=jax.ShapeDtypeStruct((batch * seq, D_MODEL), F32),
        scratch_shapes=[
            pltpu.VMEM((SUBLANES, 2 * D_FF), F32),
        ],
        compiler_params=pltpu.CompilerParams(
            dimension_semantics=("arbitrary", "arbitrary"), vmem_limit_bytes=VMEM_LIMIT),
    )(ys, yr, x2, wo, g1, g2, wup, cw, cb, wdn, g3)


def _pad_lanes(v):
    return jnp.pad(v.astype(F32), (0, LANES - v.shape[0]))[None, :]


def _layer(h2, batch, seq, pre_mix_norm, w_in, ssd_conv_w, ssd_conv_b, ssd_dt_bias, ssd_a_log, ssd_d, ssd_norm,
           rwkv_mu, rwkv_w0, rwkv_w2, rwkv_a0, rwkv_a2, rwkv_g2, rwkv_k_k, rwkv_k_a, rwkv_r_k,
           rwkv_ln_w, rwkv_ln_b, w_out, post_mix_norm, pre_ffn_norm, ffn_w_up, ffn_conv_w,
           ffn_conv_b, ffn_w_down, post_ffn_norm):
    row = lambda v: v.astype(F32).reshape(1, -1)

    w_in = w_in.astype(BF16)
    w_ssd = jnp.pad(w_in[:, :SSD_COLS], ((0, 0), (0, SSD_PCOLS - SSD_COLS)))
    i1 = RWKV_D
    i2 = i1 + DECAY_LORA
    i3 = i2 + RWKV_D
    i4 = i3 + RWKV_D
    i5 = i4 + AAA_LORA
    perm = lambda t: jnp.concatenate(
        [t[..., 0:i1], t[..., i2:i3], t[..., i3:i4], t[..., i1:i2], t[..., i4:i5], t[..., i5:]], axis=-1)
    w_rwkv = perm(w_in[:, SSD_COLS:])
    mu = perm(rwkv_mu).astype(F32).reshape(1, -1)
    w2p = jnp.concatenate([rwkv_w2, jnp.zeros((AAA_LORA, RWKV_D), rwkv_w2.dtype)], axis=0).astype(BF16)
    a2p = jnp.concatenate([jnp.zeros((DECAY_LORA, RWKV_D), rwkv_a2.dtype), rwkv_a2], axis=0).astype(BF16)

    p_ssd, p_rwkv = _inproj(h2, row(pre_mix_norm), w_ssd, w_rwkv, ssd_conv_w.astype(F32), row(ssd_conv_b), mu,
                            seq, tm=INPROJ_ROWS)
    y_ssd, y_rwkv = _mixers(
        p_ssd, p_rwkv, _pad_lanes(ssd_dt_bias), _pad_lanes(ssd_a_log), row(jnp.repeat(ssd_d, SSD_HEAD_DIM)),
        row(ssd_norm), row(rwkv_w0), w2p, row(rwkv_a0), a2p, rwkv_g2.astype(BF16), row(rwkv_k_k),
        row(rwkv_k_a), row(rwkv_r_k), row(rwkv_ln_w), row(rwkv_ln_b), batch, seq)
    return _ffn(y_ssd, y_rwkv, h2, w_out.astype(BF16), row(post_mix_norm), row(pre_ffn_norm),
                ffn_w_up.astype(BF16), ffn_conv_w.astype(F32), row(ffn_conv_b), ffn_w_down.astype(BF16),
                row(post_ffn_norm), batch, seq, tm=FFN_ROWS)


def kernel(x, pre_mix_norm, w_in, ssd_conv_w, ssd_conv_b, ssd_dt_bias, ssd_a_log, ssd_d, ssd_norm, rwkv_mu, rwkv_w0, rwkv_w2, rwkv_a0, rwkv_a2, rwkv_g2, rwkv_k_k, rwkv_k_a, rwkv_r_k, rwkv_ln_w, rwkv_ln_b, w_out, post_mix_norm, pre_ffn_norm, ffn_w_up, ffn_conv_w, ffn_conv_b, ffn_w_down, post_ffn_norm):
    batch, seq, d = x.shape
    h2 = x.reshape(batch * seq, d)
    params = (pre_mix_norm, w_in, ssd_conv_w, ssd_conv_b, ssd_dt_bias, ssd_a_log, ssd_d, ssd_norm, rwkv_mu,
              rwkv_w0, rwkv_w2, rwkv_a0, rwkv_a2, rwkv_g2, rwkv_k_k, rwkv_k_a, rwkv_r_k, rwkv_ln_w, rwkv_ln_b,
              w_out, post_mix_norm, pre_ffn_norm, ffn_w_up, ffn_conv_w, ffn_conv_b, ffn_w_down, post_ffn_norm)
    for l in range(pre_mix_norm.shape[0]):
        h2 = _layer(h2, batch, seq, *(t[l] for t in params))
    return h2.reshape(batch, seq, d)
```

```python
import functools

import jax
import jax.numpy as jnp
from jax import lax
from jax.experimental import pallas as pl
from jax.experimental.pallas import tpu as pltpu

F32 = jnp.float32
BF16 = jnp.bfloat16

D_MODEL = 1024
SSD_HEADS = 8
SSD_HEAD_DIM = 64
SSD_D = SSD_HEADS * SSD_HEAD_DIM
SSD_GROUPS = 2
SSD_STATE = 128
SSD_CONV = 4
SSD_CHUNK = 128
SSD_CONV_DIM = SSD_D + 2 * SSD_GROUPS * SSD_STATE
SSD_COLS = SSD_D + SSD_CONV_DIM + SSD_HEADS
RWKV_HEADS = 8
RWKV_HEAD_DIM = 64
RWKV_D = RWKV_HEADS * RWKV_HEAD_DIM
DECAY_LORA = 64
AAA_LORA = 64
GATE_LORA = 128
RWKV_COLS = 3 * RWKV_D + DECAY_LORA + AAA_LORA + GATE_LORA
RWKV_GN_EPS = 64e-5
D_FF = 2816
FFN_CONV = 3
NORM_EPS = 1e-6
LOG2_E = 1.4426950408889634
EXP_M_HALF = 0.6065306597126334

LANES = 128
SUBLANES = 8
SSD_PCOLS = SSD_D + SSD_CONV_DIM + LANES
SSD_BLOCK = 512
RWKV_CHUNK = 64
RWKV_BLOCK = 512
INPROJ_COLS = 256
RWKV_GROUP = 4
PAIR = 2 * RWKV_HEAD_DIM
FFN_COLS = 512
INPROJ_ROWS = 1024
FFN_ROWS = 1024
VMEM_LIMIT = 56 * 1024 * 1024

NN = (((1,), (0,)), ((), ()))
NT = (((1,), (1,)), ((), ()))
TN = (((0,), (0,)), ((), ()))


def _mm(a, b, dims=NN):
    return lax.dot_general(a, b, dims, preferred_element_type=F32)


def _dot1(a, b, dims=NN):
    return _mm(a.astype(BF16), b.astype(BF16), dims)


def _split3(a):
    a1 = a.astype(BF16)
    r1 = a - a1.astype(F32)
    a2 = r1.astype(BF16)
    a3 = (r1 - a2.astype(F32)).astype(BF16)
    return a1, a2, a3


def _dot_sel(sel_bf16, a):
    a1, a2, a3 = _split3(a)
    return _mm(sel_bf16, a1) + (_mm(sel_bf16, a2) + _mm(sel_bf16, a3))


def _a_dot_sel(a, sel_bf16):
    a1, a2, a3 = _split3(a)
    return _mm(a1, sel_bf16) + (_mm(a2, sel_bf16) + _mm(a3, sel_bf16))


def _dot_sel2(sel_bf16, a):
    hi = a.astype(BF16)
    lo = (a - hi.astype(F32)).astype(BF16)
    return _mm(sel_bf16, hi) + _mm(sel_bf16, lo)


def _shift_rows(x, tail, k):
    sub = _iota2((SUBLANES, x.shape[1]), 0)
    r = pltpu.roll(x, k, axis=0)
    head = jnp.where(sub < k, pltpu.roll(tail, k, axis=0), r[0:SUBLANES, :])
    return jnp.concatenate([head, r[SUBLANES:, :]], axis=0)


def _rms(x, g):
    return x * lax.rsqrt(jnp.mean(x * x, axis=-1, keepdims=True) + NORM_EPS) * g


def _sigmoid(x):
    return 0.5 + 0.5 * jnp.tanh(0.5 * x)


def _silu(x):
    h = 0.5 * x
    return h + h * jnp.tanh(h)


def _softplus(x):
    return jnp.maximum(x, 0.0) + jnp.log(1.0 + jnp.exp(-jnp.abs(x)))


def _iota2(shape, axis):
    return lax.broadcasted_iota(jnp.int32, shape, axis)


def _inproj_kernel(x_ref, g_ref, ws_ref, wr_ref, cw_ref, cb_ref, mu_ref, ps_ref, pr_ref, tail_s_ref, tail_r_ref,
                   *, tm, blocks_per_seq):
    i = pl.program_id(0)

    @pl.when(i == 0)
    def _():
        tail_s_ref[...] = jnp.zeros(tail_s_ref.shape, F32)
        tail_r_ref[...] = jnp.zeros(tail_r_ref.shape, F32)

    first = i % blocks_per_seq == 0
    xb = _rms(x_ref[...], g_ref[...]).astype(BF16)
    xbc0 = SSD_D
    dt0 = SSD_D + SSD_CONV_DIM
    ps_ref[:, 0:xbc0] = _mm(xb, ws_ref[:, 0:xbc0])
    ps_ref[:, dt0:SSD_PCOLS] = _mm(xb, ws_ref[:, dt0:SSD_PCOLS])
    for c0 in range(0, SSD_CONV_DIM, INPROJ_COLS):
        ts = slice(c0, c0 + INPROJ_COLS)
        cs = slice(xbc0 + c0, xbc0 + c0 + INPROJ_COLS)
        u = _mm(xb, ws_ref[:, cs])
        tail = jnp.where(first, 0.0, tail_s_ref[:, ts])
        acc = cb_ref[:, ts] + cw_ref[SSD_CONV - 1:SSD_CONV, ts] * u
        for k in range(1, SSD_CONV):
            acc = acc + cw_ref[SSD_CONV - 1 - k:SSD_CONV - k, ts] * _shift_rows(u, tail, k)
        tail_s_ref[:, ts] = u[tm - SUBLANES:tm, :]
        ps_ref[:, cs] = _silu(acc)
    for c0 in range(0, RWKV_COLS, INPROJ_COLS):
        cs = slice(c0, c0 + INPROJ_COLS)
        p = _mm(xb, wr_ref[:, cs])
        tail = jnp.where(first, 0.0, tail_r_ref[:, cs])
        prev = _shift_rows(p, tail, 1)
        tail_r_ref[:, cs] = p[tm - SUBLANES:tm, :]
        pr_ref[:, cs] = p + (prev - p) * mu_ref[:, cs]


def _inproj(x2, g, w_ssd, w_rwkv, cw, cb, mu, seq, tm):
    n = x2.shape[0]
    const = lambda i: (0, 0)
    return pl.pallas_call(
        functools.partial(_inproj_kernel, tm=tm, blocks_per_seq=seq // tm),
        name="inproj",
        grid=(n // tm,),
        in_specs=[
            pl.BlockSpec((tm, D_MODEL), lambda i: (i, 0)),
            pl.BlockSpec((1, D_MODEL), const),
            pl.BlockSpec((D_MODEL, SSD_PCOLS), const, pipeline_mode=pl.Buffered(1)),
            pl.BlockSpec((D_MODEL, RWKV_COLS), const, pipeline_mode=pl.Buffered(1)),
            pl.BlockSpec((SSD_CONV, SSD_CONV_DIM), const),
            pl.BlockSpec((1, SSD_CONV_DIM), const),
            pl.BlockSpec((1, RWKV_COLS), const),
        ],
        out_specs=[
            pl.BlockSpec((tm, SSD_PCOLS), lambda i: (i, 0)),
            pl.BlockSpec((tm, RWKV_COLS), lambda i: (i, 0)),
        ],
        out_shape=[
            jax.ShapeDtypeStruct((n, SSD_PCOLS), F32),
            jax.ShapeDtypeStruct((n, RWKV_COLS), F32),
        ],
        scratch_shapes=[
            pltpu.VMEM((SUBLANES, SSD_CONV_DIM), F32),
            pltpu.VMEM((SUBLANES, RWKV_COLS), F32),
        ],
        compiler_params=pltpu.CompilerParams(
            dimension_semantics=("arbitrary",), vmem_limit_bytes=VMEM_LIMIT),
    )(x2, g, w_ssd, w_rwkv, cw, cb, mu)


def _ssd_parts(p_ref, dtb_ref, alog_ref, dsk_ref, ng_ref, o_ref, st_ref):
    L = SSD_CHUNK
    gw = SSD_D // SSD_GROUPS

    @pl.when(pl.program_id(1) == 0)
    def _():
        st_ref[...] = jnp.zeros(st_ref.shape, F32)

    row = _iota2((L, L), 0)
    col = _iota2((L, L), 1)
    causal = row >= col
    tril = jnp.where(causal, 1.0, 0.0).astype(BF16)
    hsel = (_iota2((LANES, SSD_D), 1) // SSD_HEAD_DIM == _iota2((LANES, SSD_D), 0))
    hsel = jnp.where(hsel, 1.0, 0.0).astype(BF16)
    lane_lo = _iota2((L, LANES), 1) < SSD_HEAD_DIM
    xbc_cols = slice(SSD_D, SSD_D + SSD_CONV_DIM)

    def chunk(c):
        rs = slice(c * L, (c + 1) * L)
        z = p_ref[rs, 0:SSD_D]
        xbc = p_ref[rs, xbc_cols]
        dt_raw = p_ref[rs, SSD_D + SSD_CONV_DIM:SSD_PCOLS]
        xs = xbc[:, 0:SSD_D]

        dt = _softplus(dt_raw + dtb_ref[...])
        a = dt * (-LOG2_E * jnp.exp(alog_ref[...]))
        a_cs = _dot_sel(tril, a)
        a_cs_t = a_cs.T
        dt_e = _mm(dt.astype(BF16), hsel)
        acs_e = _a_dot_sel(a_cs, hsel)
        ea_e = jnp.exp2(acs_e)
        ds_e = jnp.exp2(acs_e[L - 1:L, :] - acs_e)

        x_dt = xs * dt_e
        x_b = x_dt.astype(BF16)
        x_dec = (x_dt * ds_e).astype(BF16)

        for g in range(SSD_GROUPS):
            gs = slice(g * gw, (g + 1) * gw)
            b_g = xbc[:, SSD_D + g * SSD_STATE:SSD_D + (g + 1) * SSD_STATE].astype(BF16)
            c_off = SSD_D + SSD_GROUPS * SSD_STATE
            c_g = xbc[:, c_off + g * SSD_STATE:c_off + (g + 1) * SSD_STATE].astype(BF16)
            scores = _mm(c_g, b_g, NT)
            state = st_ref[g]
            y_off = _mm(c_g, state.astype(BF16)) * ea_e[:, gs]
            st_ref[g] = state * ea_e[L - 1:L, gs] + _mm(b_g, x_dec[:, gs], TN)
            y_parts = []
            for j in range(gw // LANES):
                h0 = g * (SSD_HEADS // SSD_GROUPS) + 2 * j
                ms = []
                for h in (h0, h0 + 1):
                    seg = a_cs[:, h:h + 1] - a_cs_t[h:h + 1, :]
                    dec = jnp.exp2(jnp.where(causal, seg, -jnp.inf))
                    ms.append((scores * dec).astype(BF16))
                xp = x_b[:, h0 * SSD_HEAD_DIM:h0 * SSD_HEAD_DIM + LANES]
                zero = jnp.zeros_like(xp)
                x_bd = jnp.concatenate([jnp.where(lane_lo, xp, zero), jnp.where(lane_lo, zero, xp)], axis=0)
                y_parts.append(_mm(jnp.concatenate(ms, axis=1), x_bd))
            y = jnp.concatenate(y_parts, axis=1) + y_off
            y = y + dsk_ref[:, gs] * xs[:, gs]
            y = y * _silu(z[:, gs])
            y = y * lax.rsqrt(jnp.mean(y * y, axis=-1, keepdims=True) + NORM_EPS)
            o_ref[rs, gs] = (y * ng_ref[:, gs]).astype(o_ref.dtype)
    return chunk


def _seg_sum(x, lane_lo):
    s_lo = jnp.sum(jnp.where(lane_lo, x, 0.0), axis=-1, keepdims=True)
    s_hi = jnp.sum(jnp.where(lane_lo, 0.0, x), axis=-1, keepdims=True)
    return jnp.where(lane_lo, s_lo, s_hi)


def _stack_heads(x, lane_lo):
    zero = jnp.zeros_like(x)
    return jnp.concatenate([jnp.where(lane_lo, x, zero), jnp.where(lane_lo, zero, x)], axis=0)


def _unit_lower_inverse_many(a_list, row, col, eye):
    blk8 = row // 8 == col // 8
    a8f = [jnp.where(blk8, a, 0.0) for a in a_list]
    a8 = [x.astype(BF16) for x in a8f]
    t = [eye + x for x in a8f]
    a2 = [_mm(x, x).astype(BF16) for x in a8]
    t = [ti + _mm(ti.astype(BF16), x) for ti, x in zip(t, a2)]
    a4 = [_mm(x, x).astype(BF16) for x in a2]
    t = [ti + _mm(ti.astype(BF16), x) for ti, x in zip(t, a4)]
    n = a_list[0].shape[0]
    for s in (8, 16, 32):
        lower_left = (row // (2 * s) == col // (2 * s)) & ((row // s) % 2 == 1) & ((col // s) % 2 == 0)
        second = [slice(r0 + s, r0 + 2 * s) for r0 in range(0, n, 2 * s)]
        tb = [ti.astype(BF16) for ti in t]
        t2 = [jnp.concatenate([ti[r, :] for r in second], axis=0) for ti in t]
        x = [_mm(t2i.astype(BF16), jnp.where(lower_left, a, 0.0).astype(BF16)).astype(BF16)
             for t2i, a in zip(t2, a_list)]
        t2 = [t2i + _mm(xi, tbi) for t2i, xi, tbi in zip(t2, x, tb)]
        t = [jnp.concatenate([piece for m, r in enumerate(second)
                              for piece in (ti[r.start - s:r.start, :], t2i[m * s:(m + 1) * s, :])], axis=0)
             for ti, t2i in zip(t, t2)]
    return t


def _rwkv_parts(p_ref, w0_ref, w2_ref, a0_ref, a2_ref, g2_ref, kk_ref, ka_ref, rk_ref,
                lnw_ref, lnb_ref, o_ref, st_ref, pre_ref):
    L = RWKV_CHUNK
    TB = RWKV_BLOCK
    H2 = 2 * L
    D = RWKV_D
    n_pairs = D // PAIR

    @pl.when(pl.program_id(1) == 0)
    def _():
        st_ref[...] = jnp.zeros(st_ref.shape, F32)

    k = p_ref[:, D:2 * D]
    wa = p_ref[:, 3 * D:3 * D + LANES]
    g_lo = p_ref[:, 3 * D + LANES:3 * D + 2 * LANES]
    lw = -EXP_M_HALF * _sigmoid(w0_ref[...] + _dot1(jnp.tanh(wa), w2_ref[...]))
    alr = _sigmoid(a0_ref[...] + _dot1(wa, a2_ref[...]))
    CB = 4 * L
    blk_tril = (_iota2((CB, CB), 0) >= _iota2((CB, CB), 1)) & (_iota2((CB, CB), 0) // L == _iota2((CB, CB), 1) // L)
    blk_tril = jnp.where(blk_tril, 1.0, 0.0).astype(BF16)
    K_, KK_, ALR_, LW_, CS_, G_ = range(6)
    v_cols = lambda sl: slice(2 * D + sl.start, 2 * D + sl.stop)
    pre_ref[K_] = k * (1.0 + (alr - 1.0) * ka_ref[...])
    pre_ref[KK_] = k * kk_ref[...]
    pre_ref[ALR_] = alr
    pre_ref[LW_] = lw
    for r0 in range(0, TB, CB):
        pre_ref[CS_, r0:r0 + CB, :] = _dot_sel2(blk_tril, lw[r0:r0 + CB, :])
    pre_ref[G_] = _dot1(_sigmoid(g_lo), g2_ref[...])

    lane_lo = _iota2((L, PAIR), 1) < RWKV_HEAD_DIM
    row = _iota2((H2, H2), 0)
    col = _iota2((H2, H2), 1)
    eye = jnp.where(row == col, 1.0, 0.0)
    same_head = row // L == col // L
    strict = same_head & (row % L > col % L)
    incl = same_head & (row % L >= col % L)
    strict_incl = jnp.concatenate([strict, incl], axis=0)

    def independent_part(chunks):
        lhs_a, v_t, a_ab, a_k, lr_arb, hat_bk, p_all = ([] for _ in range(7))
        for c, q in [(c, q) for c in chunks for q in range(n_pairs)]:
            rs = slice(c * L, (c + 1) * L)
            sl = slice(q * PAIR, (q + 1) * PAIR)
            cs_p = pre_ref[CS_, rs, sl]
            cs_last = pre_ref[CS_, (c + 1) * L - 1:(c + 1) * L, sl]
            p_inv = jnp.exp(-cs_p)
            p_end = jnp.exp(cs_last - cs_p)
            kk_p = pre_ref[KK_, rs, sl]
            kk_n = kk_p * lax.rsqrt(jnp.maximum(_seg_sum(kk_p * kk_p, lane_lo), 1e-24))
            b_p = kk_n * pre_ref[ALR_, rs, sl]
            k_p = pre_ref[K_, rs, sl]
            la = _stack_heads(-kk_n * jnp.exp(cs_p - pre_ref[LW_, rs, sl]), lane_lo).astype(BF16)
            lr = _stack_heads(p_ref[rs, sl] * jnp.exp(cs_p), lane_lo).astype(BF16)
            bt = (b_p * p_inv).astype(BF16)
            kt = (k_p * p_inv).astype(BF16)
            g = _mm(jnp.concatenate([la, lr], axis=0), jnp.concatenate([bt, bt, kt, kt], axis=0), NT)
            a_ab.append(jnp.where(strict, g[0:H2, 0:H2], 0.0))
            a_k.append(jnp.where(strict_incl, g[:, H2:], 0.0).astype(BF16))
            lr_arb.append(jnp.concatenate([lr, jnp.where(incl, g[H2:, 0:H2], 0.0).astype(BF16)], axis=1))
            lhs_a.append(la)
            hat_bk.append(jnp.concatenate([_stack_heads(b_p * p_end, lane_lo), _stack_heads(k_p * p_end, lane_lo)],
                                          axis=0).astype(BF16))
            p_all.append(jnp.exp(cs_last))
            v_t.append(_stack_heads(p_ref[rs, v_cols(sl)], lane_lo).T.astype(BF16))
        t_inv = [t.astype(BF16) for t in _unit_lower_inverse_many(a_ab, row, col, eye)]
        t_a = [_mm(t, la).astype(BF16) for t, la in zip(t_inv, lhs_a)]
        both = [_mm(vt, a, NT) for vt, a in zip(v_t, a_k)]
        av_t = [x[:, 0:H2].astype(BF16) for x in both]
        arkv_t = [x[:, H2:] for x in both]
        tav_t = [_mm(x, t, NT) for x, t in zip(av_t, t_inv)]
        return dict(t_a=t_a, tav_t=tav_t, arkv_t=arkv_t, lr_arb=lr_arb, hat_bk=hat_bk, v_t=v_t, p_all=p_all)

    def state_part(chunks, d):
        for ci, c in enumerate(chunks):
            rs = slice(c * L, (c + 1) * L)
            idx = [ci * n_pairs + q for q in range(n_pairs)]
            s0 = [st_ref[q] for q in range(n_pairs)]
            s0b = [s.astype(BF16) for s in s0]
            u_tb = [(_mm(s0b[q], d["t_a"][i], NT) + d["tav_t"][i]).astype(BF16) for q, i in enumerate(idx)]
            for q, i in enumerate(idx):
                st_ref[q] = s0[q] * d["p_all"][i] + _mm(jnp.concatenate([u_tb[q], d["v_t"][i]], axis=1),
                                                        d["hat_bk"][i])
            y_t = [_mm(jnp.concatenate([s0b[q], u_tb[q]], axis=1), d["lr_arb"][i], NT) + d["arkv_t"][i]
                   for q, i in enumerate(idx)]
            for q in range(n_pairs):
                sl = slice(q * PAIR, (q + 1) * PAIR)
                y_st = y_t[q].T
                y = y_st[0:L, :] + y_st[L:H2, :]
                mean = _seg_sum(y, lane_lo) * (1.0 / RWKV_HEAD_DIM)
                dev = y - mean
                var = _seg_sum(dev * dev, lane_lo) * (1.0 / RWKV_HEAD_DIM)
                yn = dev * lax.rsqrt(var + RWKV_GN_EPS) * lnw_ref[:, sl] + lnb_ref[:, sl]
                v_p = p_ref[rs, v_cols(sl)]
                bonus = _seg_sum(p_ref[rs, sl] * pre_ref[K_, rs, sl] * rk_ref[:, sl], lane_lo) * v_p
                o_ref[rs, sl] = ((yn + bonus) * pre_ref[G_, rs, sl]).astype(o_ref.dtype)

    groups = [list(range(c0, c0 + RWKV_GROUP)) for c0 in range(0, TB // L, RWKV_GROUP)]
    return groups, independent_part, state_part


def _mixers_kernel(ps_ref, pr_ref, dtb_ref, alog_ref, dsk_ref, ng_ref,
                   w0_ref, w2_ref, a0_ref, a2_ref, g2_ref, kk_ref, ka_ref, rk_ref, lnw_ref, lnb_ref,
                   os_ref, or_ref, sst_ref, rst_ref, pre_ref):
    ssd_chunk = _ssd_parts(ps_ref, dtb_ref, alog_ref, dsk_ref, ng_ref, os_ref, sst_ref)
    groups, independent_part, state_part = _rwkv_parts(
        pr_ref, w0_ref, w2_ref, a0_ref, a2_ref, g2_ref, kk_ref, ka_ref, rk_ref, lnw_ref, lnb_ref,
        or_ref, rst_ref, pre_ref)
    ssd_chunks = iter(range(SSD_BLOCK // SSD_CHUNK))
    per_group = (SSD_BLOCK // SSD_CHUNK) // len(groups)
    ready = []
    for g in groups:
        ready.append(independent_part(g))
        for _ in range(per_group):
            ssd_chunk(next(ssd_chunks))
    for c in ssd_chunks:
        ssd_chunk(c)
    for g, d in zip(groups, ready):
        state_part(g, d)


def _mixers(p_ssd, p_rwkv, dtb, alog, dsk, ng, w0, w2p, a0, a2p, g2, k_k, k_a, r_k, ln_w, ln_b, batch, seq):
    assert SSD_BLOCK == RWKV_BLOCK
    TB = RWKV_BLOCK
    nb = seq // TB
    const = lambda b, c: (0, 0)
    rows = lambda b, c: (b * nb + c, 0)
    vec = pl.BlockSpec((1, RWKV_D), const)
    return pl.pallas_call(
        _mixers_kernel,
        name="mixers",
        grid=(batch, nb),
        in_specs=[
            pl.BlockSpec((TB, SSD_PCOLS), rows),
            pl.BlockSpec((TB, RWKV_COLS), rows),
            pl.BlockSpec((1, LANES), const),
            pl.BlockSpec((1, LANES), const),
            pl.BlockSpec((1, SSD_D), const),
            pl.BlockSpec((1, SSD_D), const),
            vec,
            pl.BlockSpec((LANES, RWKV_D), const),
            vec,
            pl.BlockSpec((LANES, RWKV_D), const),
            pl.BlockSpec((GATE_LORA, RWKV_D), const),
            vec, vec, vec, vec, vec,
        ],
        out_specs=[pl.BlockSpec((TB, SSD_D), rows), pl.BlockSpec((TB, RWKV_D), rows)],
        out_shape=[jax.ShapeDtypeStruct((batch * seq, SSD_D), BF16),
                   jax.ShapeDtypeStruct((batch * seq, RWKV_D), BF16)],
        scratch_shapes=[
            pltpu.VMEM((SSD_GROUPS, SSD_STATE, SSD_D // SSD_GROUPS), F32),
            pltpu.VMEM((RWKV_D // PAIR, PAIR, PAIR), F32),
            pltpu.VMEM((6, TB, RWKV_D), F32),
        ],
        compiler_params=pltpu.CompilerParams(
            dimension_semantics=("arbitrary", "arbitrary"), vmem_limit_bytes=VMEM_LIMIT),
    )(p_ssd, p_rwkv, dtb, alog, dsk, ng, w0, w2p, a0, a2p, g2, k_k, k_a, r_k, ln_w, ln_b)


def _ffn_kernel(ys_ref, yr_ref, x_ref, wo_ref, g1_ref, g2_ref, wup_ref, cw_ref, cb_ref, wdn_ref, g3_ref,
                o_ref, ubuf_ref, *, tm):
    @pl.when(pl.program_id(1) == 0)
    def _():
        ubuf_ref[...] = jnp.zeros(ubuf_ref.shape, F32)

    mix = _mm(ys_ref[...], wo_ref[0:SSD_D, :]) + _mm(yr_ref[...], wo_ref[SSD_D:, :])
    h = x_ref[...] + _rms(mix, g1_ref[...])
    hn = _rms(h, g2_ref[...]).astype(BF16)

    chunks = [(c0, min(FFN_COLS, D_FF - c0)) for c0 in range(0, D_FF, FFN_COLS)]

    def up(j):
        c0, width = chunks[j]
        return [_mm(hn, wup_ref[:, c:c + width]) for c in (c0, D_FF + c0)]

    def conv(u, c0):
        cs = slice(c0, c0 + u.shape[1])
        tail = ubuf_ref[:, cs]
        out = cb_ref[:, cs] + cw_ref[FFN_CONV - 1:FFN_CONV, cs] * u
        for k in range(1, FFN_CONV):
            out = out + cw_ref[FFN_CONV - 1 - k:FFN_CONV - k, cs] * _shift_rows(u, tail, k)
        ubuf_ref[:, cs] = u[tm - SUBLANES:tm, :]
        return out

    n_steps = len(chunks)
    u_next = up(0)
    acts = []
    for j in range(n_steps):
        u_gate, u_val = u_next
        if j + 1 < n_steps:
            u_next = up(j + 1)
        gate = conv(u_gate, chunks[j][0])
        val = conv(u_val, D_FF + chunks[j][0])
        acts.append((_silu(gate) * val).astype(BF16))
    f = _mm(jnp.concatenate(acts, axis=1), wdn_ref[...])
    o_ref[...] = h + _rms(f, g3_ref[...])


def _ffn(ys, yr, x2, wo, g1, g2, wup, cw, cb, wdn, g3, batch, seq, tm):
    nb = seq // tm
    const = lambda b, i: (0, 0)
    rows = lambda b, i: (b * nb + i, 0)
    res = functools.partial(pl.BlockSpec, index_map=const, pipeline_mode=pl.Buffered(1))
    return pl.pallas_call(
        functools.partial(_ffn_kernel, tm=tm),
        name="outproj_ffn",
        grid=(batch, nb),
        in_specs=[
            pl.BlockSpec((tm, SSD_D), rows),
            pl.BlockSpec((tm, RWKV_D), rows),
            pl.BlockSpec((tm, D_MODEL), rows),
            res((SSD_D + RWKV_D, D_MODEL)),
            pl.BlockSpec((1, D_MODEL), const),
            pl.BlockSpec((1, D_MODEL), const),
            res((D_MODEL, 2 * D_FF)),
            pl.BlockSpec((FFN_CONV, 2 * D_FF), const),
            pl.BlockSpec((1, 2 * D_FF), const),
            res((D_FF, D_MODEL)),
            pl.BlockSpec((1, D_MODEL), const),
        ],
        out_specs=pl.BlockSpec((tm, D_MODEL), rows),
        out_shape=jax.ShapeDtypeStruct((batch * seq, D_MODEL), F32),
        scratch_shapes=[
            pltpu.VMEM((SUBLANES, 2 * D_FF), F32),
        ],
        compiler_params=pltpu.CompilerParams(
            dimension_semantics=("arbitrary", "arbitrary"), vmem_limit_bytes=VMEM_LIMIT),
    )(ys, yr, x2, wo, g1, g2, wup, cw, cb, wdn, g3)


def _pad_lanes(v):
    return jnp.pad(v.astype(F32), (0, LANES - v.shape[0]))[None, :]


def _layer(h2, batch, seq, pre_mix_norm, w_in, ssd_conv_w, ssd_conv_b, ssd_dt_bias, ssd_a_log, ssd_d, ssd_norm,
           rwkv_mu, rwkv_w0, rwkv_w2, rwkv_a0, rwkv_a2, rwkv_g2, rwkv_k_k, rwkv_k_a, rwkv_r_k,
           rwkv_ln_w, rwkv_ln_b, w_out, post_mix_norm, pre_ffn_norm, ffn_w_up, ffn_conv_w,
           ffn_conv_b, ffn_w_down, post_ffn_norm):
    row = lambda v: v.astype(F32).reshape(1, -1)

    w_in = w_in.astype(BF16)
    w_ssd = jnp.pad(w_in[:, :SSD_COLS], ((0, 0), (0, SSD_PCOLS - SSD_COLS)))
    i1 = RWKV_D
    i2 = i1 + DECAY_LORA
    i3 = i2 + RWKV_D
    i4 = i3 + RWKV_D
    i5 = i4 + AAA_LORA
    perm = lambda t: jnp.concatenate(
        [t[..., 0:i1], t[..., i2:i3], t[..., i3:i4], t[..., i1:i2], t[..., i4:i5], t[..., i5:]], axis=-1)
    w_rwkv = perm(w_in[:, SSD_COLS:])
    mu = perm(rwkv_mu).astype(F32).reshape(1, -1)
    w2p = jnp.concatenate([rwkv_w2, jnp.zeros((AAA_LORA, RWKV_D), rwkv_w2.dtype)], axis=0).astype(BF16)
    a2p = jnp.concatenate([jnp.zeros((DECAY_LORA, RWKV_D), rwkv_a2.dtype), rwkv_a2], axis=0).astype(BF16)

    p_ssd, p_rwkv = _inproj(h2, row(pre_mix_norm), w_ssd, w_rwkv, ssd_conv_w.astype(F32), row(ssd_conv_b), mu,
                            seq, tm=INPROJ_ROWS)
    y_ssd, y_rwkv = _mixers(
        p_ssd, p_rwkv, _pad_lanes(ssd_dt_bias), _pad_lanes(ssd_a_log), row(jnp.repeat(ssd_d, SSD_HEAD_DIM)),
        row(ssd_norm), row(rwkv_w0), w2p, row(rwkv_a0), a2p, rwkv_g2.astype(BF16), row(rwkv_k_k),
        row(rwkv_k_a), row(rwkv_r_k), row(rwkv_ln_w), row(rwkv_ln_b), batch, seq)
    return _ffn(y_ssd, y_rwkv, h2, w_out.astype(BF16), row(post_mix_norm), row(pre_ffn_norm),
                ffn_w_up.astype(BF16), ffn_conv_w.astype(F32), row(ffn_conv_b), ffn_w_down.astype(BF16),
                row(post_ffn_norm), batch, seq, tm=FFN_ROWS)


def kernel(x, pre_mix_norm, w_in, ssd_conv_w, ssd_conv_b, ssd_dt_bias, ssd_a_log, ssd_d, ssd_norm, rwkv_mu, rwkv_w0, rwkv_w2, rwkv_a0, rwkv_a2, rwkv_g2, rwkv_k_k, rwkv_k_a, rwkv_r_k, rwkv_ln_w, rwkv_ln_b, w_out, post_mix_norm, pre_ffn_norm, ffn_w_up, ffn_conv_w, ffn_conv_b, ffn_w_down, post_ffn_norm):
    batch, seq, d = x.shape
    h2 = x.reshape(batch * seq, d)
    params = (pre_mix_norm, w_in, ssd_conv_w, ssd_conv_b, ssd_dt_bias, ssd_a_log, ssd_d, ssd_norm, rwkv_mu,
              rwkv_w0, rwkv_w2, rwkv_a0, rwkv_a2, rwkv_g2, rwkv_k_k, rwkv_k_a, rwkv_r_k, rwkv_ln_w, rwkv_ln_b,
              w_out, post_mix_norm, pre_ffn_norm, ffn_w_up, ffn_conv_w, ffn_conv_b, ffn_w_down, post_ffn_norm)
    for l in range(pre_mix_norm.shape[0]):
        h2 = _layer(h2, batch, seq, *(t[l] for t in params))
    return h2.reshape(batch, seq, d)
```

```python
import functools

import jax
import jax.numpy as jnp
from jax import lax
from jax.experimental import pallas as pl
from jax.experimental.pallas import tpu as pltpu

F32 = jnp.float32
BF16 = jnp.bfloat16

D_MODEL = 1024
SSD_HEADS = 8
SSD_HEAD_DIM = 64
SSD_D = SSD_HEADS * SSD_HEAD_DIM
SSD_GROUPS = 2
SSD_STATE = 128
SSD_CONV = 4
SSD_CHUNK = 128
SSD_CONV_DIM = SSD_D + 2 * SSD_GROUPS * SSD_STATE
SSD_COLS = SSD_D + SSD_CONV_DIM + SSD_HEADS
RWKV_HEADS = 8
RWKV_HEAD_DIM = 64
RWKV_D = RWKV_HEADS * RWKV_HEAD_DIM
DECAY_LORA = 64
AAA_LORA = 64
GATE_LORA = 128
RWKV_COLS = 3 * RWKV_D + DECAY_LORA + AAA_LORA + GATE_LORA
RWKV_GN_EPS = 64e-5
D_FF = 2816
FFN_CONV = 3
NORM_EPS = 1e-6
LOG2_E = 1.4426950408889634
EXP_M_HALF = 0.6065306597126334

LANES = 128
SUBLANES = 8
SSD_PCOLS = SSD_D + SSD_CONV_DIM + LANES
SSD_BLOCK = 512
RWKV_CHUNK = 64
RWKV_BLOCK = 512
INPROJ_COLS = 256
RWKV_GROUP = 4
PAIR = 2 * RWKV_HEAD_DIM
FFN_COLS = 768
INPROJ_ROWS = 1024
FFN_ROWS = 1024
VMEM_LIMIT = 56 * 1024 * 1024

NN = (((1,), (0,)), ((), ()))
NT = (((1,), (1,)), ((), ()))
TN = (((0,), (0,)), ((), ()))


def _mm(a, b, dims=NN):
    return lax.dot_general(a, b, dims, preferred_element_type=F32)


def _dot1(a, b, dims=NN):
    return _mm(a.astype(BF16), b.astype(BF16), dims)


def _split3(a):
    a1 = a.astype(BF16)
    r1 = a - a1.astype(F32)
    a2 = r1.astype(BF16)
    a3 = (r1 - a2.astype(F32)).astype(BF16)
    return a1, a2, a3


def _dot_sel(sel_bf16, a):
    a1, a2, a3 = _split3(a)
    return _mm(sel_bf16, a1) + (_mm(sel_bf16, a2) + _mm(sel_bf16, a3))


def _a_dot_sel(a, sel_bf16):
    a1, a2, a3 = _split3(a)
    return _mm(a1, sel_bf16) + (_mm(a2, sel_bf16) + _mm(a3, sel_bf16))


def _dot_sel2(sel_bf16, a):
    hi = a.astype(BF16)
    lo = (a - hi.astype(F32)).astype(BF16)
    return _mm(sel_bf16, hi) + _mm(sel_bf16, lo)


def _shift_rows(x, tail, k):
    sub = _iota2((SUBLANES, x.shape[1]), 0)
    r = pltpu.roll(x, k, axis=0)
    head = jnp.where(sub < k, pltpu.roll(tail, k, axis=0), r[0:SUBLANES, :])
    return jnp.concatenate([head, r[SUBLANES:, :]], axis=0)


def _rms(x, g):
    return x * lax.rsqrt(jnp.mean(x * x, axis=-1, keepdims=True) + NORM_EPS) * g


def _sigmoid(x):
    return 0.5 + 0.5 * jnp.tanh(0.5 * x)


def _silu(x):
    h = 0.5 * x
    return h + h * jnp.tanh(h)


def _softplus(x):
    return jnp.maximum(x, 0.0) + jnp.log(1.0 + jnp.exp(-jnp.abs(x)))


def _iota2(shape, axis):
    return lax.broadcasted_iota(jnp.int32, shape, axis)


def _inproj_kernel(x_ref, g_ref, ws_ref, wr_ref, cw_ref, cb_ref, mu_ref, ps_ref, pr_ref, tail_s_ref, tail_r_ref,
                   *, tm, blocks_per_seq):
    i = pl.program_id(0)

    @pl.when(i == 0)
    def _():
        tail_s_ref[...] = jnp.zeros(tail_s_ref.shape, F32)
        tail_r_ref[...] = jnp.zeros(tail_r_ref.shape, F32)

    first = i % blocks_per_seq == 0
    xb = _rms(x_ref[...], g_ref[...]).astype(BF16)
    xbc0 = SSD_D
    dt0 = SSD_D + SSD_CONV_DIM
    def conv_cols(c0):
        ts = slice(c0, c0 + INPROJ_COLS)
        cs = slice(xbc0 + c0, xbc0 + c0 + INPROJ_COLS)
        u = _mm(xb, ws_ref[:, cs])
        tail = jnp.where(first, 0.0, tail_s_ref[:, ts])
        acc = cb_ref[:, ts] + cw_ref[SSD_CONV - 1:SSD_CONV, ts] * u
        for k in range(1, SSD_CONV):
            acc = acc + cw_ref[SSD_CONV - 1 - k:SSD_CONV - k, ts] * _shift_rows(u, tail, k)
        tail_s_ref[:, ts] = u[tm - SUBLANES:tm, :]
        ps_ref[:, cs] = _silu(acc)

    def shift_cols(c0):
        cs = slice(c0, c0 + INPROJ_COLS)
        p = _mm(xb, wr_ref[:, cs])
        tail = jnp.where(first, 0.0, tail_r_ref[:, cs])
        prev = _shift_rows(p, tail, 1)
        tail_r_ref[:, cs] = p[tm - SUBLANES:tm, :]
        pr_ref[:, cs] = p + (prev - p) * mu_ref[:, cs]

    def plain_cols(lo, hi):
        ps_ref[:, lo:hi] = _mm(xb, ws_ref[:, lo:hi])

    conv = [functools.partial(conv_cols, c0) for c0 in range(0, SSD_CONV_DIM, INPROJ_COLS)]
    light = ([functools.partial(plain_cols, c0, c0 + INPROJ_COLS) for c0 in range(0, xbc0, INPROJ_COLS)]
             + [functools.partial(plain_cols, dt0, SSD_PCOLS)]
             + [functools.partial(shift_cols, c0) for c0 in range(0, RWKV_COLS, INPROJ_COLS)])
    per_conv = 2
    for j, step in enumerate(conv):
        step()
        for fill in light[j * per_conv:(j + 1) * per_conv]:
            fill()
    for fill in light[len(conv) * per_conv:]:
        fill()


def _inproj(x2, g, w_ssd, w_rwkv, cw, cb, mu, seq, tm):
    n = x2.shape[0]
    const = lambda i: (0, 0)
    return pl.pallas_call(
        functools.partial(_inproj_kernel, tm=tm, blocks_per_seq=seq // tm),
        name="inproj",
        grid=(n // tm,),
        in_specs=[
            pl.BlockSpec((tm, D_MODEL), lambda i: (i, 0)),
            pl.BlockSpec((1, D_MODEL), const),
            pl.BlockSpec((D_MODEL, SSD_PCOLS), const, pipeline_mode=pl.Buffered(1)),
            pl.BlockSpec((D_MODEL, RWKV_COLS), const, pipeline_mode=pl.Buffered(1)),
            pl.BlockSpec((SSD_CONV, SSD_CONV_DIM), const),
            pl.BlockSpec((1, SSD_CONV_DIM), const),
            pl.BlockSpec((1, RWKV_COLS), const),
        ],
        out_specs=[
            pl.BlockSpec((tm, SSD_PCOLS), lambda i: (i, 0)),
            pl.BlockSpec((tm, RWKV_COLS), lambda i: (i, 0)),
        ],
        out_shape=[
            jax.ShapeDtypeStruct((n, SSD_PCOLS), F32),
            jax.ShapeDtypeStruct((n, RWKV_COLS), F32),
        ],
        scratch_shapes=[
            pltpu.VMEM((SUBLANES, SSD_CONV_DIM), F32),
            pltpu.VMEM((SUBLANES, RWKV_COLS), F32),
        ],
        compiler_params=pltpu.CompilerParams(
            dimension_semantics=("arbitrary",), vmem_limit_bytes=VMEM_LIMIT),
    )(x2, g, w_ssd, w_rwkv, cw, cb, mu)


def _ssd_parts(p_ref, dtb_ref, alog_ref, dsk_ref, ng_ref, o_ref, st_ref):
    L = SSD_CHUNK
    gw = SSD_D // SSD_GROUPS

    @pl.when(pl.program_id(1) == 0)
    def _():
        st_ref[...] = jnp.zeros(st_ref.shape, F32)

    row = _iota2((L, L), 0)
    col = _iota2((L, L), 1)
    causal = row >= col
    tril = jnp.where(causal, 1.0, 0.0).astype(BF16)
    hsel = (_iota2((LANES, SSD_D), 1) // SSD_HEAD_DIM == _iota2((LANES, SSD_D), 0))
    hsel = jnp.where(hsel, 1.0, 0.0).astype(BF16)
    lane_lo = _iota2((L, LANES), 1) < SSD_HEAD_DIM
    xbc_cols = slice(SSD_D, SSD_D + SSD_CONV_DIM)

    def chunk(c):
        rs = slice(c * L, (c + 1) * L)
        z = p_ref[rs, 0:SSD_D]
        xbc = p_ref[rs, xbc_cols]
        dt_raw = p_ref[rs, SSD_D + SSD_CONV_DIM:SSD_PCOLS]
        xs = xbc[:, 0:SSD_D]

        dt = _softplus(dt_raw + dtb_ref[...])
        a = dt * (-LOG2_E * jnp.exp(alog_ref[...]))
        a_cs = _dot_sel(tril, a)
        a_cs_t = a_cs.T
        dt_e = _mm(dt.astype(BF16), hsel)
        acs_e = _a_dot_sel(a_cs, hsel)
        ea_e = jnp.exp2(acs_e)
        ds_e = jnp.exp2(acs_e[L - 1:L, :] - acs_e)

        x_dt = xs * dt_e
        x_b = x_dt.astype(BF16)
        x_dec = (x_dt * ds_e).astype(BF16)

        for g in range(SSD_GROUPS):
            gs = slice(g * gw, (g + 1) * gw)
            b_g = xbc[:, SSD_D + g * SSD_STATE:SSD_D + (g + 1) * SSD_STATE].astype(BF16)
            c_off = SSD_D + SSD_GROUPS * SSD_STATE
            c_g = xbc[:, c_off + g * SSD_STATE:c_off + (g + 1) * SSD_STATE].astype(BF16)
            scores = _mm(c_g, b_g, NT)
            state = st_ref[g]
            y_off = _mm(c_g, state.astype(BF16)) * ea_e[:, gs]
            st_ref[g] = state * ea_e[L - 1:L, gs] + _mm(b_g, x_dec[:, gs], TN)
            y_parts = []
            for j in range(gw // LANES):
                h0 = g * (SSD_HEADS // SSD_GROUPS) + 2 * j
                ms = []
                for h in (h0, h0 + 1):
                    seg = a_cs[:, h:h + 1] - a_cs_t[h:h + 1, :]
                    dec = jnp.exp2(jnp.where(causal, seg, -jnp.inf))
                    ms.append((scores * dec).astype(BF16))
                xp = x_b[:, h0 * SSD_HEAD_DIM:h0 * SSD_HEAD_DIM + LANES]
                zero = jnp.zeros_like(xp)
                x_bd = jnp.concatenate([jnp.where(lane_lo, xp, zero), jnp.where(lane_lo, zero, xp)], axis=0)
                y_parts.append(_mm(jnp.concatenate(ms, axis=1), x_bd))
            y = jnp.concatenate(y_parts, axis=1) + y_off
            y = y + dsk_ref[:, gs] * xs[:, gs]
            y = y * _silu(z[:, gs])
            y = y * lax.rsqrt(jnp.mean(y * y, axis=-1, keepdims=True) + NORM_EPS)
            o_ref[rs, gs] = (y * ng_ref[:, gs]).astype(o_ref.dtype)
    return chunk


def _seg_sum(x, lane_lo):
    s_lo = jnp.sum(jnp.where(lane_lo, x, 0.0), axis=-1, keepdims=True)
    s_hi = jnp.sum(jnp.where(lane_lo, 0.0, x), axis=-1, keepdims=True)
    return jnp.where(lane_lo, s_lo, s_hi)


def _stack_heads(x, lane_lo):
    zero = jnp.zeros_like(x)
    return jnp.concatenate([jnp.where(lane_lo, x, zero), jnp.where(lane_lo, zero, x)], axis=0)


def _unit_lower_inverse_many(a_list, row, col, eye):
    blk8 = row // 8 == col // 8
    a8f = [jnp.where(blk8, a, 0.0) for a in a_list]
    a8 = [x.astype(BF16) for x in a8f]
    t = [eye + x for x in a8f]
    a2 = [_mm(x, x).astype(BF16) for x in a8]
    t = [ti + _mm(ti.astype(BF16), x) for ti, x in zip(t, a2)]
    a4 = [_mm(x, x).astype(BF16) for x in a2]
    t = [ti + _mm(ti.astype(BF16), x) for ti, x in zip(t, a4)]
    n = a_list[0].shape[0]
    for s in (8, 16, 32):
        lower_left = (row // (2 * s) == col // (2 * s)) & ((row // s) % 2 == 1) & ((col // s) % 2 == 0)
        second = [slice(r0 + s, r0 + 2 * s) for r0 in range(0, n, 2 * s)]
        tb = [ti.astype(BF16) for ti in t]
        t2 = [jnp.concatenate([ti[r, :] for r in second], axis=0) for ti in t]
        x = [_mm(t2i.astype(BF16), jnp.where(lower_left, a, 0.0).astype(BF16)).astype(BF16)
             for t2i, a in zip(t2, a_list)]
        t2 = [t2i + _mm(xi, tbi) for t2i, xi, tbi in zip(t2, x, tb)]
        t = [jnp.concatenate([piece for m, r in enumerate(second)
                              for piece in (ti[r.start - s:r.start, :], t2i[m * s:(m + 1) * s, :])], axis=0)
             for ti, t2i in zip(t, t2)]
    return t


def _rwkv_parts(p_ref, w0_ref, w2_ref, a0_ref, a2_ref, g2_ref, kk_ref, ka_ref, rk_ref,
                lnw_ref, lnb_ref, o_ref, st_ref, pre_ref):
    L = RWKV_CHUNK
    TB = RWKV_BLOCK
    H2 = 2 * L
    D = RWKV_D
    n_pairs = D // PAIR

    @pl.when(pl.program_id(1) == 0)
    def _():
        st_ref[...] = jnp.zeros(st_ref.shape, F32)

    k = p_ref[:, D:2 * D]
    wa = p_ref[:, 3 * D:3 * D + LANES]
    g_lo = p_ref[:, 3 * D + LANES:3 * D + 2 * LANES]
    lw = -EXP_M_HALF * _sigmoid(w0_ref[...] + _dot1(jnp.tanh(wa), w2_ref[...]))
    alr = _sigmoid(a0_ref[...] + _dot1(wa, a2_ref[...]))
    CB = 4 * L
    blk_tril = (_iota2((CB, CB), 0) >= _iota2((CB, CB), 1)) & (_iota2((CB, CB), 0) // L == _iota2((CB, CB), 1) // L)
    blk_tril = jnp.where(blk_tril, 1.0, 0.0).astype(BF16)
    K_, KK_, ALR_, LW_, CS_, G_ = range(6)
    v_cols = lambda sl: slice(2 * D + sl.start, 2 * D + sl.stop)
    pre_ref[K_] = k * (1.0 + (alr - 1.0) * ka_ref[...])
    pre_ref[KK_] = k * kk_ref[...]
    pre_ref[ALR_] = alr
    pre_ref[LW_] = lw
    for r0 in range(0, TB, CB):
        pre_ref[CS_, r0:r0 + CB, :] = _dot_sel2(blk_tril, lw[r0:r0 + CB, :])
    pre_ref[G_] = _dot1(_sigmoid(g_lo), g2_ref[...])

    lane_lo = _iota2((L, PAIR), 1) < RWKV_HEAD_DIM
    row = _iota2((H2, H2), 0)
    col = _iota2((H2, H2), 1)
    eye = jnp.where(row == col, 1.0, 0.0)
    same_head = row // L == col // L
    strict = same_head & (row % L > col % L)
    incl = same_head & (row % L >= col % L)
    strict_incl = jnp.concatenate([strict, incl], axis=0)

    def independent_part(chunks):
        lhs_a, v_t, a_ab, a_k, lr_arb, hat_bk, p_all = ([] for _ in range(7))
        for c, q in [(c, q) for c in chunks for q in range(n_pairs)]:
            rs = slice(c * L, (c + 1) * L)
            sl = slice(q * PAIR, (q + 1) * PAIR)
            cs_p = pre_ref[CS_, rs, sl]
            cs_last = pre_ref[CS_, (c + 1) * L - 1:(c + 1) * L, sl]
            p_inv = jnp.exp(-cs_p)
            p_end = jnp.exp(cs_last - cs_p)
            kk_p = pre_ref[KK_, rs, sl]
            kk_n = kk_p * lax.rsqrt(jnp.maximum(_seg_sum(kk_p * kk_p, lane_lo), 1e-24))
            b_p = kk_n * pre_ref[ALR_, rs, sl]
            k_p = pre_ref[K_, rs, sl]
            la = _stack_heads(-kk_n * jnp.exp(cs_p - pre_ref[LW_, rs, sl]), lane_lo).astype(BF16)
            lr = _stack_heads(p_ref[rs, sl] * jnp.exp(cs_p), lane_lo).astype(BF16)
            bt = (b_p * p_inv).astype(BF16)
            kt = (k_p * p_inv).astype(BF16)
            g = _mm(jnp.concatenate([la, lr], axis=0), jnp.concatenate([bt, bt, kt, kt], axis=0), NT)
            a_ab.append(jnp.where(strict, g[0:H2, 0:H2], 0.0))
            a_k.append(jnp.where(strict_incl, g[:, H2:], 0.0).astype(BF16))
            lr_arb.append(jnp.concatenate([lr, jnp.where(incl, g[H2:, 0:H2], 0.0).astype(BF16)], axis=1))
            lhs_a.append(la)
            hat_bk.append(jnp.concatenate([_stack_heads(b_p * p_end, lane_lo), _stack_heads(k_p * p_end, lane_lo)],
                                          axis=0).astype(BF16))
            p_all.append(jnp.exp(cs_last))
            v_t.append(_stack_heads(p_ref[rs, v_cols(sl)], lane_lo).T.astype(BF16))
        t_inv = [t.astype(BF16) for t in _unit_lower_inverse_many(a_ab, row, col, eye)]
        t_a = [_mm(t, la).astype(BF16) for t, la in zip(t_inv, lhs_a)]
        both = [_mm(vt, a, NT) for vt, a in zip(v_t, a_k)]
        av_t = [x[:, 0:H2].astype(BF16) for x in both]
        arkv_t = [x[:, H2:] for x in both]
        tav_t = [_mm(x, t, NT) for x, t in zip(av_t, t_inv)]
        return dict(t_a=t_a, tav_t=tav_t, arkv_t=arkv_t, lr_arb=lr_arb, hat_bk=hat_bk, v_t=v_t, p_all=p_all)

    def state_part(chunks, d):
        for ci, c in enumerate(chunks):
            rs = slice(c * L, (c + 1) * L)
            idx = [ci * n_pairs + q for q in range(n_pairs)]
            s0 = [st_ref[q] for q in range(n_pairs)]
            s0b = [s.astype(BF16) for s in s0]
            u_tb = [(_mm(s0b[q], d["t_a"][i], NT) + d["tav_t"][i]).astype(BF16) for q, i in enumerate(idx)]
            for q, i in enumerate(idx):
                st_ref[q] = s0[q] * d["p_all"][i] + _mm(jnp.concatenate([u_tb[q], d["v_t"][i]], axis=1),
                                                        d["hat_bk"][i])
            y_t = [_mm(jnp.concatenate([s0b[q], u_tb[q]], axis=1), d["lr_arb"][i], NT) + d["arkv_t"][i]
                   for q, i in enumerate(idx)]
            for q in range(n_pairs):
                sl = slice(q * PAIR, (q + 1) * PAIR)
                y_st = y_t[q].T
                y = y_st[0:L, :] + y_st[L:H2, :]
                mean = _seg_sum(y, lane_lo) * (1.0 / RWKV_HEAD_DIM)
                dev = y - mean
                var = _seg_sum(dev * dev, lane_lo) * (1.0 / RWKV_HEAD_DIM)
                yn = dev * lax.rsqrt(var + RWKV_GN_EPS) * lnw_ref[:, sl] + lnb_ref[:, sl]
                v_p = p_ref[rs, v_cols(sl)]
                bonus = _seg_sum(p_ref[rs, sl] * pre_ref[K_, rs, sl] * rk_ref[:, sl], lane_lo) * v_p
                o_ref[rs, sl] = ((yn + bonus) * pre_ref[G_, rs, sl]).astype(o_ref.dtype)

    groups = [list(range(c0, c0 + RWKV_GROUP)) for c0 in range(0, TB // L, RWKV_GROUP)]
    return groups, independent_part, state_part


def _mixers_kernel(ps_ref, pr_ref, dtb_ref, alog_ref, dsk_ref, ng_ref,
                   w0_ref, w2_ref, a0_ref, a2_ref, g2_ref, kk_ref, ka_ref, rk_ref, lnw_ref, lnb_ref,
                   os_ref, or_ref, sst_ref, rst_ref, pre_ref):
    ssd_chunk = _ssd_parts(ps_ref, dtb_ref, alog_ref, dsk_ref, ng_ref, os_ref, sst_ref)
    groups, independent_part, state_part = _rwkv_parts(
        pr_ref, w0_ref, w2_ref, a0_ref, a2_ref, g2_ref, kk_ref, ka_ref, rk_ref, lnw_ref, lnb_ref,
        or_ref, rst_ref, pre_ref)
    ssd_chunks = iter(range(SSD_BLOCK // SSD_CHUNK))
    per_group = (SSD_BLOCK // SSD_CHUNK) // len(groups)
    ready = []
    for g in groups:
        ready.append(independent_part(g))
        for _ in range(per_group):
            ssd_chunk(next(ssd_chunks))
    for c in ssd_chunks:
        ssd_chunk(c)
    for g, d in zip(groups, ready):
        state_part(g, d)


def _mixers(p_ssd, p_rwkv, dtb, alog, dsk, ng, w0, w2p, a0, a2p, g2, k_k, k_a, r_k, ln_w, ln_b, batch, seq):
    assert SSD_BLOCK == RWKV_BLOCK
    TB = RWKV_BLOCK
    nb = seq // TB
    const = lambda b, c: (0, 0)
    rows = lambda b, c: (b * nb + c, 0)
    vec = pl.BlockSpec((1, RWKV_D), const)
    return pl.pallas_call(
        _mixers_kernel,
        name="mixers",
        grid=(batch, nb),
        in_specs=[
            pl.BlockSpec((TB, SSD_PCOLS), rows),
            pl.BlockSpec((TB, RWKV_COLS), rows),
            pl.BlockSpec((1, LANES), const),
            pl.BlockSpec((1, LANES), const),
            pl.BlockSpec((1, SSD_D), const),
            pl.BlockSpec((1, SSD_D), const),
            vec,
            pl.BlockSpec((LANES, RWKV_D), const),
            vec,
            pl.BlockSpec((LANES, RWKV_D), const),
            pl.BlockSpec((GATE_LORA, RWKV_D), const),
            vec, vec, vec, vec, vec,
        ],
        out_specs=[pl.BlockSpec((TB, SSD_D), rows), pl.BlockSpec((TB, RWKV_D), rows)],
        out_shape=[jax.ShapeDtypeStruct((batch * seq, SSD_D), BF16),
                   jax.ShapeDtypeStruct((batch * seq, RWKV_D), BF16)],
        scratch_shapes=[
            pltpu.VMEM((SSD_GROUPS, SSD_STATE, SSD_D // SSD_GROUPS), F32),
            pltpu.VMEM((RWKV_D // PAIR, PAIR, PAIR), F32),
            pltpu.VMEM((6, TB, RWKV_D), F32),
        ],
        compiler_params=pltpu.CompilerParams(
            dimension_semantics=("arbitrary", "arbitrary"), vmem_limit_bytes=VMEM_LIMIT),
    )(p_ssd, p_rwkv, dtb, alog, dsk, ng, w0, w2p, a0, a2p, g2, k_k, k_a, r_k, ln_w, ln_b)


def _ffn_kernel(ys_ref, yr_ref, x_ref, wo_ref, g1_ref, g2_ref, wup_ref, cw_ref, cb_ref, wdn_ref, g3_ref,
                o_ref, ubuf_ref, *, tm):
    @pl.when(pl.program_id(1) == 0)
    def _():
        ubuf_ref[...] = jnp.zeros(ubuf_ref.shape, F32)

    mix = _mm(ys_ref[...], wo_ref[0:SSD_D, :]) + _mm(yr_ref[...], wo_ref[SSD_D:, :])
    h = x_ref[...] + _rms(mix, g1_ref[...])
    hn = _rms(h, g2_ref[...]).astype(BF16)

    chunks = [(c0, min(FFN_COLS, D_FF - c0)) for c0 in range(0, D_FF, FFN_COLS)]

    def up(j):
        c0, width = chunks[j]
        return [_mm(hn, wup_ref[:, c:c + width]) for c in (c0, D_FF + c0)]

    def conv(u, c0):
        cs = slice(c0, c0 + u.shape[1])
        tail = ubuf_ref[:, cs]
        out = cb_ref[:, cs] + cw_ref[FFN_CONV - 1:FFN_CONV, cs] * u
        for k in range(1, FFN_CONV):
            out = out + cw_ref[FFN_CONV - 1 - k:FFN_CONV - k, cs] * _shift_rows(u, tail, k)
        ubuf_ref[:, cs] = u[tm - SUBLANES:tm, :]
        return out

    n_steps = len(chunks)
    u_next = up(0)
    acts = []
    for j in range(n_steps):
        u_gate, u_val = u_next
        if j + 1 < n_steps:
            u_next = up(j + 1)
        gate = conv(u_gate, chunks[j][0])
        val = conv(u_val, D_FF + chunks[j][0])
        acts.append((_silu(gate) * val).astype(BF16))
    f = _mm(jnp.concatenate(acts, axis=1), wdn_ref[...])
    o_ref[...] = h + _rms(f, g3_ref[...])


def _ffn(ys, yr, x2, wo, g1, g2, wup, cw, cb, wdn, g3, batch, seq, tm):
    nb = seq // tm
    const = lambda b, i: (0, 0)
    rows = lambda b, i: (b * nb + i, 0)
    res = functools.partial(pl.BlockSpec, index_map=const, pipeline_mode=pl.Buffered(1))
    return pl.pallas_call(
        functools.partial(_ffn_kernel, tm=tm),
        name="outproj_ffn",
        grid=(batch, nb),
        in_specs=[
            pl.BlockSpec((tm, SSD_D), rows),
            pl.BlockSpec((tm, RWKV_D), rows),
            pl.BlockSpec((tm, D_MODEL), rows),
            res((SSD_D + RWKV_D, D_MODEL)),
            pl.BlockSpec((1, D_MODEL), const),
            pl.BlockSpec((1, D_MODEL), const),
            res((D_MODEL, 2 * D_FF)),
            pl.BlockSpec((FFN_CONV, 2 * D_FF), const),
            pl.BlockSpec((1, 2 * D_FF), const),
            res((D_FF, D_MODEL)),
            pl.BlockSpec((1, D_MODEL), const),
        ],
        out_specs=pl.BlockSpec((tm, D_MODEL), rows),
        out_shape=jax.ShapeDtypeStruct((batch * seq, D_MODEL), F32),
        scratch_shapes=[
            pltpu.VMEM((SUBLANES, 2 * D_FF), F32),
        ],
        compiler_params=pltpu.CompilerParams(
            dimension_semantics=("arbitrary", "arbitrary"), vmem_limit_bytes=VMEM_LIMIT),
    )(ys, yr, x2, wo, g1, g2, wup, cw, cb, wdn, g3)


def _pad_lanes(v):
    return jnp.pad(v.astype(F32), (0, LANES - v.shape[0]))[None, :]


def _layer(h2, batch, seq, pre_mix_norm, w_in, ssd_conv_w, ssd_conv_b, ssd_dt_bias, ssd_a_log, ssd_d, ssd_norm,
           rwkv_mu, rwkv_w0, rwkv_w2, rwkv_a0, rwkv_a2, rwkv_g2, rwkv_k_k, rwkv_k_a, rwkv_r_k,
           rwkv_ln_w, rwkv_ln_b, w_out, post_mix_norm, pre_ffn_norm, ffn_w_up, ffn_conv_w,
           ffn_conv_b, ffn_w_down, post_ffn_norm):
    row = lambda v: v.astype(F32).reshape(1, -1)

    w_in = w_in.astype(BF16)
    w_ssd = jnp.pad(w_in[:, :SSD_COLS], ((0, 0), (0, SSD_PCOLS - SSD_COLS)))
    i1 = RWKV_D
    i2 = i1 + DECAY_LORA
    i3 = i2 + RWKV_D
    i4 = i3 + RWKV_D
    i5 = i4 + AAA_LORA
    perm = lambda t: jnp.concatenate(
        [t[..., 0:i1], t[..., i2:i3], t[..., i3:i4], t[..., i1:i2], t[..., i4:i5], t[..., i5:]], axis=-1)
    w_rwkv = perm(w_in[:, SSD_COLS:])
    mu = perm(rwkv_mu).astype(F32).reshape(1, -1)
    w2p = jnp.concatenate([rwkv_w2, jnp.zeros((AAA_LORA, RWKV_D), rwkv_w2.dtype)], axis=0).astype(BF16)
    a2p = jnp.concatenate([jnp.zeros((DECAY_LORA, RWKV_D), rwkv_a2.dtype), rwkv_a2], axis=0).astype(BF16)

    p_ssd, p_rwkv = _inproj(h2, row(pre_mix_norm), w_ssd, w_rwkv, ssd_conv_w.astype(F32), row(ssd_conv_b), mu,
                            seq, tm=INPROJ_ROWS)
    y_ssd, y_rwkv = _mixers(
        p_ssd, p_rwkv, _pad_lanes(ssd_dt_bias), _pad_lanes(ssd_a_log), row(jnp.repeat(ssd_d, SSD_HEAD_DIM)),
        row(ssd_norm), row(rwkv_w0), w2p, row(rwkv_a0), a2p, rwkv_g2.astype(BF16), row(rwkv_k_k),
        row(rwkv_k_a), row(rwkv_r_k), row(rwkv_ln_w), row(rwkv_ln_b), batch, seq)
    return _ffn(y_ssd, y_rwkv, h2, w_out.astype(BF16), row(post_mix_norm), row(pre_ffn_norm),
                ffn_w_up.astype(BF16), ffn_conv_w.astype(F32), row(ffn_conv_b), ffn_w_down.astype(BF16),
                row(post_ffn_norm), batch, seq, tm=FFN_ROWS)


def kernel(x, pre_mix_norm, w_in, ssd_conv_w, ssd_conv_b, ssd_dt_bias, ssd_a_log, ssd_d, ssd_norm, rwkv_mu, rwkv_w0, rwkv_w2, rwkv_a0, rwkv_a2, rwkv_g2, rwkv_k_k, rwkv_k_a, rwkv_r_k, rwkv_ln_w, rwkv_ln_b, w_out, post_mix_norm, pre_ffn_norm, ffn_w_up, ffn_conv_w, ffn_conv_b, ffn_w_down, post_ffn_norm):
    batch, seq, d = x.shape
    h2 = x.reshape(batch * seq, d)
    params = (pre_mix_norm, w_in, ssd_conv_w, ssd_conv_b, ssd_dt_bias, ssd_a_log, ssd_d, ssd_norm, rwkv_mu,
              rwkv_w0, rwkv_w2, rwkv_a0, rwkv_a2, rwkv_g2, rwkv_k_k, rwkv_k_a, rwkv_r_k, rwkv_ln_w, rwkv_ln_b,
              w_out, post_mix_norm, pre_ffn_norm, ffn_w_up, ffn_conv_w, ffn_conv_b, ffn_w_down, post_ffn_norm)
    for l in range(pre_mix_norm.shape[0]):
        h2 = _layer(h2, batch, seq, *(t[l] for t in params))
    return h2.reshape(batch, seq, d)
```

```python
import functools

import jax
import jax.numpy as jnp
from jax import lax
from jax.experimental import pallas as pl
from jax.experimental.pallas import tpu as pltpu

F32 = jnp.float32
BF16 = jnp.bfloat16

D_MODEL = 1024
SSD_HEADS = 8
SSD_HEAD_DIM = 64
SSD_D = SSD_HEADS * SSD_HEAD_DIM
SSD_GROUPS = 2
SSD_STATE = 128
SSD_CONV = 4
SSD_CHUNK = 128
SSD_CONV_DIM = SSD_D + 2 * SSD_GROUPS * SSD_STATE
SSD_COLS = SSD_D + SSD_CONV_DIM + SSD_HEADS
RWKV_HEADS = 8
RWKV_HEAD_DIM = 64
RWKV_D = RWKV_HEADS * RWKV_HEAD_DIM
DECAY_LORA = 64
AAA_LORA = 64
GATE_LORA = 128
RWKV_COLS = 3 * RWKV_D + DECAY_LORA + AAA_LORA + GATE_LORA
RWKV_GN_EPS = 64e-5
D_FF = 2816
FFN_CONV = 3
NORM_EPS = 1e-6
LOG2_E = 1.4426950408889634
EXP_M_HALF = 0.6065306597126334

LANES = 128
SUBLANES = 8
SSD_PCOLS = SSD_D + SSD_CONV_DIM + LANES
SSD_BLOCK = 512
RWKV_CHUNK = 64
RWKV_BLOCK = 512
INPROJ_COLS = 256
RWKV_GROUP = 4
PAIR = 2 * RWKV_HEAD_DIM
FFN_COLS = 768
FFN_ROWS = 1024
VMEM_LIMIT = 56 * 1024 * 1024

NN = (((1,), (0,)), ((), ()))
NT = (((1,), (1,)), ((), ()))
TN = (((0,), (0,)), ((), ()))


def _mm(a, b, dims=NN):
    return lax.dot_general(a, b, dims, preferred_element_type=F32)


def _dot1(a, b, dims=NN):
    return _mm(a.astype(BF16), b.astype(BF16), dims)


def _split3(a):
    a1 = a.astype(BF16)
    r1 = a - a1.astype(F32)
    a2 = r1.astype(BF16)
    a3 = (r1 - a2.astype(F32)).astype(BF16)
    return a1, a2, a3


def _dot_sel(sel_bf16, a):
    a1, a2, a3 = _split3(a)
    return _mm(sel_bf16, a1) + (_mm(sel_bf16, a2) + _mm(sel_bf16, a3))


def _a_dot_sel(a, sel_bf16):
    a1, a2, a3 = _split3(a)
    return _mm(a1, sel_bf16) + (_mm(a2, sel_bf16) + _mm(a3, sel_bf16))


def _dot_sel2(sel_bf16, a):
    hi = a.astype(BF16)
    lo = (a - hi.astype(F32)).astype(BF16)
    return _mm(sel_bf16, hi) + _mm(sel_bf16, lo)


def _shift_rows(x, tail, k):
    sub = _iota2((SUBLANES, x.shape[1]), 0)
    r = pltpu.roll(x, k, axis=0)
    head = jnp.where(sub < k, pltpu.roll(tail, k, axis=0), r[0:SUBLANES, :])
    return jnp.concatenate([head, r[SUBLANES:, :]], axis=0)


def _rms(x, g):
    return x * lax.rsqrt(jnp.mean(x * x, axis=-1, keepdims=True) + NORM_EPS) * g


def _sigmoid(x):
    return 0.5 + 0.5 * jnp.tanh(0.5 * x)


def _silu(x):
    h = 0.5 * x
    return h + h * jnp.tanh(h)


def _softplus(x):
    return jnp.maximum(x, 0.0) + jnp.log(1.0 + jnp.exp(-jnp.abs(x)))


def _iota2(shape, axis):
    return lax.broadcasted_iota(jnp.int32, shape, axis)


def _project(x_ref, g_ref, ws_ref, wr_ref, cw_ref, cb_ref, mu_ref, ps_ref, pr_ref, tail_s_ref, tail_r_ref, first, tm):
    xb = _rms(x_ref[...], g_ref[...]).astype(BF16)
    xbc0 = SSD_D
    dt0 = SSD_D + SSD_CONV_DIM
    def conv_cols(c0):
        ts = slice(c0, c0 + INPROJ_COLS)
        cs = slice(xbc0 + c0, xbc0 + c0 + INPROJ_COLS)
        u = _mm(xb, ws_ref[:, cs])
        tail = jnp.where(first, 0.0, tail_s_ref[:, ts])
        acc = cb_ref[:, ts] + cw_ref[SSD_CONV - 1:SSD_CONV, ts] * u
        for k in range(1, SSD_CONV):
            acc = acc + cw_ref[SSD_CONV - 1 - k:SSD_CONV - k, ts] * _shift_rows(u, tail, k)
        tail_s_ref[:, ts] = u[tm - SUBLANES:tm, :]
        ps_ref[:, cs] = _silu(acc)

    def shift_cols(c0):
        cs = slice(c0, c0 + INPROJ_COLS)
        p = _mm(xb, wr_ref[:, cs])
        tail = jnp.where(first, 0.0, tail_r_ref[:, cs])
        prev = _shift_rows(p, tail, 1)
        tail_r_ref[:, cs] = p[tm - SUBLANES:tm, :]
        pr_ref[:, cs] = p + (prev - p) * mu_ref[:, cs]

    def plain_cols(lo, hi):
        ps_ref[:, lo:hi] = _mm(xb, ws_ref[:, lo:hi])

    conv = [functools.partial(conv_cols, c0) for c0 in range(0, SSD_CONV_DIM, INPROJ_COLS)]
    light = ([functools.partial(plain_cols, c0, c0 + INPROJ_COLS) for c0 in range(0, xbc0, INPROJ_COLS)]
             + [functools.partial(plain_cols, dt0, SSD_PCOLS)]
             + [functools.partial(shift_cols, c0) for c0 in range(0, RWKV_COLS, INPROJ_COLS)])
    per_conv = 2
    for j, step in enumerate(conv):
        step()
        for fill in light[j * per_conv:(j + 1) * per_conv]:
            fill()
    for fill in light[len(conv) * per_conv:]:
        fill()


def _ssd_parts(p_ref, dtb_ref, alog_ref, dsk_ref, ng_ref, o_ref, st_ref):
    L = SSD_CHUNK
    gw = SSD_D // SSD_GROUPS

    @pl.when(pl.program_id(1) == 0)
    def _():
        st_ref[...] = jnp.zeros(st_ref.shape, F32)

    row = _iota2((L, L), 0)
    col = _iota2((L, L), 1)
    causal = row >= col
    tril = jnp.where(causal, 1.0, 0.0).astype(BF16)
    hsel = (_iota2((LANES, SSD_D), 1) // SSD_HEAD_DIM == _iota2((LANES, SSD_D), 0))
    hsel = jnp.where(hsel, 1.0, 0.0).astype(BF16)
    lane_lo = _iota2((L, LANES), 1) < SSD_HEAD_DIM
    xbc_cols = slice(SSD_D, SSD_D + SSD_CONV_DIM)

    def chunk(c):
        rs = slice(c * L, (c + 1) * L)
        z = p_ref[rs, 0:SSD_D]
        xbc = p_ref[rs, xbc_cols]
        dt_raw = p_ref[rs, SSD_D + SSD_CONV_DIM:SSD_PCOLS]
        xs = xbc[:, 0:SSD_D]

        dt = _softplus(dt_raw + dtb_ref[...])
        a = dt * (-LOG2_E * jnp.exp(alog_ref[...]))
        a_cs = _dot_sel(tril, a)
        a_cs_t = a_cs.T
        dt_e = _mm(dt.astype(BF16), hsel)
        acs_e = _a_dot_sel(a_cs, hsel)
        ea_e = jnp.exp2(acs_e)
        ds_e = jnp.exp2(acs_e[L - 1:L, :] - acs_e)

        x_dt = xs * dt_e
        x_b = x_dt.astype(BF16)
        x_dec = (x_dt * ds_e).astype(BF16)

        for g in range(SSD_GROUPS):
            gs = slice(g * gw, (g + 1) * gw)
            b_g = xbc[:, SSD_D + g * SSD_STATE:SSD_D + (g + 1) * SSD_STATE].astype(BF16)
            c_off = SSD_D + SSD_GROUPS * SSD_STATE
            c_g = xbc[:, c_off + g * SSD_STATE:c_off + (g + 1) * SSD_STATE].astype(BF16)
            scores = _mm(c_g, b_g, NT)
            state = st_ref[g]
            y_off = _mm(c_g, state.astype(BF16)) * ea_e[:, gs]
            st_ref[g] = state * ea_e[L - 1:L, gs] + _mm(b_g, x_dec[:, gs], TN)
            y_parts = []
            for j in range(gw // LANES):
                h0 = g * (SSD_HEADS // SSD_GROUPS) + 2 * j
                ms = []
                for h in (h0, h0 + 1):
                    seg = a_cs[:, h:h + 1] - a_cs_t[h:h + 1, :]
                    dec = jnp.exp2(jnp.where(causal, seg, -jnp.inf))
                    ms.append((scores * dec).astype(BF16))
                xp = x_b[:, h0 * SSD_HEAD_DIM:h0 * SSD_HEAD_DIM + LANES]
                zero = jnp.zeros_like(xp)
                x_bd = jnp.concatenate([jnp.where(lane_lo, xp, zero), jnp.where(lane_lo, zero, xp)], axis=0)
                y_parts.append(_mm(jnp.concatenate(ms, axis=1), x_bd))
            y = jnp.concatenate(y_parts, axis=1) + y_off
            y = y + dsk_ref[:, gs] * xs[:, gs]
            y = y * _silu(z[:, gs])
            y = y * lax.rsqrt(jnp.mean(y * y, axis=-1, keepdims=True) + NORM_EPS)
            o_ref[rs, gs] = (y * ng_ref[:, gs]).astype(o_ref.dtype)
    return chunk


def _seg_sum(x, lane_lo):
    s_lo = jnp.sum(jnp.where(lane_lo, x, 0.0), axis=-1, keepdims=True)
    s_hi = jnp.sum(jnp.where(lane_lo, 0.0, x), axis=-1, keepdims=True)
    return jnp.where(lane_lo, s_lo, s_hi)


def _stack_heads(x, lane_lo):
    zero = jnp.zeros_like(x)
    return jnp.concatenate([jnp.where(lane_lo, x, zero), jnp.where(lane_lo, zero, x)], axis=0)


def _unit_lower_inverse_many(a_list, row, col, eye):
    blk8 = row // 8 == col // 8
    a8f = [jnp.where(blk8, a, 0.0) for a in a_list]
    a8 = [x.astype(BF16) for x in a8f]
    t = [eye + x for x in a8f]
    a2 = [_mm(x, x).astype(BF16) for x in a8]
    t = [ti + _mm(ti.astype(BF16), x) for ti, x in zip(t, a2)]
    a4 = [_mm(x, x).astype(BF16) for x in a2]
    t = [ti + _mm(ti.astype(BF16), x) for ti, x in zip(t, a4)]
    n = a_list[0].shape[0]
    for s in (8, 16, 32):
        lower_left = (row // (2 * s) == col // (2 * s)) & ((row // s) % 2 == 1) & ((col // s) % 2 == 0)
        second = [slice(r0 + s, r0 + 2 * s) for r0 in range(0, n, 2 * s)]
        tb = [ti.astype(BF16) for ti in t]
        t2 = [jnp.concatenate([ti[r, :] for r in second], axis=0) for ti in t]
        x = [_mm(t2i.astype(BF16), jnp.where(lower_left, a, 0.0).astype(BF16)).astype(BF16)
             for t2i, a in zip(t2, a_list)]
        t2 = [t2i + _mm(xi, tbi) for t2i, xi, tbi in zip(t2, x, tb)]
        t = [jnp.concatenate([piece for m, r in enumerate(second)
                              for piece in (ti[r.start - s:r.start, :], t2i[m * s:(m + 1) * s, :])], axis=0)
             for ti, t2i in zip(t, t2)]
    return t


def _rwkv_parts(p_ref, w0_ref, w2_ref, a0_ref, a2_ref, g2_ref, kk_ref, ka_ref, rk_ref,
                lnw_ref, lnb_ref, o_ref, st_ref, pre_ref):
    L = RWKV_CHUNK
    TB = RWKV_BLOCK
    H2 = 2 * L
    D = RWKV_D
    n_pairs = D // PAIR

    @pl.when(pl.program_id(1) == 0)
    def _():
        st_ref[...] = jnp.zeros(st_ref.shape, F32)

    k = p_ref[:, D:2 * D]
    wa = p_ref[:, 3 * D:3 * D + LANES]
    g_lo = p_ref[:, 3 * D + LANES:3 * D + 2 * LANES]
    lw = -EXP_M_HALF * _sigmoid(w0_ref[...] + _dot1(jnp.tanh(wa), w2_ref[...]))
    alr = _sigmoid(a0_ref[...] + _dot1(wa, a2_ref[...]))
    CB = 4 * L
    blk_tril = (_iota2((CB, CB), 0) >= _iota2((CB, CB), 1)) & (_iota2((CB, CB), 0) // L == _iota2((CB, CB), 1) // L)
    blk_tril = jnp.where(blk_tril, 1.0, 0.0).astype(BF16)
    K_, KK_, ALR_, LW_, CS_, G_ = range(6)
    v_cols = lambda sl: slice(2 * D + sl.start, 2 * D + sl.stop)
    pre_ref[K_] = k * (1.0 + (alr - 1.0) * ka_ref[...])
    pre_ref[KK_] = k * kk_ref[...]
    pre_ref[ALR_] = alr
    pre_ref[LW_] = lw
    for r0 in range(0, TB, CB):
        pre_ref[CS_, r0:r0 + CB, :] = _dot_sel2(blk_tril, lw[r0:r0 + CB, :])
    pre_ref[G_] = _dot1(_sigmoid(g_lo), g2_ref[...])

    lane_lo = _iota2((L, PAIR), 1) < RWKV_HEAD_DIM
    row = _iota2((H2, H2), 0)
    col = _iota2((H2, H2), 1)
    eye = jnp.where(row == col, 1.0, 0.0)
    same_head = row // L == col // L
    strict = same_head & (row % L > col % L)
    incl = same_head & (row % L >= col % L)
    strict_incl = jnp.concatenate([strict, incl], axis=0)

    def independent_part(chunks):
        lhs_a, v_t, a_ab, a_k, lr_arb, hat_bk, p_all = ([] for _ in range(7))
        for c, q in [(c, q) for c in chunks for q in range(n_pairs)]:
            rs = slice(c * L, (c + 1) * L)
            sl = slice(q * PAIR, (q + 1) * PAIR)
            cs_p = pre_ref[CS_, rs, sl]
            cs_last = pre_ref[CS_, (c + 1) * L - 1:(c + 1) * L, sl]
            p_inv = jnp.exp(-cs_p)
            p_end = jnp.exp(cs_last - cs_p)
            kk_p = pre_ref[KK_, rs, sl]
            kk_n = kk_p * lax.rsqrt(jnp.maximum(_seg_sum(kk_p * kk_p, lane_lo), 1e-24))
            b_p = kk_n * pre_ref[ALR_, rs, sl]
            k_p = pre_ref[K_, rs, sl]
            la = _stack_heads(-kk_n * jnp.exp(cs_p - pre_ref[LW_, rs, sl]), lane_lo).astype(BF16)
            lr = _stack_heads(p_ref[rs, sl] * jnp.exp(cs_p), lane_lo).astype(BF16)
            bt = (b_p * p_inv).astype(BF16)
            kt = (k_p * p_inv).astype(BF16)
            g = _mm(jnp.concatenate([la, lr], axis=0), jnp.concatenate([bt, bt, kt, kt], axis=0), NT)
            a_ab.append(jnp.where(strict, g[0:H2, 0:H2], 0.0))
            a_k.append(jnp.where(strict_incl, g[:, H2:], 0.0).astype(BF16))
            lr_arb.append(jnp.concatenate([lr, jnp.where(incl, g[H2:, 0:H2], 0.0).astype(BF16)], axis=1))
            lhs_a.append(la)
            hat_bk.append(jnp.concatenate([_stack_heads(b_p * p_end, lane_lo), _stack_heads(k_p * p_end, lane_lo)],
                                          axis=0).astype(BF16))
            p_all.append(jnp.exp(cs_last))
            v_t.append(_stack_heads(p_ref[rs, v_cols(sl)], lane_lo).T.astype(BF16))
        t_inv = [t.astype(BF16) for t in _unit_lower_inverse_many(a_ab, row, col, eye)]
        t_a = [_mm(t, la).astype(BF16) for t, la in zip(t_inv, lhs_a)]
        both = [_mm(vt, a, NT) for vt, a in zip(v_t, a_k)]
        av_t = [x[:, 0:H2].astype(BF16) for x in both]
        arkv_t = [x[:, H2:] for x in both]
        tav_t = [_mm(x, t, NT) for x, t in zip(av_t, t_inv)]
        return dict(t_a=t_a, tav_t=tav_t, arkv_t=arkv_t, lr_arb=lr_arb, hat_bk=hat_bk, v_t=v_t, p_all=p_all)

    def state_part(chunks, d):
        for ci, c in enumerate(chunks):
            rs = slice(c * L, (c + 1) * L)
            idx = [ci * n_pairs + q for q in range(n_pairs)]
            s0 = [st_ref[q] for q in range(n_pairs)]
            s0b = [s.astype(BF16) for s in s0]
            u_tb = [(_mm(s0b[q], d["t_a"][i], NT) + d["tav_t"][i]).astype(BF16) for q, i in enumerate(idx)]
            for q, i in enumerate(idx):
                st_ref[q] = s0[q] * d["p_all"][i] + _mm(jnp.concatenate([u_tb[q], d["v_t"][i]], axis=1),
                                                        d["hat_bk"][i])
            y_t = [_mm(jnp.concatenate([s0b[q], u_tb[q]], axis=1), d["lr_arb"][i], NT) + d["arkv_t"][i]
                   for q, i in enumerate(idx)]
            for q in range(n_pairs):
                sl = slice(q * PAIR, (q + 1) * PAIR)
                y_st = y_t[q].T
                y = y_st[0:L, :] + y_st[L:H2, :]
                mean = _seg_sum(y, lane_lo) * (1.0 / RWKV_HEAD_DIM)
                dev = y - mean
                var = _seg_sum(dev * dev, lane_lo) * (1.0 / RWKV_HEAD_DIM)
                yn = dev * lax.rsqrt(var + RWKV_GN_EPS) * lnw_ref[:, sl] + lnb_ref[:, sl]
                v_p = p_ref[rs, v_cols(sl)]
                bonus = _seg_sum(p_ref[rs, sl] * pre_ref[K_, rs, sl] * rk_ref[:, sl], lane_lo) * v_p
                o_ref[rs, sl] = ((yn + bonus) * pre_ref[G_, rs, sl]).astype(o_ref.dtype)

    groups = [list(range(c0, c0 + RWKV_GROUP)) for c0 in range(0, TB // L, RWKV_GROUP)]
    return groups, independent_part, state_part


def _mixers_kernel(x_ref, gin_ref, ws_ref, wr_ref, cw_ref, cb_ref, mu_ref, dtb_ref, alog_ref, dsk_ref, ng_ref,
                   w0_ref, w2_ref, a0_ref, a2_ref, g2_ref, kk_ref, ka_ref, rk_ref, lnw_ref, lnb_ref,
                   os_ref, or_ref, sst_ref, rst_ref, pre_ref, ps_ref, pr_ref, tail_s_ref, tail_r_ref):
    first = pl.program_id(1) == 0

    @pl.when(first)
    def _():
        tail_s_ref[...] = jnp.zeros(tail_s_ref.shape, F32)
        tail_r_ref[...] = jnp.zeros(tail_r_ref.shape, F32)

    _project(x_ref, gin_ref, ws_ref, wr_ref, cw_ref, cb_ref, mu_ref, ps_ref, pr_ref, tail_s_ref, tail_r_ref,
             first, RWKV_BLOCK)
    ssd_chunk = _ssd_parts(ps_ref, dtb_ref, alog_ref, dsk_ref, ng_ref, os_ref, sst_ref)
    groups, independent_part, state_part = _rwkv_parts(
        pr_ref, w0_ref, w2_ref, a0_ref, a2_ref, g2_ref, kk_ref, ka_ref, rk_ref, lnw_ref, lnb_ref,
        or_ref, rst_ref, pre_ref)
    ssd_chunks = iter(range(SSD_BLOCK // SSD_CHUNK))
    per_group = (SSD_BLOCK // SSD_CHUNK) // len(groups)
    ready = []
    for g in groups:
        ready.append(independent_part(g))
        for _ in range(per_group):
            ssd_chunk(next(ssd_chunks))
    for c in ssd_chunks:
        ssd_chunk(c)
    for g, d in zip(groups, ready):
        state_part(g, d)


def _mixers(x2, g_in, w_ssd, w_rwkv, cw, cb, mu, dtb, alog, dsk, ng, w0, w2p, a0, a2p, g2, k_k, k_a, r_k,
            ln_w, ln_b, batch, seq):
    assert SSD_BLOCK == RWKV_BLOCK
    TB = RWKV_BLOCK
    nb = seq // TB
    const = lambda b, c: (0, 0)
    rows = lambda b, c: (b * nb + c, 0)
    vec = pl.BlockSpec((1, RWKV_D), const)
    return pl.pallas_call(
        _mixers_kernel,
        name="mixers",
        grid=(batch, nb),
        in_specs=[
            pl.BlockSpec((TB, D_MODEL), rows),
            pl.BlockSpec((1, D_MODEL), const),
            pl.BlockSpec((D_MODEL, SSD_PCOLS), const, pipeline_mode=pl.Buffered(1)),
            pl.BlockSpec((D_MODEL, RWKV_COLS), const, pipeline_mode=pl.Buffered(1)),
            pl.BlockSpec((SSD_CONV, SSD_CONV_DIM), const),
            pl.BlockSpec((1, SSD_CONV_DIM), const),
            pl.BlockSpec((1, RWKV_COLS), const),
            pl.BlockSpec((1, LANES), const),
            pl.BlockSpec((1, LANES), const),
            pl.BlockSpec((1, SSD_D), const),
            pl.BlockSpec((1, SSD_D), const),
            vec,
            pl.BlockSpec((LANES, RWKV_D), const),
            vec,
            pl.BlockSpec((LANES, RWKV_D), const),
            pl.BlockSpec((GATE_LORA, RWKV_D), const),
            vec, vec, vec, vec, vec,
        ],
        out_specs=[pl.BlockSpec((TB, SSD_D), rows), pl.BlockSpec((TB, RWKV_D), rows)],
        out_shape=[jax.ShapeDtypeStruct((batch * seq, SSD_D), BF16),
                   jax.ShapeDtypeStruct((batch * seq, RWKV_D), BF16)],
        scratch_shapes=[
            pltpu.VMEM((SSD_GROUPS, SSD_STATE, SSD_D // SSD_GROUPS), F32),
            pltpu.VMEM((RWKV_D // PAIR, PAIR, PAIR), F32),
            pltpu.VMEM((6, TB, RWKV_D), F32),
            pltpu.VMEM((TB, SSD_PCOLS), F32),
            pltpu.VMEM((TB, RWKV_COLS), F32),
            pltpu.VMEM((SUBLANES, SSD_CONV_DIM), F32),
            pltpu.VMEM((SUBLANES, RWKV_COLS), F32),
        ],
        compiler_params=pltpu.CompilerParams(
            dimension_semantics=("arbitrary", "arbitrary"), vmem_limit_bytes=VMEM_LIMIT),
    )(x2, g_in, w_ssd, w_rwkv, cw, cb, mu, dtb, alog, dsk, ng, w0, w2p, a0, a2p, g2, k_k, k_a, r_k, ln_w, ln_b)


def _ffn_kernel(ys_ref, yr_ref, x_ref, wo_ref, g1_ref, g2_ref, wup_ref, cw_ref, cb_ref, wdn_ref, g3_ref,
                o_ref, ubuf_ref, *, tm):
    @pl.when(pl.program_id(1) == 0)
    def _():
        ubuf_ref[...] = jnp.zeros(ubuf_ref.shape, F32)

    mix = _mm(ys_ref[...], wo_ref[0:SSD_D, :]) + _mm(yr_ref[...], wo_ref[SSD_D:, :])
    h = x_ref[...] + _rms(mix, g1_ref[...])
    hn = _rms(h, g2_ref[...]).astype(BF16)

    chunks = [(c0, min(FFN_COLS, D_FF - c0)) for c0 in range(0, D_FF, FFN_COLS)]

    def up(j):
        c0, width = chunks[j]
        return [_mm(hn, wup_ref[:, c:c + width]) for c in (c0, D_FF + c0)]

    def conv(u, c0):
        cs = slice(c0, c0 + u.shape[1])
        tail = ubuf_ref[:, cs]
        out = cb_ref[:, cs] + cw_ref[FFN_CONV - 1:FFN_CONV, cs] * u
        for k in range(1, FFN_CONV):
            out = out + cw_ref[FFN_CONV - 1 - k:FFN_CONV - k, cs] * _shift_rows(u, tail, k)
        ubuf_ref[:, cs] = u[tm - SUBLANES:tm, :]
        return out

    n_steps = len(chunks)
    u_next = up(0)
    acts = []
    for j in range(n_steps):
        u_gate, u_val = u_next
        if j + 1 < n_steps:
            u_next = up(j + 1)
        gate = conv(u_gate, chunks[j][0])
        val = conv(u_val, D_FF + chunks[j][0])
        acts.append((_silu(gate) * val).astype(BF16))
    f = _mm(jnp.concatenate(acts, axis=1), wdn_ref[...])
    o_ref[...] = h + _rms(f, g3_ref[...])


def _ffn(ys, yr, x2, wo, g1, g2, wup, cw, cb, wdn, g3, batch, seq, tm):
    nb = seq // tm
    const = lambda b, i: (0, 0)
    rows = lambda b, i: (b * nb + i, 0)
    res = functools.partial(pl.BlockSpec, index_map=const, pipeline_mode=pl.Buffered(1))
    return pl.pallas_call(
        functools.partial(_ffn_kernel, tm=tm),
        name="outproj_ffn",
        grid=(batch, nb),
        in_specs=[
            pl.BlockSpec((tm, SSD_D), rows),
            pl.BlockSpec((tm, RWKV_D), rows),
            pl.BlockSpec((tm, D_MODEL), rows),
            res((SSD_D + RWKV_D, D_MODEL)),
            pl.BlockSpec((1, D_MODEL), const),
            pl.BlockSpec((1, D_MODEL), const),
            res((D_MODEL, 2 * D_FF)),
            pl.BlockSpec((FFN_CONV, 2 * D_FF), const),
            pl.BlockSpec((1, 2 * D_FF), const),
            res((D_FF, D_MODEL)),
            pl.BlockSpec((1, D_MODEL), const),
        ],
        out_specs=pl.BlockSpec((tm, D_MODEL), rows),
        out_shape=jax.ShapeDtypeStruct((batch * seq, D_MODEL), F32),
        scratch_shapes=[
            pltpu.VMEM((SUBLANES, 2 * D_FF), F32),
        ],
        compiler_params=pltpu.CompilerParams(
            dimension_semantics=("arbitrary", "arbitrary"), vmem_limit_bytes=VMEM_LIMIT),
    )(ys, yr, x2, wo, g1, g2, wup, cw, cb, wdn, g3)


def _pad_lanes(v):
    return jnp.pad(v.astype(F32), (0, LANES - v.shape[0]))[None, :]


def _layer(h2, batch, seq, pre_mix_norm, w_in, ssd_conv_w, ssd_conv_b, ssd_dt_bias, ssd_a_log, ssd_d, ssd_norm,
           rwkv_mu, rwkv_w0, rwkv_w2, rwkv_a0, rwkv_a2, rwkv_g2, rwkv_k_k, rwkv_k_a, rwkv_r_k,
           rwkv_ln_w, rwkv_ln_b, w_out, post_mix_norm, pre_ffn_norm, ffn_w_up, ffn_conv_w,
           ffn_conv_b, ffn_w_down, post_ffn_norm):
    row = lambda v: v.astype(F32).reshape(1, -1)

    w_in = w_in.astype(BF16)
    w_ssd = jnp.pad(w_in[:, :SSD_COLS], ((0, 0), (0, SSD_PCOLS - SSD_COLS)))
    i1 = RWKV_D
    i2 = i1 + DECAY_LORA
    i3 = i2 + RWKV_D
    i4 = i3 + RWKV_D
    i5 = i4 + AAA_LORA
    perm = lambda t: jnp.concatenate(
        [t[..., 0:i1], t[..., i2:i3], t[..., i3:i4], t[..., i1:i2], t[..., i4:i5], t[..., i5:]], axis=-1)
    w_rwkv = perm(w_in[:, SSD_COLS:])
    mu = perm(rwkv_mu).astype(F32).reshape(1, -1)
    w2p = jnp.concatenate([rwkv_w2, jnp.zeros((AAA_LORA, RWKV_D), rwkv_w2.dtype)], axis=0).astype(BF16)
    a2p = jnp.concatenate([jnp.zeros((DECAY_LORA, RWKV_D), rwkv_a2.dtype), rwkv_a2], axis=0).astype(BF16)

    y_ssd, y_rwkv = _mixers(
        h2, row(pre_mix_norm), w_ssd, w_rwkv, ssd_conv_w.astype(F32), row(ssd_conv_b), mu, _pad_lanes(ssd_dt_bias), _pad_lanes(ssd_a_log), row(jnp.repeat(ssd_d, SSD_HEAD_DIM)),
        row(ssd_norm), row(rwkv_w0), w2p, row(rwkv_a0), a2p, rwkv_g2.astype(BF16), row(rwkv_k_k),
        row(rwkv_k_a), row(rwkv_r_k), row(rwkv_ln_w), row(rwkv_ln_b), batch, seq)
    return _ffn(y_ssd, y_rwkv, h2, w_out.astype(BF16), row(post_mix_norm), row(pre_ffn_norm),
                ffn_w_up.astype(BF16), ffn_conv_w.astype(F32), row(ffn_conv_b), ffn_w_down.astype(BF16),
                row(post_ffn_norm), batch, seq, tm=FFN_ROWS)


def kernel(x, pre_mix_norm, w_in, ssd_conv_w, ssd_conv_b, ssd_dt_bias, ssd_a_log, ssd_d, ssd_norm, rwkv_mu, rwkv_w0, rwkv_w2, rwkv_a0, rwkv_a2, rwkv_g2, rwkv_k_k, rwkv_k_a, rwkv_r_k, rwkv_ln_w, rwkv_ln_b, w_out, post_mix_norm, pre_ffn_norm, ffn_w_up, ffn_conv_w, ffn_conv_b, ffn_w_down, post_ffn_norm):
    batch, seq, d = x.shape
    h2 = x.reshape(batch * seq, d)
    params = (pre_mix_norm, w_in, ssd_conv_w, ssd_conv_b, ssd_dt_bias, ssd_a_log, ssd_d, ssd_norm, rwkv_mu,
              rwkv_w0, rwkv_w2, rwkv_a0, rwkv_a2, rwkv_g2, rwkv_k_k, rwkv_k_a, rwkv_r_k, rwkv_ln_w, rwkv_ln_b,
              w_out, post_mix_norm, pre_ffn_norm, ffn_w_up, ffn_conv_w, ffn_conv_b, ffn_w_down, post_ffn_norm)
    for l in range(pre_mix_norm.shape[0]):
        h2 = _layer(h2, batch, seq, *(t[l] for t in params))
    return h2.reshape(batch, seq, d)
```

```python
import functools

import jax
import jax.numpy as jnp
from jax import lax
from jax.experimental import pallas as pl
from jax.experimental.pallas import tpu as pltpu

F32 = jnp.float32
BF16 = jnp.bfloat16

D_MODEL = 1024
SSD_HEADS = 8
SSD_HEAD_DIM = 64
SSD_D = SSD_HEADS * SSD_HEAD_DIM
SSD_GROUPS = 2
SSD_STATE = 128
SSD_CONV = 4
SSD_CHUNK = 128
SSD_CONV_DIM = SSD_D + 2 * SSD_GROUPS * SSD_STATE
SSD_COLS = SSD_D + SSD_CONV_DIM + SSD_HEADS
RWKV_HEADS = 8
RWKV_HEAD_DIM = 64
RWKV_D = RWKV_HEADS * RWKV_HEAD_DIM
DECAY_LORA = 64
AAA_LORA = 64
GATE_LORA = 128
RWKV_COLS = 3 * RWKV_D + DECAY_LORA + AAA_LORA + GATE_LORA
RWKV_GN_EPS = 64e-5
D_FF = 2816
FFN_CONV = 3
NORM_EPS = 1e-6
LOG2_E = 1.4426950408889634
EXP_M_HALF = 0.6065306597126334

LANES = 128
SUBLANES = 8
SSD_PCOLS = SSD_D + SSD_CONV_DIM + LANES
SSD_BLOCK = 512
RWKV_CHUNK = 64
RWKV_BLOCK = 512
INPROJ_COLS = 256
RWKV_GROUP = 4
PAIR = 2 * RWKV_HEAD_DIM
FFN_COLS = 768
INPROJ_ROWS = 1024
FFN_ROWS = 1024
FFN_PART = 512
VMEM_LIMIT = 56 * 1024 * 1024

NN = (((1,), (0,)), ((), ()))
NT = (((1,), (1,)), ((), ()))
TN = (((0,), (0,)), ((), ()))


def _mm(a, b, dims=NN):
    return lax.dot_general(a, b, dims, preferred_element_type=F32)


def _dot1(a, b, dims=NN):
    return _mm(a.astype(BF16), b.astype(BF16), dims)


def _split3(a):
    a1 = a.astype(BF16)
    r1 = a - a1.astype(F32)
    a2 = r1.astype(BF16)
    a3 = (r1 - a2.astype(F32)).astype(BF16)
    return a1, a2, a3


def _dot_sel(sel_bf16, a):
    a1, a2, a3 = _split3(a)
    return _mm(sel_bf16, a1) + (_mm(sel_bf16, a2) + _mm(sel_bf16, a3))


def _a_dot_sel(a, sel_bf16):
    a1, a2, a3 = _split3(a)
    return _mm(a1, sel_bf16) + (_mm(a2, sel_bf16) + _mm(a3, sel_bf16))


def _dot_sel2(sel_bf16, a):
    hi = a.astype(BF16)
    lo = (a - hi.astype(F32)).astype(BF16)
    return _mm(sel_bf16, hi) + _mm(sel_bf16, lo)


def _shift_rows(x, tail, k):
    sub = _iota2((SUBLANES, x.shape[1]), 0)
    r = pltpu.roll(x, k, axis=0)
    head = jnp.where(sub < k, pltpu.roll(tail, k, axis=0), r[0:SUBLANES, :])
    return jnp.concatenate([head, r[SUBLANES:, :]], axis=0)


def _rms(x, g):
    return x * lax.rsqrt(jnp.mean(x * x, axis=-1, keepdims=True) + NORM_EPS) * g


def _sigmoid(x):
    return 0.5 + 0.5 * jnp.tanh(0.5 * x)


def _silu(x):
    h = 0.5 * x
    return h + h * jnp.tanh(h)


def _softplus(x):
    return jnp.maximum(x, 0.0) + jnp.log(1.0 + jnp.exp(-jnp.abs(x)))


def _iota2(shape, axis):
    return lax.broadcasted_iota(jnp.int32, shape, axis)


def _inproj_kernel(x_ref, g_ref, ws_ref, wr_ref, cw_ref, cb_ref, mu_ref, ps_ref, pr_ref, tail_s_ref, tail_r_ref,
                   *, tm, blocks_per_seq):
    i = pl.program_id(0)

    @pl.when(i == 0)
    def _():
        tail_s_ref[...] = jnp.zeros(tail_s_ref.shape, F32)
        tail_r_ref[...] = jnp.zeros(tail_r_ref.shape, F32)

    first = i % blocks_per_seq == 0
    xb = _rms(x_ref[...], g_ref[...]).astype(BF16)
    xbc0 = SSD_D
    dt0 = SSD_D + SSD_CONV_DIM
    def conv_cols(c0):
        ts = slice(c0, c0 + INPROJ_COLS)
        cs = slice(xbc0 + c0, xbc0 + c0 + INPROJ_COLS)
        u = _mm(xb, ws_ref[:, cs])
        tail = jnp.where(first, 0.0, tail_s_ref[:, ts])
        acc = cb_ref[:, ts] + cw_ref[SSD_CONV - 1:SSD_CONV, ts] * u
        for k in range(1, SSD_CONV):
            acc = acc + cw_ref[SSD_CONV - 1 - k:SSD_CONV - k, ts] * _shift_rows(u, tail, k)
        tail_s_ref[:, ts] = u[tm - SUBLANES:tm, :]
        ps_ref[:, cs] = _silu(acc)

    def shift_cols(c0):
        cs = slice(c0, c0 + INPROJ_COLS)
        p = _mm(xb, wr_ref[:, cs])
        tail = jnp.where(first, 0.0, tail_r_ref[:, cs])
        prev = _shift_rows(p, tail, 1)
        tail_r_ref[:, cs] = p[tm - SUBLANES:tm, :]
        pr_ref[:, cs] = p + (prev - p) * mu_ref[:, cs]

    def plain_cols(lo, hi):
        ps_ref[:, lo:hi] = _mm(xb, ws_ref[:, lo:hi])

    conv = [functools.partial(conv_cols, c0) for c0 in range(0, SSD_CONV_DIM, INPROJ_COLS)]
    light = ([functools.partial(plain_cols, c0, c0 + INPROJ_COLS) for c0 in range(0, xbc0, INPROJ_COLS)]
             + [functools.partial(plain_cols, dt0, SSD_PCOLS)]
             + [functools.partial(shift_cols, c0) for c0 in range(0, RWKV_COLS, INPROJ_COLS)])
    per_conv = 2
    for j, step in enumerate(conv):
        step()
        for fill in light[j * per_conv:(j + 1) * per_conv]:
            fill()
    for fill in light[len(conv) * per_conv:]:
        fill()


def _inproj(x2, g, w_ssd, w_rwkv, cw, cb, mu, seq, tm):
    n = x2.shape[0]
    const = lambda i: (0, 0)
    return pl.pallas_call(
        functools.partial(_inproj_kernel, tm=tm, blocks_per_seq=seq // tm),
        name="inproj",
        grid=(n // tm,),
        in_specs=[
            pl.BlockSpec((tm, D_MODEL), lambda i: (i, 0)),
            pl.BlockSpec((1, D_MODEL), const),
            pl.BlockSpec((D_MODEL, SSD_PCOLS), const, pipeline_mode=pl.Buffered(1)),
            pl.BlockSpec((D_MODEL, RWKV_COLS), const, pipeline_mode=pl.Buffered(1)),
            pl.BlockSpec((SSD_CONV, SSD_CONV_DIM), const),
            pl.BlockSpec((1, SSD_CONV_DIM), const),
            pl.BlockSpec((1, RWKV_COLS), const),
        ],
        out_specs=[
            pl.BlockSpec((tm, SSD_PCOLS), lambda i: (i, 0)),
            pl.BlockSpec((tm, RWKV_COLS), lambda i: (i, 0)),
        ],
        out_shape=[
            jax.ShapeDtypeStruct((n, SSD_PCOLS), F32),
            jax.ShapeDtypeStruct((n, RWKV_COLS), F32),
        ],
        scratch_shapes=[
            pltpu.VMEM((SUBLANES, SSD_CONV_DIM), F32),
            pltpu.VMEM((SUBLANES, RWKV_COLS), F32),
        ],
        compiler_params=pltpu.CompilerParams(
            dimension_semantics=("arbitrary",), vmem_limit_bytes=VMEM_LIMIT),
    )(x2, g, w_ssd, w_rwkv, cw, cb, mu)


def _ssd_parts(p_ref, dtb_ref, alog_ref, dsk_ref, ng_ref, o_ref, st_ref):
    L = SSD_CHUNK
    gw = SSD_D // SSD_GROUPS

    @pl.when(pl.program_id(1) == 0)
    def _():
        st_ref[...] = jnp.zeros(st_ref.shape, F32)

    row = _iota2((L, L), 0)
    col = _iota2((L, L), 1)
    causal = row >= col
    tril = jnp.where(causal, 1.0, 0.0).astype(BF16)
    hsel = (_iota2((LANES, SSD_D), 1) // SSD_HEAD_DIM == _iota2((LANES, SSD_D), 0))
    hsel = jnp.where(hsel, 1.0, 0.0).astype(BF16)
    lane_lo = _iota2((L, LANES), 1) < SSD_HEAD_DIM
    xbc_cols = slice(SSD_D, SSD_D + SSD_CONV_DIM)

    def chunk(c):
        rs = slice(c * L, (c + 1) * L)
        z = p_ref[rs, 0:SSD_D]
        xbc = p_ref[rs, xbc_cols]
        dt_raw = p_ref[rs, SSD_D + SSD_CONV_DIM:SSD_PCOLS]
        xs = xbc[:, 0:SSD_D]

        dt = _softplus(dt_raw + dtb_ref[...])
        a = dt * (-LOG2_E * jnp.exp(alog_ref[...]))
        a_cs = _dot_sel(tril, a)
        a_cs_t = a_cs.T
        dt_e = _mm(dt.astype(BF16), hsel)
        acs_e = _a_dot_sel(a_cs, hsel)
        ea_e = jnp.exp2(acs_e)
        ds_e = jnp.exp2(acs_e[L - 1:L, :] - acs_e)

        x_dt = xs * dt_e
        x_b = x_dt.astype(BF16)
        x_dec = (x_dt * ds_e).astype(BF16)

        for g in range(SSD_GROUPS):
            gs = slice(g * gw, (g + 1) * gw)
            b_g = xbc[:, SSD_D + g * SSD_STATE:SSD_D + (g + 1) * SSD_STATE].astype(BF16)
            c_off = SSD_D + SSD_GROUPS * SSD_STATE
            c_g = xbc[:, c_off + g * SSD_STATE:c_off + (g + 1) * SSD_STATE].astype(BF16)
            scores = _mm(c_g, b_g, NT)
            state = st_ref[g]
            y_off = _mm(c_g, state.astype(BF16)) * ea_e[:, gs]
            st_ref[g] = state * ea_e[L - 1:L, gs] + _mm(b_g, x_dec[:, gs], TN)
            y_parts = []
            for j in range(gw // LANES):
                h0 = g * (SSD_HEADS // SSD_GROUPS) + 2 * j
                ms = []
                for h in (h0, h0 + 1):
                    seg = a_cs[:, h:h + 1] - a_cs_t[h:h + 1, :]
                    dec = jnp.exp2(jnp.where(causal, seg, -jnp.inf))
                    ms.append((scores * dec).astype(BF16))
                xp = x_b[:, h0 * SSD_HEAD_DIM:h0 * SSD_HEAD_DIM + LANES]
                zero = jnp.zeros_like(xp)
                x_bd = jnp.concatenate([jnp.where(lane_lo, xp, zero), jnp.where(lane_lo, zero, xp)], axis=0)
                y_parts.append(_mm(jnp.concatenate(ms, axis=1), x_bd))
            y = jnp.concatenate(y_parts, axis=1) + y_off
            y = y + dsk_ref[:, gs] * xs[:, gs]
            y = y * _silu(z[:, gs])
            y = y * lax.rsqrt(jnp.mean(y * y, axis=-1, keepdims=True) + NORM_EPS)
            o_ref[rs, gs] = (y * ng_ref[:, gs]).astype(o_ref.dtype)
    return chunk


def _seg_sum(x, lane_lo):
    s_lo = jnp.sum(jnp.where(lane_lo, x, 0.0), axis=-1, keepdims=True)
    s_hi = jnp.sum(jnp.where(lane_lo, 0.0, x), axis=-1, keepdims=True)
    return jnp.where(lane_lo, s_lo, s_hi)


def _stack_heads(x, lane_lo):
    zero = jnp.zeros_like(x)
    return jnp.concatenate([jnp.where(lane_lo, x, zero), jnp.where(lane_lo, zero, x)], axis=0)


def _unit_lower_inverse_many(a_list, row, col, eye):
    blk8 = row // 8 == col // 8
    a8f = [jnp.where(blk8, a, 0.0) for a in a_list]
    a8 = [x.astype(BF16) for x in a8f]
    t = [eye + x for x in a8f]
    a2 = [_mm(x, x).astype(BF16) for x in a8]
    t = [ti + _mm(ti.astype(BF16), x) for ti, x in zip(t, a2)]
    a4 = [_mm(x, x).astype(BF16) for x in a2]
    t = [ti + _mm(ti.astype(BF16), x) for ti, x in zip(t, a4)]
    n = a_list[0].shape[0]
    for s in (8, 16, 32):
        lower_left = (row // (2 * s) == col // (2 * s)) & ((row // s) % 2 == 1) & ((col // s) % 2 == 0)
        second = [slice(r0 + s, r0 + 2 * s) for r0 in range(0, n, 2 * s)]
        tb = [ti.astype(BF16) for ti in t]
        t2 = [jnp.concatenate([ti[r, :] for r in second], axis=0) for ti in t]
        x = [_mm(t2i.astype(BF16), jnp.where(lower_left, a, 0.0).astype(BF16)).astype(BF16)
             for t2i, a in zip(t2, a_list)]
        t2 = [t2i + _mm(xi, tbi) for t2i, xi, tbi in zip(t2, x, tb)]
        t = [jnp.concatenate([piece for m, r in enumerate(second)
                              for piece in (ti[r.start - s:r.start, :], t2i[m * s:(m + 1) * s, :])], axis=0)
             for ti, t2i in zip(t, t2)]
    return t


def _rwkv_parts(p_ref, w0_ref, w2_ref, a0_ref, a2_ref, g2_ref, kk_ref, ka_ref, rk_ref,
                lnw_ref, lnb_ref, o_ref, st_ref, pre_ref):
    L = RWKV_CHUNK
    TB = RWKV_BLOCK
    H2 = 2 * L
    D = RWKV_D
    n_pairs = D // PAIR

    @pl.when(pl.program_id(1) == 0)
    def _():
        st_ref[...] = jnp.zeros(st_ref.shape, F32)

    k = p_ref[:, D:2 * D]
    wa = p_ref[:, 3 * D:3 * D + LANES]
    g_lo = p_ref[:, 3 * D + LANES:3 * D + 2 * LANES]
    lw = -EXP_M_HALF * _sigmoid(w0_ref[...] + _dot1(jnp.tanh(wa), w2_ref[...]))
    alr = _sigmoid(a0_ref[...] + _dot1(wa, a2_ref[...]))
    CB = 4 * L
    blk_tril = (_iota2((CB, CB), 0) >= _iota2((CB, CB), 1)) & (_iota2((CB, CB), 0) // L == _iota2((CB, CB), 1) // L)
    blk_tril = jnp.where(blk_tril, 1.0, 0.0).astype(BF16)
    K_, KK_, ALR_, LW_, CS_, G_ = range(6)
    v_cols = lambda sl: slice(2 * D + sl.start, 2 * D + sl.stop)
    pre_ref[K_] = k * (1.0 + (alr - 1.0) * ka_ref[...])
    pre_ref[KK_] = k * kk_ref[...]
    pre_ref[ALR_] = alr
    pre_ref[LW_] = lw
    for r0 in range(0, TB, CB):
        pre_ref[CS_, r0:r0 + CB, :] = _dot_sel2(blk_tril, lw[r0:r0 + CB, :])
    pre_ref[G_] = _dot1(_sigmoid(g_lo), g2_ref[...])

    lane_lo = _iota2((L, PAIR), 1) < RWKV_HEAD_DIM
    row = _iota2((H2, H2), 0)
    col = _iota2((H2, H2), 1)
    eye = jnp.where(row == col, 1.0, 0.0)
    same_head = row // L == col // L
    strict = same_head & (row % L > col % L)
    incl = same_head & (row % L >= col % L)
    strict_incl = jnp.concatenate([strict, incl], axis=0)

    def independent_part(chunks):
        lhs_a, v_t, a_ab, a_k, lr_arb, hat_bk, p_all = ([] for _ in range(7))
        for c, q in [(c, q) for c in chunks for q in range(n_pairs)]:
            rs = slice(c * L, (c + 1) * L)
            sl = slice(q * PAIR, (q + 1) * PAIR)
            cs_p = pre_ref[CS_, rs, sl]
            cs_last = pre_ref[CS_, (c + 1) * L - 1:(c + 1) * L, sl]
            p_inv = jnp.exp(-cs_p)
            p_end = jnp.exp(cs_last - cs_p)
            kk_p = pre_ref[KK_, rs, sl]
            kk_n = kk_p * lax.rsqrt(jnp.maximum(_seg_sum(kk_p * kk_p, lane_lo), 1e-24))
            b_p = kk_n * pre_ref[ALR_, rs, sl]
            k_p = pre_ref[K_, rs, sl]
            la = _stack_heads(-kk_n * jnp.exp(cs_p - pre_ref[LW_, rs, sl]), lane_lo).astype(BF16)
            lr = _stack_heads(p_ref[rs, sl] * jnp.exp(cs_p), lane_lo).astype(BF16)
            bt = (b_p * p_inv).astype(BF16)
            kt = (k_p * p_inv).astype(BF16)
            g = _mm(jnp.concatenate([la, lr], axis=0), jnp.concatenate([bt, bt, kt, kt], axis=0), NT)
            a_ab.append(jnp.where(strict, g[0:H2, 0:H2], 0.0))
            a_k.append(jnp.where(strict_incl, g[:, H2:], 0.0).astype(BF16))
            lr_arb.append(jnp.concatenate([lr, jnp.where(incl, g[H2:, 0:H2], 0.0).astype(BF16)], axis=1))
            lhs_a.append(la)
            hat_bk.append(jnp.concatenate([_stack_heads(b_p * p_end, lane_lo), _stack_heads(k_p * p_end, lane_lo)],
                                          axis=0).astype(BF16))
            p_all.append(jnp.exp(cs_last))
            v_t.append(_stack_heads(p_ref[rs, v_cols(sl)], lane_lo).T.astype(BF16))
        t_inv = [t.astype(BF16) for t in _unit_lower_inverse_many(a_ab, row, col, eye)]
        t_a = [_mm(t, la).astype(BF16) for t, la in zip(t_inv, lhs_a)]
        both = [_mm(vt, a, NT) for vt, a in zip(v_t, a_k)]
        av_t = [x[:, 0:H2].astype(BF16) for x in both]
        arkv_t = [x[:, H2:] for x in both]
        tav_t = [_mm(x, t, NT) for x, t in zip(av_t, t_inv)]
        return dict(t_a=t_a, tav_t=tav_t, arkv_t=arkv_t, lr_arb=lr_arb, hat_bk=hat_bk, v_t=v_t, p_all=p_all)

    def state_part(chunks, d):
        for ci, c in enumerate(chunks):
            rs = slice(c * L, (c + 1) * L)
            idx = [ci * n_pairs + q for q in range(n_pairs)]
            s0 = [st_ref[q] for q in range(n_pairs)]
            s0b = [s.astype(BF16) for s in s0]
            u_tb = [(_mm(s0b[q], d["t_a"][i], NT) + d["tav_t"][i]).astype(BF16) for q, i in enumerate(idx)]
            for q, i in enumerate(idx):
                st_ref[q] = s0[q] * d["p_all"][i] + _mm(jnp.concatenate([u_tb[q], d["v_t"][i]], axis=1),
                                                        d["hat_bk"][i])
            y_t = [_mm(jnp.concatenate([s0b[q], u_tb[q]], axis=1), d["lr_arb"][i], NT) + d["arkv_t"][i]
                   for q, i in enumerate(idx)]
            for q in range(n_pairs):
                sl = slice(q * PAIR, (q + 1) * PAIR)
                y_st = y_t[q].T
                y = y_st[0:L, :] + y_st[L:H2, :]
                mean = _seg_sum(y, lane_lo) * (1.0 / RWKV_HEAD_DIM)
                dev = y - mean
                var = _seg_sum(dev * dev, lane_lo) * (1.0 / RWKV_HEAD_DIM)
                yn = dev * lax.rsqrt(var + RWKV_GN_EPS) * lnw_ref[:, sl] + lnb_ref[:, sl]
                v_p = p_ref[rs, v_cols(sl)]
                bonus = _seg_sum(p_ref[rs, sl] * pre_ref[K_, rs, sl] * rk_ref[:, sl], lane_lo) * v_p
                o_ref[rs, sl] = ((yn + bonus) * pre_ref[G_, rs, sl]).astype(o_ref.dtype)

    groups = [list(range(c0, c0 + RWKV_GROUP)) for c0 in range(0, TB // L, RWKV_GROUP)]
    return groups, independent_part, state_part


def _mixers_kernel(ps_ref, pr_ref, dtb_ref, alog_ref, dsk_ref, ng_ref,
                   w0_ref, w2_ref, a0_ref, a2_ref, g2_ref, kk_ref, ka_ref, rk_ref, lnw_ref, lnb_ref,
                   os_ref, or_ref, sst_ref, rst_ref, pre_ref):
    ssd_chunk = _ssd_parts(ps_ref, dtb_ref, alog_ref, dsk_ref, ng_ref, os_ref, sst_ref)
    groups, independent_part, state_part = _rwkv_parts(
        pr_ref, w0_ref, w2_ref, a0_ref, a2_ref, g2_ref, kk_ref, ka_ref, rk_ref, lnw_ref, lnb_ref,
        or_ref, rst_ref, pre_ref)
    ssd_chunks = iter(range(SSD_BLOCK // SSD_CHUNK))
    per_group = (SSD_BLOCK // SSD_CHUNK) // len(groups)
    ready = []
    for g in groups:
        ready.append(independent_part(g))
        for _ in range(per_group):
            ssd_chunk(next(ssd_chunks))
    for c in ssd_chunks:
        ssd_chunk(c)
    for g, d in zip(groups, ready):
        state_part(g, d)


def _mixers(p_ssd, p_rwkv, dtb, alog, dsk, ng, w0, w2p, a0, a2p, g2, k_k, k_a, r_k, ln_w, ln_b, batch, seq):
    assert SSD_BLOCK == RWKV_BLOCK
    TB = RWKV_BLOCK
    nb = seq // TB
    const = lambda b, c: (0, 0)
    rows = lambda b, c: (b * nb + c, 0)
    vec = pl.BlockSpec((1, RWKV_D), const)
    return pl.pallas_call(
        _mixers_kernel,
        name="mixers",
        grid=(batch, nb),
        in_specs=[
            pl.BlockSpec((TB, SSD_PCOLS), rows),
            pl.BlockSpec((TB, RWKV_COLS), rows),
            pl.BlockSpec((1, LANES), const),
            pl.BlockSpec((1, LANES), const),
            pl.BlockSpec((1, SSD_D), const),
            pl.BlockSpec((1, SSD_D), const),
            vec,
            pl.BlockSpec((LANES, RWKV_D), const),
            vec,
            pl.BlockSpec((LANES, RWKV_D), const),
            pl.BlockSpec((GATE_LORA, RWKV_D), const),
            vec, vec, vec, vec, vec,
        ],
        out_specs=[pl.BlockSpec((TB, SSD_D), rows), pl.BlockSpec((TB, RWKV_D), rows)],
        out_shape=[jax.ShapeDtypeStruct((batch * seq, SSD_D), BF16),
                   jax.ShapeDtypeStruct((batch * seq, RWKV_D), BF16)],
        scratch_shapes=[
            pltpu.VMEM((SSD_GROUPS, SSD_STATE, SSD_D // SSD_GROUPS), F32),
            pltpu.VMEM((RWKV_D // PAIR, PAIR, PAIR), F32),
            pltpu.VMEM((6, TB, RWKV_D), F32),
        ],
        compiler_params=pltpu.CompilerParams(
            dimension_semantics=("arbitrary", "arbitrary"), vmem_limit_bytes=VMEM_LIMIT),
    )(p_ssd, p_rwkv, dtb, alog, dsk, ng, w0, w2p, a0, a2p, g2, k_k, k_a, r_k, ln_w, ln_b)


def _ffn_kernel(ys_ref, yr_ref, x_ref, wo_ref, g1_ref, g2_ref, wup_ref, cw_ref, cb_ref, wdn_ref, g3_ref,
                o_ref, ubuf_ref, *, tm):
    @pl.when(pl.program_id(1) == 0)
    def _():
        ubuf_ref[...] = jnp.zeros(ubuf_ref.shape, F32)

    chunks = [(c0, min(FFN_COLS, D_FF - c0)) for c0 in range(0, D_FF, FFN_COLS)]

    def rows_part(rs):
        mix = _mm(ys_ref[rs, :], wo_ref[0:SSD_D, :]) + _mm(yr_ref[rs, :], wo_ref[SSD_D:, :])
        h = x_ref[rs, :] + _rms(mix, g1_ref[...])
        hn = _rms(h, g2_ref[...]).astype(BF16)

        def up(j):
            c0, width = chunks[j]
            return [_mm(hn, wup_ref[:, c:c + width]) for c in (c0, D_FF + c0)]

        def conv(u, c0):
            cs = slice(c0, c0 + u.shape[1])
            tail = ubuf_ref[:, cs]
            out = cb_ref[:, cs] + cw_ref[FFN_CONV - 1:FFN_CONV, cs] * u
            for k in range(1, FFN_CONV):
                out = out + cw_ref[FFN_CONV - 1 - k:FFN_CONV - k, cs] * _shift_rows(u, tail, k)
            ubuf_ref[:, cs] = u[u.shape[0] - SUBLANES:, :]
            return out

        n_steps = len(chunks)
        u_next = up(0)
        acts = []
        for j in range(n_steps):
            u_gate, u_val = u_next
            if j + 1 < n_steps:
                u_next = up(j + 1)
            gate = conv(u_gate, chunks[j][0])
            val = conv(u_val, D_FF + chunks[j][0])
            acts.append((_silu(gate) * val).astype(BF16))
        f = _mm(jnp.concatenate(acts, axis=1), wdn_ref[...])
        o_ref[rs, :] = h + _rms(f, g3_ref[...])

    for r0 in range(0, tm, FFN_PART):
        rows_part(slice(r0, r0 + FFN_PART))


def _ffn(ys, yr, x2, wo, g1, g2, wup, cw, cb, wdn, g3, batch, seq, tm):
    nb = seq // tm
    const = lambda b, i: (0, 0)
    rows = lambda b, i: (b * nb + i, 0)
    res = functools.partial(pl.BlockSpec, index_map=const, pipeline_mode=pl.Buffered(1))
    return pl.pallas_call(
        functools.partial(_ffn_kernel, tm=tm),
        name="outproj_ffn",
        grid=(batch, nb),
        in_specs=[
            pl.BlockSpec((tm, SSD_D), rows),
            pl.BlockSpec((tm, RWKV_D), rows),
            pl.BlockSpec((tm, D_MODEL), rows),
            res((SSD_D + RWKV_D, D_MODEL)),
            pl.BlockSpec((1, D_MODEL), const),
            pl.BlockSpec((1, D_MODEL), const),
            res((D_MODEL, 2 * D_FF)),
            pl.BlockSpec((FFN_CONV, 2 * D_FF), const),
            pl.BlockSpec((1, 2 * D_FF), const),
            res((D_FF, D_MODEL)),
            pl.BlockSpec((1, D_MODEL), const),
        ],
        out_specs=pl.BlockSpec((tm, D_MODEL), rows),
        out_shape=jax.ShapeDtypeStruct((batch * seq, D_MODEL), F32),
        scratch_shapes=[
            pltpu.VMEM((SUBLANES, 2 * D_FF), F32),
        ],
        compiler_params=pltpu.CompilerParams(
            dimension_semantics=("arbitrary", "arbitrary"), vmem_limit_bytes=VMEM_LIMIT),
    )(ys, yr, x2, wo, g1, g2, wup, cw, cb, wdn, g3)


def _pad_lanes(v):
    return jnp.pad(v.astype(F32), (0, LANES - v.shape[0]))[None, :]


def _layer(h2, batch, seq, pre_mix_norm, w_in, ssd_conv_w, ssd_conv_b, ssd_dt_bias, ssd_a_log, ssd_d, ssd_norm,
           rwkv_mu, rwkv_w0, rwkv_w2, rwkv_a0, rwkv_a2, rwkv_g2, rwkv_k_k, rwkv_k_a, rwkv_r_k,
           rwkv_ln_w, rwkv_ln_b, w_out, post_mix_norm, pre_ffn_norm, ffn_w_up, ffn_conv_w,
           ffn_conv_b, ffn_w_down, post_ffn_norm):
    row = lambda v: v.astype(F32).reshape(1, -1)

    w_in = w_in.astype(BF16)
    w_ssd = jnp.pad(w_in[:, :SSD_COLS], ((0, 0), (0, SSD_PCOLS - SSD_COLS)))
    i1 = RWKV_D
    i2 = i1 + DECAY_LORA
    i3 = i2 + RWKV_D
    i4 = i3 + RWKV_D
    i5 = i4 + AAA_LORA
    perm = lambda t: jnp.concatenate(
        [t[..., 0:i1], t[..., i2:i3], t[..., i3:i4], t[..., i1:i2], t[..., i4:i5], t[..., i5:]], axis=-1)
    w_rwkv = perm(w_in[:, SSD_COLS:])
    mu = perm(rwkv_mu).astype(F32).reshape(1, -1)
    w2p = jnp.concatenate([rwkv_w2, jnp.zeros((AAA_LORA, RWKV_D), rwkv_w2.dtype)], axis=0).astype(BF16)
    a2p = jnp.concatenate([jnp.zeros((DECAY_LORA, RWKV_D), rwkv_a2.dtype), rwkv_a2], axis=0).astype(BF16)

    p_ssd, p_rwkv = _inproj(h2, row(pre_mix_norm), w_ssd, w_rwkv, ssd_conv_w.astype(F32), row(ssd_conv_b), mu,
                            seq, tm=INPROJ_ROWS)
    y_ssd, y_rwkv = _mixers(
        p_ssd, p_rwkv, _pad_lanes(ssd_dt_bias), _pad_lanes(ssd_a_log), row(jnp.repeat(ssd_d, SSD_HEAD_DIM)),
        row(ssd_norm), row(rwkv_w0), w2p, row(rwkv_a0), a2p, rwkv_g2.astype(BF16), row(rwkv_k_k),
        row(rwkv_k_a), row(rwkv_r_k), row(rwkv_ln_w), row(rwkv_ln_b), batch, seq)
    return _ffn(y_ssd, y_rwkv, h2, w_out.astype(BF16), row(post_mix_norm), row(pre_ffn_norm),
                ffn_w_up.astype(BF16), ffn_conv_w.astype(F32), row(ffn_conv_b), ffn_w_down.astype(BF16),
                row(post_ffn_norm), batch, seq, tm=FFN_ROWS)


def kernel(x, pre_mix_norm, w_in, ssd_conv_w, ssd_conv_b, ssd_dt_bias, ssd_a_log, ssd_d, ssd_norm, rwkv_mu, rwkv_w0, rwkv_w2, rwkv_a0, rwkv_a2, rwkv_g2, rwkv_k_k, rwkv_k_a, rwkv_r_k, rwkv_ln_w, rwkv_ln_b, w_out, post_mix_norm, pre_ffn_norm, ffn_w_up, ffn_conv_w, ffn_conv_b, ffn_w_down, post_ffn_norm):
    batch, seq, d = x.shape
    h2 = x.reshape(batch * seq, d)
    params = (pre_mix_norm, w_in, ssd_conv_w, ssd_conv_b, ssd_dt_bias, ssd_a_log, ssd_d, ssd_norm, rwkv_mu,
              rwkv_w0, rwkv_w2, rwkv_a0, rwkv_a2, rwkv_g2, rwkv_k_k, rwkv_k_a, rwkv_r_k, rwkv_ln_w, rwkv_ln_b,
              w_out, post_mix_norm, pre_ffn_norm, ffn_w_up, ffn_conv_w, ffn_conv_b, ffn_w_down, post_ffn_norm)
    for l in range(pre_mix_norm.shape[0]):
        h2 = _layer(h2, batch, seq, *(t[l] for t in params))
    return h2.reshape(batch, seq, d)
```

```python
import functools

import jax
import jax.numpy as jnp
from jax import lax
from jax.experimental import pallas as pl
from jax.experimental.pallas import tpu as pltpu

F32 = jnp.float32
BF16 = jnp.bfloat16

D_MODEL = 1024
SSD_HEADS = 8
SSD_HEAD_DIM = 64
SSD_D = SSD_HEADS * SSD_HEAD_DIM
SSD_GROUPS = 2
SSD_STATE = 128
SSD_CONV = 4
SSD_CHUNK = 128
SSD_CONV_DIM = SSD_D + 2 * SSD_GROUPS * SSD_STATE
SSD_COLS = SSD_D + SSD_CONV_DIM + SSD_HEADS
RWKV_HEADS = 8
RWKV_HEAD_DIM = 64
RWKV_D = RWKV_HEADS * RWKV_HEAD_DIM
DECAY_LORA = 64
AAA_LORA = 64
GATE_LORA = 128
RWKV_COLS = 3 * RWKV_D + DECAY_LORA + AAA_LORA + GATE_LORA
RWKV_GN_EPS = 64e-5
D_FF = 2816
FFN_CONV = 3
NORM_EPS = 1e-6
LOG2_E = 1.4426950408889634
EXP_M_HALF = 0.6065306597126334

LANES = 128
SUBLANES = 8
SSD_PCOLS = SSD_D + SSD_CONV_DIM + LANES
SSD_BLOCK = 512
RWKV_CHUNK = 64
RWKV_BLOCK = 512
INPROJ_COLS = 256
RWKV_GROUP = 4
PAIR = 2 * RWKV_HEAD_DIM
FFN_COLS = 768
INPROJ_ROWS = 1024
FFN_ROWS = 1024
VMEM_LIMIT = 56 * 1024 * 1024

NN = (((1,), (0,)), ((), ()))
NT = (((1,), (1,)), ((), ()))
TN = (((0,), (0,)), ((), ()))


def _mm(a, b, dims=NN):
    return lax.dot_general(a, b, dims, preferred_element_type=F32)


def _dot1(a, b, dims=NN):
    return _mm(a.astype(BF16), b.astype(BF16), dims)


def _split3(a):
    a1 = a.astype(BF16)
    r1 = a - a1.astype(F32)
    a2 = r1.astype(BF16)
    a3 = (r1 - a2.astype(F32)).astype(BF16)
    return a1, a2, a3


def _dot_sel(sel_bf16, a):
    a1, a2, a3 = _split3(a)
    return _mm(sel_bf16, a1) + (_mm(sel_bf16, a2) + _mm(sel_bf16, a3))


def _a_dot_sel(a, sel_bf16):
    a1, a2, a3 = _split3(a)
    return _mm(a1, sel_bf16) + (_mm(a2, sel_bf16) + _mm(a3, sel_bf16))


def _dot_sel2(sel_bf16, a):
    hi = a.astype(BF16)
    lo = (a - hi.astype(F32)).astype(BF16)
    return _mm(sel_bf16, hi) + _mm(sel_bf16, lo)


def _shift_rows(x, tail, k):
    sub = _iota2((SUBLANES, x.shape[1]), 0)
    r = pltpu.roll(x, k, axis=0)
    head = jnp.where(sub < k, pltpu.roll(tail, k, axis=0), r[0:SUBLANES, :])
    return jnp.concatenate([head, r[SUBLANES:, :]], axis=0)


def _rms(x, g):
    return x * lax.rsqrt(jnp.mean(x * x, axis=-1, keepdims=True) + NORM_EPS) * g


def _sigmoid(x):
    return 0.5 + 0.5 * jnp.tanh(0.5 * x)


def _silu(x):
    h = 0.5 * x
    return h + h * jnp.tanh(h)


def _softplus(x):
    return jnp.maximum(x, 0.0) + jnp.log(1.0 + jnp.exp(-jnp.abs(x)))


def _iota2(shape, axis):
    return lax.broadcasted_iota(jnp.int32, shape, axis)


def _inproj_kernel(x_ref, g_ref, ws_ref, wr_ref, cw_ref, cb_ref, mu_ref, ps_ref, pr_ref, tail_s_ref, tail_r_ref,
                   *, tm, blocks_per_seq):
    i = pl.program_id(0)

    @pl.when(i == 0)
    def _():
        tail_s_ref[...] = jnp.zeros(tail_s_ref.shape, F32)
        tail_r_ref[...] = jnp.zeros(tail_r_ref.shape, F32)

    first = i % blocks_per_seq == 0
    xb = _rms(x_ref[...], g_ref[...]).astype(BF16)
    xbc0 = SSD_D
    dt0 = SSD_D + SSD_CONV_DIM
    def conv_cols(c0):
        ts = slice(c0, c0 + INPROJ_COLS)
        cs = slice(xbc0 + c0, xbc0 + c0 + INPROJ_COLS)
        u = _mm(xb, ws_ref[:, cs])
        tail = jnp.where(first, 0.0, tail_s_ref[:, ts])
        acc = cb_ref[:, ts] + cw_ref[SSD_CONV - 1:SSD_CONV, ts] * u
        for k in range(1, SSD_CONV):
            acc = acc + cw_ref[SSD_CONV - 1 - k:SSD_CONV - k, ts] * _shift_rows(u, tail, k)
        tail_s_ref[:, ts] = u[tm - SUBLANES:tm, :]
        ps_ref[:, cs] = _silu(acc)

    def shift_cols(c0):
        cs = slice(c0, c0 + INPROJ_COLS)
        p = _mm(xb, wr_ref[:, cs])
        tail = jnp.where(first, 0.0, tail_r_ref[:, cs])
        prev = _shift_rows(p, tail, 1)
        tail_r_ref[:, cs] = p[tm - SUBLANES:tm, :]
        pr_ref[:, cs] = p + (prev - p) * mu_ref[:, cs]

    def plain_cols(lo, hi):
        ps_ref[:, lo:hi] = _mm(xb, ws_ref[:, lo:hi])

    conv = [functools.partial(conv_cols, c0) for c0 in range(0, SSD_CONV_DIM, INPROJ_COLS)]
    light = ([functools.partial(plain_cols, c0, c0 + INPROJ_COLS) for c0 in range(0, xbc0, INPROJ_COLS)]
             + [functools.partial(plain_cols, dt0, SSD_PCOLS)]
             + [functools.partial(shift_cols, c0) for c0 in range(0, RWKV_COLS, INPROJ_COLS)])
    per_conv = 2
    for j, step in enumerate(conv):
        step()
        for fill in light[j * per_conv:(j + 1) * per_conv]:
            fill()
    for fill in light[len(conv) * per_conv:]:
        fill()


def _inproj(x2, g, w_ssd, w_rwkv, cw, cb, mu, seq, tm):
    n = x2.shape[0]
    const = lambda i: (0, 0)
    return pl.pallas_call(
        functools.partial(_inproj_kernel, tm=tm, blocks_per_seq=seq // tm),
        name="inproj",
        grid=(n // tm,),
        in_specs=[
            pl.BlockSpec((tm, D_MODEL), lambda i: (i, 0)),
            pl.BlockSpec((1, D_MODEL), const),
            pl.BlockSpec((D_MODEL, SSD_PCOLS), const, pipeline_mode=pl.Buffered(1)),
            pl.BlockSpec((D_MODEL, RWKV_COLS), const, pipeline_mode=pl.Buffered(1)),
            pl.BlockSpec((SSD_CONV, SSD_CONV_DIM), const),
            pl.BlockSpec((1, SSD_CONV_DIM), const),
            pl.BlockSpec((1, RWKV_COLS), const),
        ],
        out_specs=[
            pl.BlockSpec((tm, SSD_PCOLS), lambda i: (i, 0)),
            pl.BlockSpec((tm, RWKV_COLS), lambda i: (i, 0)),
        ],
        out_shape=[
            jax.ShapeDtypeStruct((n, SSD_PCOLS), F32),
            jax.ShapeDtypeStruct((n, RWKV_COLS), F32),
        ],
        scratch_shapes=[
            pltpu.VMEM((SUBLANES, SSD_CONV_DIM), F32),
            pltpu.VMEM((SUBLANES, RWKV_COLS), F32),
        ],
        compiler_params=pltpu.CompilerParams(
            dimension_semantics=("arbitrary",), vmem_limit_bytes=VMEM_LIMIT),
    )(x2, g, w_ssd, w_rwkv, cw, cb, mu)


def _ssd_parts(p_ref, dtb_ref, alog_ref, dsk_ref, ng_ref, o_ref, st_ref):
    L = SSD_CHUNK
    gw = SSD_D // SSD_GROUPS

    @pl.when(pl.program_id(1) == 0)
    def _():
        st_ref[...] = jnp.zeros(st_ref.shape, F32)

    row = _iota2((L, L), 0)
    col = _iota2((L, L), 1)
    causal = row >= col
    tril = jnp.where(causal, 1.0, 0.0).astype(BF16)
    hsel = (_iota2((LANES, SSD_D), 1) // SSD_HEAD_DIM == _iota2((LANES, SSD_D), 0))
    hsel = jnp.where(hsel, 1.0, 0.0).astype(BF16)
    lane_lo = _iota2((L, LANES), 1) < SSD_HEAD_DIM
    xbc_cols = slice(SSD_D, SSD_D + SSD_CONV_DIM)

    def chunk(c):
        rs = slice(c * L, (c + 1) * L)
        z = p_ref[rs, 0:SSD_D]
        xbc = p_ref[rs, xbc_cols]
        dt_raw = p_ref[rs, SSD_D + SSD_CONV_DIM:SSD_PCOLS]
        xs = xbc[:, 0:SSD_D]

        dt = _softplus(dt_raw + dtb_ref[...])
        a = dt * (-LOG2_E * jnp.exp(alog_ref[...]))
        a_cs = _dot_sel(tril, a)
        a_cs_t = a_cs.T
        dt_e = _mm(dt.astype(BF16), hsel)
        acs_e = _a_dot_sel(a_cs, hsel)
        ea_e = jnp.exp2(acs_e)
        ds_e = jnp.exp2(acs_e[L - 1:L, :] - acs_e)

        x_dt = xs * dt_e
        x_b = x_dt.astype(BF16)
        x_dec = (x_dt * ds_e).astype(BF16)

        for g in range(SSD_GROUPS):
            gs = slice(g * gw, (g + 1) * gw)
            b_g = xbc[:, SSD_D + g * SSD_STATE:SSD_D + (g + 1) * SSD_STATE].astype(BF16)
            c_off = SSD_D + SSD_GROUPS * SSD_STATE
            c_g = xbc[:, c_off + g * SSD_STATE:c_off + (g + 1) * SSD_STATE].astype(BF16)
            scores = _mm(c_g, b_g, NT)
            state = st_ref[g]
            y_off = _mm(c_g, state.astype(BF16)) * ea_e[:, gs]
            st_ref[g] = state * ea_e[L - 1:L, gs] + _mm(b_g, x_dec[:, gs], TN)
            y_parts = []
            for j in range(gw // LANES):
                h0 = g * (SSD_HEADS // SSD_GROUPS) + 2 * j
                ms = []
                for h in (h0, h0 + 1):
                    seg = a_cs[:, h:h + 1] - a_cs_t[h:h + 1, :]
                    dec = jnp.exp2(jnp.where(causal, seg, -jnp.inf))
                    ms.append((scores * dec).astype(BF16))
                xp = x_b[:, h0 * SSD_HEAD_DIM:h0 * SSD_HEAD_DIM + LANES]
                zero = jnp.zeros_like(xp)
                x_bd = jnp.concatenate([jnp.where(lane_lo, xp, zero), jnp.where(lane_lo, zero, xp)], axis=0)
                y_parts.append(_mm(jnp.concatenate(ms, axis=1), x_bd))
            y = jnp.concatenate(y_parts, axis=1) + y_off
            y = y + dsk_ref[:, gs] * xs[:, gs]
            y = y * _silu(z[:, gs])
            y = y * lax.rsqrt(jnp.mean(y * y, axis=-1, keepdims=True) + NORM_EPS)
            o_ref[rs, gs] = (y * ng_ref[:, gs]).astype(o_ref.dtype)
    return chunk


def _seg_sum(x, lane_lo):
    s_lo = jnp.sum(jnp.where(lane_lo, x, 0.0), axis=-1, keepdims=True)
    s_hi = jnp.sum(jnp.where(lane_lo, 0.0, x), axis=-1, keepdims=True)
    return jnp.where(lane_lo, s_lo, s_hi)


def _stack_heads(x, lane_lo):
    zero = jnp.zeros_like(x)
    return jnp.concatenate([jnp.where(lane_lo, x, zero), jnp.where(lane_lo, zero, x)], axis=0)


def _unit_lower_inverse_many(a_list, row, col, eye):
    blk8 = row // 8 == col // 8
    a8f = [jnp.where(blk8, a, 0.0) for a in a_list]
    a8 = [x.astype(BF16) for x in a8f]
    t = [eye + x for x in a8f]
    a2 = [_mm(x, x).astype(BF16) for x in a8]
    t = [ti + _mm(ti.astype(BF16), x) for ti, x in zip(t, a2)]
    a4 = [_mm(x, x).astype(BF16) for x in a2]
    t = [ti + _mm(ti.astype(BF16), x) for ti, x in zip(t, a4)]
    n = a_list[0].shape[0]
    for s in (8, 16, 32):
        lower_left = (row // (2 * s) == col // (2 * s)) & ((row // s) % 2 == 1) & ((col // s) % 2 == 0)
        second = [slice(r0 + s, r0 + 2 * s) for r0 in range(0, n, 2 * s)]
        tb = [ti.astype(BF16) for ti in t]
        t2 = [jnp.concatenate([ti[r, :] for r in second], axis=0) for ti in t]
        x = [_mm(t2i.astype(BF16), jnp.where(lower_left, a, 0.0).astype(BF16)).astype(BF16)
             for t2i, a in zip(t2, a_list)]
        t2 = [t2i + _mm(xi, tbi) for t2i, xi, tbi in zip(t2, x, tb)]
        t = [jnp.concatenate([piece for m, r in enumerate(second)
                              for piece in (ti[r.start - s:r.start, :], t2i[m * s:(m + 1) * s, :])], axis=0)
             for ti, t2i in zip(t, t2)]
    return t


def _rwkv_parts(p_ref, w0_ref, w2_ref, a0_ref, a2_ref, g2_ref, kk_ref, ka_ref, rk_ref,
                lnw_ref, lnb_ref, o_ref, st_ref, pre_ref):
    L = RWKV_CHUNK
    TB = RWKV_BLOCK
    H2 = 2 * L
    D = RWKV_D
    n_pairs = D // PAIR

    @pl.when(pl.program_id(1) == 0)
    def _():
        st_ref[...] = jnp.zeros(st_ref.shape, F32)

    CB = 4 * L
    blk_tril = (_iota2((CB, CB), 0) >= _iota2((CB, CB), 1)) & (_iota2((CB, CB), 0) // L == _iota2((CB, CB), 1) // L)
    blk_tril = jnp.where(blk_tril, 1.0, 0.0).astype(BF16)
    K_, KK_, ALR_, LW_, CS_, G_ = range(6)
    v_cols = lambda sl: slice(2 * D + sl.start, 2 * D + sl.stop)
    for r0 in range(0, TB, CB):
        rows = slice(r0, r0 + CB)
        k = p_ref[rows, D:2 * D]
        wa = p_ref[rows, 3 * D:3 * D + LANES]
        g_lo = p_ref[rows, 3 * D + LANES:3 * D + 2 * LANES]
        lw = -EXP_M_HALF * _sigmoid(w0_ref[...] + _dot1(jnp.tanh(wa), w2_ref[...]))
        alr = _sigmoid(a0_ref[...] + _dot1(wa, a2_ref[...]))
        pre_ref[K_, rows, :] = k * (1.0 + (alr - 1.0) * ka_ref[...])
        pre_ref[KK_, rows, :] = k * kk_ref[...]
        pre_ref[ALR_, rows, :] = alr
        pre_ref[LW_, rows, :] = lw
        pre_ref[CS_, rows, :] = _dot_sel2(blk_tril, lw)
        pre_ref[G_, rows, :] = _dot1(_sigmoid(g_lo), g2_ref[...])

    lane_lo = _iota2((L, PAIR), 1) < RWKV_HEAD_DIM
    row = _iota2((H2, H2), 0)
    col = _iota2((H2, H2), 1)
    eye = jnp.where(row == col, 1.0, 0.0)
    same_head = row // L == col // L
    strict = same_head & (row % L > col % L)
    incl = same_head & (row % L >= col % L)
    strict_incl = jnp.concatenate([strict, incl], axis=0)

    def independent_part(chunks):
        lhs_a, v_t, a_ab, a_k, lr_arb, hat_bk, p_all = ([] for _ in range(7))
        for c, q in [(c, q) for c in chunks for q in range(n_pairs)]:
            rs = slice(c * L, (c + 1) * L)
            sl = slice(q * PAIR, (q + 1) * PAIR)
            cs_p = pre_ref[CS_, rs, sl]
            cs_last = pre_ref[CS_, (c + 1) * L - 1:(c + 1) * L, sl]
            p_inv = jnp.exp(-cs_p)
            p_end = jnp.exp(cs_last - cs_p)
            kk_p = pre_ref[KK_, rs, sl]
            kk_n = kk_p * lax.rsqrt(jnp.maximum(_seg_sum(kk_p * kk_p, lane_lo), 1e-24))
            b_p = kk_n * pre_ref[ALR_, rs, sl]
            k_p = pre_ref[K_, rs, sl]
            la = _stack_heads(-kk_n * jnp.exp(cs_p - pre_ref[LW_, rs, sl]), lane_lo).astype(BF16)
            lr = _stack_heads(p_ref[rs, sl] * jnp.exp(cs_p), lane_lo).astype(BF16)
            bt = (b_p * p_inv).astype(BF16)
            kt = (k_p * p_inv).astype(BF16)
            g = _mm(jnp.concatenate([la, lr], axis=0), jnp.concatenate([bt, bt, kt, kt], axis=0), NT)
            a_ab.append(jnp.where(strict, g[0:H2, 0:H2], 0.0))
            a_k.append(jnp.where(strict_incl, g[:, H2:], 0.0).astype(BF16))
            lr_arb.append(jnp.concatenate([lr, jnp.where(incl, g[H2:, 0:H2], 0.0).astype(BF16)], axis=1))
            lhs_a.append(la)
            hat_bk.append(jnp.concatenate([_stack_heads(b_p * p_end, lane_lo), _stack_heads(k_p * p_end, lane_lo)],
                                          axis=0).astype(BF16))
            p_all.append(jnp.exp(cs_last))
            v_t.append(_stack_heads(p_ref[rs, v_cols(sl)], lane_lo).T.astype(BF16))
        t_inv = [t.astype(BF16) for t in _unit_lower_inverse_many(a_ab, row, col, eye)]
        t_a = [_mm(t, la).astype(BF16) for t, la in zip(t_inv, lhs_a)]
        both = [_mm(vt, a, NT) for vt, a in zip(v_t, a_k)]
        av_t = [x[:, 0:H2].astype(BF16) for x in both]
        arkv_t = [x[:, H2:] for x in both]
        tav_t = [_mm(x, t, NT) for x, t in zip(av_t, t_inv)]
        return dict(t_a=t_a, tav_t=tav_t, arkv_t=arkv_t, lr_arb=lr_arb, hat_bk=hat_bk, v_t=v_t, p_all=p_all)

    def state_part(chunks, d):
        for ci, c in enumerate(chunks):
            rs = slice(c * L, (c + 1) * L)
            idx = [ci * n_pairs + q for q in range(n_pairs)]
            s0 = [st_ref[q] for q in range(n_pairs)]
            s0b = [s.astype(BF16) for s in s0]
            u_tb = [(_mm(s0b[q], d["t_a"][i], NT) + d["tav_t"][i]).astype(BF16) for q, i in enumerate(idx)]
            for q, i in enumerate(idx):
                st_ref[q] = s0[q] * d["p_all"][i] + _mm(jnp.concatenate([u_tb[q], d["v_t"][i]], axis=1),
                                                        d["hat_bk"][i])
            y_t = [_mm(jnp.concatenate([s0b[q], u_tb[q]], axis=1), d["lr_arb"][i], NT) + d["arkv_t"][i]
                   for q, i in enumerate(idx)]
            for q in range(n_pairs):
                sl = slice(q * PAIR, (q + 1) * PAIR)
                y_st = y_t[q].T
                y = y_st[0:L, :] + y_st[L:H2, :]
                mean = _seg_sum(y, lane_lo) * (1.0 / RWKV_HEAD_DIM)
                dev = y - mean
                var = _seg_sum(dev * dev, lane_lo) * (1.0 / RWKV_HEAD_DIM)
                yn = dev * lax.rsqrt(var + RWKV_GN_EPS) * lnw_ref[:, sl] + lnb_ref[:, sl]
                v_p = p_ref[rs, v_cols(sl)]
                bonus = _seg_sum(p_ref[rs, sl] * pre_ref[K_, rs, sl] * rk_ref[:, sl], lane_lo) * v_p
                o_ref[rs, sl] = ((yn + bonus) * pre_ref[G_, rs, sl]).astype(o_ref.dtype)

    groups = [list(range(c0, c0 + RWKV_GROUP)) for c0 in range(0, TB // L, RWKV_GROUP)]
    return groups, independent_part, state_part


def _mixers_kernel(ps_ref, pr_ref, dtb_ref, alog_ref, dsk_ref, ng_ref,
                   w0_ref, w2_ref, a0_ref, a2_ref, g2_ref, kk_ref, ka_ref, rk_ref, lnw_ref, lnb_ref,
                   os_ref, or_ref, sst_ref, rst_ref, pre_ref):
    ssd_chunk = _ssd_parts(ps_ref, dtb_ref, alog_ref, dsk_ref, ng_ref, os_ref, sst_ref)
    groups, independent_part, state_part = _rwkv_parts(
        pr_ref, w0_ref, w2_ref, a0_ref, a2_ref, g2_ref, kk_ref, ka_ref, rk_ref, lnw_ref, lnb_ref,
        or_ref, rst_ref, pre_ref)
    ssd_chunks = iter(range(SSD_BLOCK // SSD_CHUNK))
    per_group = (SSD_BLOCK // SSD_CHUNK) // len(groups)
    ready = []
    for g in groups:
        ready.append(independent_part(g))
        for _ in range(per_group):
            ssd_chunk(next(ssd_chunks))
    for c in ssd_chunks:
        ssd_chunk(c)
    for g, d in zip(groups, ready):
        state_part(g, d)


def _mixers(p_ssd, p_rwkv, dtb, alog, dsk, ng, w0, w2p, a0, a2p, g2, k_k, k_a, r_k, ln_w, ln_b, batch, seq):
    assert SSD_BLOCK == RWKV_BLOCK
    TB = RWKV_BLOCK
    nb = seq // TB
    const = lambda b, c: (0, 0)
    rows = lambda b, c: (b * nb + c, 0)
    vec = pl.BlockSpec((1, RWKV_D), const)
    return pl.pallas_call(
        _mixers_kernel,
        name="mixers",
        grid=(batch, nb),
        in_specs=[
            pl.BlockSpec((TB, SSD_PCOLS), rows),
            pl.BlockSpec((TB, RWKV_COLS), rows),
            pl.BlockSpec((1, LANES), const),
            pl.BlockSpec((1, LANES), const),
            pl.BlockSpec((1, SSD_D), const),
            pl.BlockSpec((1, SSD_D), const),
            vec,
            pl.BlockSpec((LANES, RWKV_D), const),
            vec,
            pl.BlockSpec((LANES, RWKV_D), const),
            pl.BlockSpec((GATE_LORA, RWKV_D), const),
            vec, vec, vec, vec, vec,
        ],
        out_specs=[pl.BlockSpec((TB, SSD_D), rows), pl.BlockSpec((TB, RWKV_D), rows)],
        out_shape=[jax.ShapeDtypeStruct((batch * seq, SSD_D), BF16),
                   jax.ShapeDtypeStruct((batch * seq, RWKV_D), BF16)],
        scratch_shapes=[
            pltpu.VMEM((SSD_GROUPS, SSD_STATE, SSD_D // SSD_GROUPS), F32),
            pltpu.VMEM((RWKV_D // PAIR, PAIR, PAIR), F32),
            pltpu.VMEM((6, TB, RWKV_D), F32),
        ],
        compiler_params=pltpu.CompilerParams(
            dimension_semantics=("arbitrary", "arbitrary"), vmem_limit_bytes=VMEM_LIMIT),
    )(p_ssd, p_rwkv, dtb, alog, dsk, ng, w0, w2p, a0, a2p, g2, k_k, k_a, r_k, ln_w, ln_b)


def _ffn_kernel(ys_ref, yr_ref, x_ref, wo_ref, g1_ref, g2_ref, wup_ref, cw_ref, cb_ref, wdn_ref, g3_ref,
                o_ref, ubuf_ref, *, tm):
    @pl.when(pl.program_id(1) == 0)
    def _():
        ubuf_ref[...] = jnp.zeros(ubuf_ref.shape, F32)

    mix = _mm(ys_ref[...], wo_ref[0:SSD_D, :]) + _mm(yr_ref[...], wo_ref[SSD_D:, :])
    h = x_ref[...] + _rms(mix, g1_ref[...])
    hn = _rms(h, g2_ref[...]).astype(BF16)

    chunks = [(c0, min(FFN_COLS, D_FF - c0)) for c0 in range(0, D_FF, FFN_COLS)]

    def up(j):
        c0, width = chunks[j]
        return [_mm(hn, wup_ref[:, c:c + width]) for c in (c0, D_FF + c0)]

    def conv(u, c0):
        cs = slice(c0, c0 + u.shape[1])
        tail = ubuf_ref[:, cs]
        out = cb_ref[:, cs] + cw_ref[FFN_CONV - 1:FFN_CONV, cs] * u
        for k in range(1, FFN_CONV):
            out = out + cw_ref[FFN_CONV - 1 - k:FFN_CONV - k, cs] * _shift_rows(u, tail, k)
        ubuf_ref[:, cs] = u[tm - SUBLANES:tm, :]
        return out

    n_steps = len(chunks)
    u_next = up(0)
    acts = []
    for j in range(n_steps):
        u_gate, u_val = u_next
        if j + 1 < n_steps:
            u_next = up(j + 1)
        gate = conv(u_gate, chunks[j][0])
        val = conv(u_val, D_FF + chunks[j][0])
        acts.append((_silu(gate) * val).astype(BF16))
    f = _mm(jnp.concatenate(acts, axis=1), wdn_ref[...])
    o_ref[...] = h + _rms(f, g3_ref[...])


def _ffn(ys, yr, x2, wo, g1, g2, wup, cw, cb, wdn, g3, batch, seq, tm):
    nb = seq // tm
    const = lambda b, i: (0, 0)
    rows = lambda b, i: (b * nb + i, 0)
    res = functools.partial(pl.BlockSpec, index_map=const, pipeline_mode=pl.Buffered(1))
    return pl.pallas_call(
        functools.partial(_ffn_kernel, tm=tm),
        name="outproj_ffn",
        grid=(batch, nb),
        in_specs=[
            pl.BlockSpec((tm, SSD_D), rows),
            pl.BlockSpec((tm, RWKV_D), rows),
            pl.BlockSpec((tm, D_MODEL), rows),
            res((SSD_D + RWKV_D, D_MODEL)),
            pl.BlockSpec((1, D_MODEL), const),
            pl.BlockSpec((1, D_MODEL), const),
            res((D_MODEL, 2 * D_FF)),
            pl.BlockSpec((FFN_CONV, 2 * D_FF), const),
            pl.BlockSpec((1, 2 * D_FF), const),
            res((D_FF, D_MODEL)),
            pl.BlockSpec((1, D_MODEL), const),
        ],
        out_specs=pl.BlockSpec((tm, D_MODEL), rows),
        out_shape=jax.ShapeDtypeStruct((batch * seq, D_MODEL), F32),
        scratch_shapes=[
            pltpu.VMEM((SUBLANES, 2 * D_FF), F32),
        ],
        compiler_params=pltpu.CompilerParams(
            dimension_semantics=("arbitrary", "arbitrary"), vmem_limit_bytes=VMEM_LIMIT),
    )(ys, yr, x2, wo, g1, g2, wup, cw, cb, wdn, g3)


def _pad_lanes(v):
    return jnp.pad(v.astype(F32), (0, LANES - v.shape[0]))[None, :]


def _layer(h2, batch, seq, pre_mix_norm, w_in, ssd_conv_w, ssd_conv_b, ssd_dt_bias, ssd_a_log, ssd_d, ssd_norm,
           rwkv_mu, rwkv_w0, rwkv_w2, rwkv_a0, rwkv_a2, rwkv_g2, rwkv_k_k, rwkv_k_a, rwkv_r_k,
           rwkv_ln_w, rwkv_ln_b, w_out, post_mix_norm, pre_ffn_norm, ffn_w_up, ffn_conv_w,
           ffn_conv_b, ffn_w_down, post_ffn_norm):
    row = lambda v: v.astype(F32).reshape(1, -1)

    w_in = w_in.astype(BF16)
    w_ssd = jnp.pad(w_in[:, :SSD_COLS], ((0, 0), (0, SSD_PCOLS - SSD_COLS)))
    i1 = RWKV_D
    i2 = i1 + DECAY_LORA
    i3 = i2 + RWKV_D
    i4 = i3 + RWKV_D
    i5 = i4 + AAA_LORA
    perm = lambda t: jnp.concatenate(
        [t[..., 0:i1], t[..., i2:i3], t[..., i3:i4], t[..., i1:i2], t[..., i4:i5], t[..., i5:]], axis=-1)
    w_rwkv = perm(w_in[:, SSD_COLS:])
    mu = perm(rwkv_mu).astype(F32).reshape(1, -1)
    w2p = jnp.concatenate([rwkv_w2, jnp.zeros((AAA_LORA, RWKV_D), rwkv_w2.dtype)], axis=0).astype(BF16)
    a2p = jnp.concatenate([jnp.zeros((DECAY_LORA, RWKV_D), rwkv_a2.dtype), rwkv_a2], axis=0).astype(BF16)

    p_ssd, p_rwkv = _inproj(h2, row(pre_mix_norm), w_ssd, w_rwkv, ssd_conv_w.astype(F32), row(ssd_conv_b), mu,
                            seq, tm=INPROJ_ROWS)
    y_ssd, y_rwkv = _mixers(
        p_ssd, p_rwkv, _pad_lanes(ssd_dt_bias), _pad_lanes(ssd_a_log), row(jnp.repeat(ssd_d, SSD_HEAD_DIM)),
        row(ssd_norm), row(rwkv_w0), w2p, row(rwkv_a0), a2p, rwkv_g2.astype(BF16), row(rwkv_k_k),
        row(rwkv_k_a), row(rwkv_r_k), row(rwkv_ln_w), row(rwkv_ln_b), batch, seq)
    return _ffn(y_ssd, y_rwkv, h2, w_out.astype(BF16), row(post_mix_norm), row(pre_ffn_norm),
                ffn_w_up.astype(BF16), ffn_conv_w.astype(F32), row(ffn_conv_b), ffn_w_down.astype(BF16),
                row(post_ffn_norm), batch, seq, tm=FFN_ROWS)


def kernel(x, pre_mix_norm, w_in, ssd_conv_w, ssd_conv_b, ssd_dt_bias, ssd_a_log, ssd_d, ssd_norm, rwkv_mu, rwkv_w0, rwkv_w2, rwkv_a0, rwkv_a2, rwkv_g2, rwkv_k_k, rwkv_k_a, rwkv_r_k, rwkv_ln_w, rwkv_ln_b, w_out, post_mix_norm, pre_ffn_norm, ffn_w_up, ffn_conv_w, ffn_conv_b, ffn_w_down, post_ffn_norm):
    batch, seq, d = x.shape
    h2 = x.reshape(batch * seq, d)
    params = (pre_mix_norm, w_in, ssd_conv_w, ssd_conv_b, ssd_dt_bias, ssd_a_log, ssd_d, ssd_norm, rwkv_mu,
              rwkv_w0, rwkv_w2, rwkv_a0, rwkv_a2, rwkv_g2, rwkv_k_k, rwkv_k_a, rwkv_r_k, rwkv_ln_w, rwkv_ln_b,
              w_out, post_mix_norm, pre_ffn_norm, ffn_w_up, ffn_conv_w, ffn_conv_b, ffn_w_down, post_ffn_norm)
    for l in range(pre_mix_norm.shape[0]):
        h2 = _layer(h2, batch, seq, *(t[l] for t in params))
    return h2.reshape(batch, seq, d)
```
